```python
import functools
import jax, jax.numpy as jnp
from jax import lax
import numpy as np

D_MODEL = 2048
BATCH = 2
SEQ = 4096
DEPTH = 1
DEC_BATCH = 32
DEC_SEQ = 1
PAST_LEN = 16384
PAGE_SIZE = 128

N_HEADS = 16
HEAD_DIM = D_MODEL // N_HEADS
N_KV_HEADS = 4
KV_GROUP = N_HEADS // N_KV_HEADS
Q_WIDTH = N_HEADS * HEAD_DIM
KV_WIDTH = N_KV_HEADS * HEAD_DIM
MOBA_BLOCK = 256
MOBA_TOPK = 3
Q_CHUNK = 64
ROPE_THETA = 10000.0
CONV_WIDTH = D_MODEL // 2
CONV_K = 3
D_FF = -(-(8 * D_MODEL) // (3 * 256)) * 256
RMS_EPS = 1e-6
IN_SIZES = [Q_WIDTH, KV_WIDTH, KV_WIDTH, CONV_WIDTH, CONV_WIDTH, CONV_WIDTH, D_MODEL, D_MODEL]
N_IN = sum(IN_SIZES)
IN_SPLITS = np.cumsum(IN_SIZES)[:-1].tolist()

kernel_name = "moba_shortconv_gated_hybrid_step"


def rmsnorm(x, g):
    xf = x.astype(jnp.float32)
    y = xf * lax.rsqrt(jnp.mean(xf * xf, axis=-1, keepdims=True) + RMS_EPS)
    return (y * g.astype(jnp.float32)).astype(x.dtype)


def rope(x, pos):
    half = HEAD_DIM // 2
    inv = 1.0 / (ROPE_THETA ** (jnp.arange(half, dtype=jnp.float32) / half))
    ang = pos.astype(jnp.float32)[:, None] * inv[None, :]
    cos = jnp.cos(ang)[None, :, None, :]
    sin = jnp.sin(ang)[None, :, None, :]
    xf = x.astype(jnp.float32)
    x1, x2 = xf[..., :half], xf[..., half:]
    return jnp.concatenate([x1 * cos - x2 * sin, x2 * cos + x1 * sin], axis=-1).astype(x.dtype)


def moba_blocks(k_full, v_full):
    B, L = k_full.shape[:2]
    nbp = L // MOBA_BLOCK
    kb = k_full.reshape(B, nbp, MOBA_BLOCK, N_KV_HEADS, HEAD_DIM).transpose(0, 1, 3, 2, 4)
    vb = v_full.reshape(B, nbp, MOBA_BLOCK, N_KV_HEADS, HEAD_DIM).transpose(0, 1, 3, 2, 4)
    kmean = jnp.mean(kb.astype(jnp.float32), axis=3)
    return kb, vb, kmean


def moba_attend(q, kb, vb, kmean, q_pos):
    B, T = q.shape[:2]
    nbp = kb.shape[1]
    scale = HEAD_DIM ** -0.5
    qf = q.astype(jnp.float32)
    qg = qf.reshape(B, T, N_KV_HEADS, KV_GROUP, HEAD_DIM)
    own = q_pos // MOBA_BLOCK
    gate = jnp.einsum('btkgd,bjkd->btkgj', qg, kmean).reshape(B, T, N_HEADS, nbp)
    past_blk = jnp.arange(nbp)[None, :] < own[:, None]
    gate = jnp.where(past_blk[None, :, None, :], gate, -jnp.inf)
    _, sel = lax.top_k(gate, MOBA_TOPK)
    sel_ok = sel < own[None, :, None, None]
    bi = jnp.arange(B)[:, None, None, None]
    hi = (jnp.arange(N_HEADS) // KV_GROUP)[None, None, :, None]
    k_sel = kb[bi, sel, hi].astype(jnp.float32)
    v_sel = vb[bi, sel, hi].astype(jnp.float32)
    s_sel = jnp.einsum('bthd,bthjnd->bthjn', qf, k_sel) * scale
    s_sel = jnp.where(sel_ok[..., None], s_sel, -jnp.inf).reshape(B, T, N_HEADS, MOBA_TOPK * MOBA_BLOCK)
    k_own = kb[jnp.arange(B)[:, None], own[None, :]].astype(jnp.float32)
    v_own = vb[jnp.arange(B)[:, None], own[None, :]].astype(jnp.float32)
    s_own = jnp.einsum('btkgd,btknd->btkgn', qg, k_own).reshape(B, T, N_HEADS, MOBA_BLOCK) * scale
    key_pos = own[:, None] * MOBA_BLOCK + jnp.arange(MOBA_BLOCK)[None, :]
    causal = key_pos <= q_pos[:, None]
    s_own = jnp.where(causal[None, :, None, :], s_own, -jnp.inf)
    p = jax.nn.softmax(jnp.concatenate([s_sel, s_own], axis=-1), axis=-1)
    p_sel = p[..., :MOBA_TOPK * MOBA_BLOCK].reshape(B, T, N_HEADS, MOBA_TOPK, MOBA_BLOCK)
    p_own = p[..., MOBA_TOPK * MOBA_BLOCK:].reshape(B, T, N_KV_HEADS, KV_GROUP, MOBA_BLOCK)
    o = jnp.einsum('bthjn,bthjnd->bthd', p_sel, v_sel)
    o = o + jnp.einsum('btkgn,btknd->btkgd', p_own, v_own).reshape(B, T, N_HEADS, HEAD_DIM)
    return o.astype(q.dtype)


def pad_blocks(k, v):
    L = k.shape[1]
    nbp = max(-(-L // MOBA_BLOCK), MOBA_TOPK)
    pad = ((0, 0), (0, nbp * MOBA_BLOCK - L), (0, 0), (0, 0))
    return jnp.pad(k, pad), jnp.pad(v, pad)


def attn_prompt(q, k, v):
    B, S = q.shape[:2]
    kb, vb, kmean = moba_blocks(*pad_blocks(k, v))
    nc = S // Q_CHUNK
    qc = q.reshape(B, nc, Q_CHUNK, N_HEADS, HEAD_DIM).transpose(1, 0, 2, 3, 4)
    pc = jnp.arange(S, dtype=jnp.int32).reshape(nc, Q_CHUNK)
    out = lax.map(lambda a: moba_attend(a[0], kb, vb, kmean, a[1]), (qc, pc))
    return out.transpose(1, 0, 2, 3, 4).reshape(B, S, Q_WIDTH)


def attn_sample(q, k, v, cache_k, cache_v, page_table, layer):
    DB, T = q.shape[:2]
    past = page_table.shape[1] * cache_k.shape[2]
    k_past = cache_k[layer, page_table].reshape(DB, past, N_KV_HEADS, HEAD_DIM)
    v_past = cache_v[layer, page_table].reshape(DB, past, N_KV_HEADS, HEAD_DIM)
    k_all = jnp.concatenate([k_past, k.astype(k_past.dtype)], axis=1)
    v_all = jnp.concatenate([v_past, v.astype(v_past.dtype)], axis=1)
    kb, vb, kmean = moba_blocks(*pad_blocks(k_all, v_all))
    q_pos = past + jnp.arange(T, dtype=jnp.int32)
    return moba_attend(q, kb, vb, kmean, q_pos).reshape(DB, T, Q_WIDTH)


def short_conv(b_gate, c_gate, h, conv_w, prev):
    T = h.shape[1]
    u = c_gate * h
    upad = jnp.concatenate([prev.astype(u.dtype), u], axis=1)
    conv = sum(conv_w[j] * upad[:, j:j + T] for j in range(CONV_K))
    return b_gate * conv, upad[:, -(CONV_K - 1):]


def layer(x, pos, attn_core, conv_prev, ln1, w_in, conv_w, w_attn_br, w_conv_br, w_o,
          ln2, w_ff_gate, w_ff_up, w_ff_down):
    B, T, _ = x.shape
    xn = rmsnorm(x, ln1)
    q, k, v, b_gate, c_gate, h, g_attn, g_conv = jnp.split(xn @ w_in, IN_SPLITS, axis=-1)
    q = rope(q.reshape(B, T, N_HEADS, HEAD_DIM), pos)
    k = rope(k.reshape(B, T, N_KV_HEADS, HEAD_DIM), pos)
    v = v.reshape(B, T, N_KV_HEADS, HEAD_DIM)
    a = attn_core(q, k, v)
    c, conv_new = short_conv(b_gate, c_gate, h, conv_w, conv_prev)
    merged = jax.nn.sigmoid(g_attn) * (a @ w_attn_br) + jax.nn.sigmoid(g_conv) * (c @ w_conv_br)
    x = x + merged @ w_o
    hn = rmsnorm(x, ln2)
    x = x + (jax.nn.silu(hn @ w_ff_gate) * (hn @ w_ff_up)) @ w_ff_down
    return x, k, v, conv_new


def setup_inputs(seed: int = 0) -> dict:
    key = jax.random.key(seed)
    ks = jax.random.split(key, 20)
    n_pages = PAST_LEN // PAGE_SIZE
    n_phys = (DEC_BATCH * n_pages * 5) // 4
    f32 = jnp.float32

    def w(k, shape, fan_in):
        return jax.random.normal(k, shape, f32) * (fan_in ** -0.5)

    def gain(k, shape):
        return 1.0 + 0.05 * jax.random.normal(k, shape, f32)

    page_table = jax.random.permutation(ks[5], n_phys)[:DEC_BATCH * n_pages].reshape(DEC_BATCH, n_pages).astype(jnp.int32)
    return {
        "x_prompt": jax.random.normal(ks[0], (BATCH, SEQ, D_MODEL), f32),
        "x_sample": jax.random.normal(ks[1], (DEC_BATCH, DEC_SEQ, D_MODEL), f32),
        "cache_k": jax.random.normal(ks[2], (DEPTH, n_phys, PAGE_SIZE, N_KV_HEADS, HEAD_DIM), f32),
        "cache_v": jax.random.normal(ks[3], (DEPTH, n_phys, PAGE_SIZE, N_KV_HEADS, HEAD_DIM), f32),
        "state_conv": jax.random.normal(ks[4], (DEPTH, DEC_BATCH, CONV_K - 1, CONV_WIDTH), f32),
        "page_table": page_table,
        "ln1": gain(ks[6], (DEPTH, D_MODEL)),
        "w_in": w(ks[7], (DEPTH, D_MODEL, N_IN), D_MODEL),
        "conv_w": w(ks[8], (DEPTH, CONV_K, CONV_WIDTH), CONV_K),
        "w_attn_br": w(ks[9], (DEPTH, Q_WIDTH, D_MODEL), Q_WIDTH),
        "w_conv_br": w(ks[10], (DEPTH, CONV_WIDTH, D_MODEL), CONV_WIDTH),
        "w_o": w(ks[11], (DEPTH, D_MODEL, D_MODEL), D_MODEL),
        "ln2": gain(ks[12], (DEPTH, D_MODEL)),
        "w_ff_gate": w(ks[13], (DEPTH, D_MODEL, D_FF), D_MODEL),
        "w_ff_up": w(ks[14], (DEPTH, D_MODEL, D_FF), D_MODEL),
        "w_ff_down": w(ks[15], (DEPTH, D_FF, D_MODEL), D_FF),
        "ln_f": gain(ks[16], (D_MODEL,)),
    }


def reference(x_prompt, x_sample, cache_k, cache_v, state_conv, page_table, ln1, w_in, conv_w,
              w_attn_br, w_conv_br, w_o, ln2, w_ff_gate, w_ff_up, w_ff_down, ln_f):
    B, S, _ = x_prompt.shape
    DB, T, _ = x_sample.shape
    past = page_table.shape[1] * cache_k.shape[2]
    pos_p = jnp.arange(S, dtype=jnp.int32)
    pos_s = past + jnp.arange(T, dtype=jnp.int32)
    conv_zero = jnp.zeros((B, CONV_K - 1, CONV_WIDTH), x_prompt.dtype)
    xp, xs = x_prompt, x_sample
    kps, vps, cps, kss, vss, css = [], [], [], [], [], []
    for l in range(DEPTH):
        params = (ln1[l], w_in[l], conv_w[l], w_attn_br[l], w_conv_br[l], w_o[l],
                  ln2[l], w_ff_gate[l], w_ff_up[l], w_ff_down[l])
        xp, kp, vp, cp = layer(xp, pos_p, attn_prompt, conv_zero, *params)
        core_s = functools.partial(attn_sample, cache_k=cache_k, cache_v=cache_v,
                                   page_table=page_table, layer=l)
        xs, k_s, v_s, c_s = layer(xs, pos_s, core_s, state_conv[l], *params)
        kps.append(kp); vps.append(vp); cps.append(cp)
        kss.append(k_s); vss.append(v_s); css.append(c_s)
    y_prompt = rmsnorm(xp, ln_f)
    y_sample = rmsnorm(xs, ln_f)
    return (y_prompt, y_sample, jnp.stack(kps), jnp.stack(vps), jnp.stack(cps),
            jnp.stack(kss), jnp.stack(vss), jnp.stack(css))
```

```python
import functools

import jax
import jax.numpy as jnp
from jax import lax
from jax.experimental import pallas as pl
from jax.experimental.pallas import tpu as pltpu

F32 = jnp.float32
BF16 = jnp.bfloat16

N_HEADS = 16
HEAD_DIM = 128
N_KV_HEADS = 4
KV_GROUP = N_HEADS // N_KV_HEADS
MOBA_BLOCK = 256
MOBA_TOPK = 3
ROPE_THETA = 10000.0
CONV_K = 3
RMS_EPS = 1e-6
ATTN_SCALE = HEAD_DIM ** -0.5

LANES = 128
SUBLANES = 8
MIB = 1 << 20
ROW_TILE = 1024
COL_TILE = 512
PAGES_PER_STEP = 16


def _cparams(semantics, vmem_mib):
    return pltpu.CompilerParams(dimension_semantics=semantics, vmem_limit_bytes=vmem_mib * MIB)


def _row_tile(m, cap=ROW_TILE):
    t = min(m, cap)
    assert m % t == 0 and t % SUBLANES == 0, (m, t)
    return t


def _rmsnorm_f32(x, g):
    return x * lax.rsqrt(jnp.mean(x * x, axis=-1, keepdims=True) + RMS_EPS) * g


def _inproj_kernel(x_ref, g_ref, w_ref, cos_ref, sin_ref, o_ref, xn_ref, *, rope_tiles):
    j = pl.program_id(1)

    @pl.when(j == 0)
    def _():
        xn_ref[...] = _rmsnorm_f32(x_ref[...], g_ref[...]).astype(BF16)

    acc = jnp.dot(xn_ref[...], w_ref[...].astype(BF16), preferred_element_type=F32)

    @pl.when(j < rope_tiles)
    def _():
        cos = cos_ref[...]
        sin = sin_ref[...]
        for c in range(acc.shape[1] // HEAD_DIM):
            a = acc[:, c * HEAD_DIM:(c + 1) * HEAD_DIM]
            o_ref[:, c * HEAD_DIM:(c + 1) * HEAD_DIM] = a * cos + pltpu.roll(a, HEAD_DIM // 2, axis=1) * sin

    @pl.when(j >= rope_tiles)
    def _():
        o_ref[...] = acc


def _inproj(x, ln1, w_in, cos, sin, rope_cols):
    m, d = x.shape
    n = w_in.shape[1]
    tm, tn = _row_tile(m), COL_TILE
    assert n % tn == 0 and rope_cols % tn == 0
    return pl.pallas_call(
        functools.partial(_inproj_kernel, rope_tiles=rope_cols // tn),
        grid=(m // tm, n // tn),
        in_specs=[
            pl.BlockSpec((tm, d), lambda i, j: (i, 0)),
            pl.BlockSpec((1, d), lambda i, j: (0, 0)),
            pl.BlockSpec((d, tn), lambda i, j: (0, j)),
            pl.BlockSpec((tm, HEAD_DIM), lambda i, j: (i, 0)),
            pl.BlockSpec((tm, HEAD_DIM), lambda i, j: (i, 0)),
        ],
        out_specs=pl.BlockSpec((tm, tn), lambda i, j: (i, j)),
        out_shape=jax.ShapeDtypeStruct((m, n), F32),
        scratch_shapes=[pltpu.VMEM((tm, d), BF16)],
        compiler_params=_cparams(("parallel", "arbitrary"), 48),
        name="inproj",
    )(x, ln1.reshape(1, d), w_in, cos, sin)


def _attn_prompt_kernel(q_ref, k_ref, v_ref, o_ref, kb_ref, vb_ref, kmean_ref, m_ref, l_ref, acc_ref):
    qi = pl.program_id(2)
    seq = k_ref.shape[0]
    nb = seq // MOBA_BLOCK
    rows = KV_GROUP * MOBA_BLOCK

    @pl.when(qi == 0)
    def _():
        k = k_ref[...]
        kb_ref[...] = k.astype(BF16)
        vb_ref[...] = v_ref[...].astype(BF16)
        kmean_ref[...] = jnp.mean(k.reshape(nb, MOBA_BLOCK, HEAD_DIM), axis=1)

    q = q_ref[...]
    q4 = jnp.concatenate([q[:, h * HEAD_DIM:(h + 1) * HEAD_DIM] for h in range(KV_GROUP)], axis=0)
    q4b = (q4 * ATTN_SCALE).astype(BF16)

    gate = lax.dot_general(q4, kmean_ref[...], (((1,), (1,)), ((), ())),
                           precision=lax.Precision.HIGHEST, preferred_element_type=F32)
    blk_i = lax.broadcasted_iota(jnp.int32, (rows, nb), 1)
    blk = blk_i.astype(F32)
    past = blk_i < qi
    gate = jnp.where(past, gate, -jnp.inf)
    sel = jnp.zeros((rows, nb), F32)
    for _ in range(MOBA_TOPK):
        top = jnp.max(gate, axis=1, keepdims=True)
        first = jnp.min(jnp.where(gate == top, blk, float(nb)), axis=1, keepdims=True)
        pick = blk == first
        sel = jnp.where(pick & past, 1.0, sel)
        gate = jnp.where(pick, -jnp.inf, gate)

    own = pl.multiple_of(qi * MOBA_BLOCK, MOBA_BLOCK)
    s = lax.dot_general(q4b, kb_ref[pl.ds(own, MOBA_BLOCK), :], (((1,), (1,)), ((), ())),
                        preferred_element_type=F32)
    qrow = lax.broadcasted_iota(jnp.int32, (rows, MOBA_BLOCK), 0) & (MOBA_BLOCK - 1)
    kcol = lax.broadcasted_iota(jnp.int32, (rows, MOBA_BLOCK), 1)
    s = jnp.where(kcol <= qrow, s, -jnp.inf)
    m0 = jnp.max(s, axis=1, keepdims=True)
    p = jnp.exp(s - m0)
    m_ref[...] = m0
    l_ref[...] = jnp.sum(p, axis=1, keepdims=True)
    acc_ref[...] = jnp.dot(p.astype(BF16), vb_ref[pl.ds(own, MOBA_BLOCK), :], preferred_element_type=F32)

    def past_block(i, carry):
        start = pl.multiple_of(i * MOBA_BLOCK, MOBA_BLOCK)
        chosen = jnp.sum(jnp.where(blk_i == i, sel, 0.0), axis=1, keepdims=True) > 0.0
        s = lax.dot_general(q4b, kb_ref[pl.ds(start, MOBA_BLOCK), :], (((1,), (1,)), ((), ())),
                            preferred_element_type=F32)
        s = jnp.where(chosen, s, -jnp.inf)
        m_old = m_ref[...]
        m_new = jnp.maximum(m_old, jnp.max(s, axis=1, keepdims=True))
        alpha = jnp.exp(m_old - m_new)
        p = jnp.exp(s - m_new)
        l_ref[...] = alpha * l_ref[...] + jnp.sum(p, axis=1, keepdims=True)
        acc_ref[...] = alpha * acc_ref[...] + jnp.dot(
            p.astype(BF16), vb_ref[pl.ds(start, MOBA_BLOCK), :], preferred_element_type=F32)
        m_ref[...] = m_new
        return carry

    lax.fori_loop(0, qi, past_block, 0)

    o = acc_ref[...] / l_ref[...]
    for h in range(KV_GROUP):
        o_ref[:, h * HEAD_DIM:(h + 1) * HEAD_DIM] = o[h * MOBA_BLOCK:(h + 1) * MOBA_BLOCK].astype(o_ref.dtype)


def _attn_prompt(proj, batch, seq, k_col, v_col):
    assert seq % MOBA_BLOCK == 0 and seq // MOBA_BLOCK >= MOBA_TOPK
    nq = seq // MOBA_BLOCK
    gw = KV_GROUP * HEAD_DIM
    rows = KV_GROUP * MOBA_BLOCK
    kblk, vblk = k_col // HEAD_DIM, v_col // HEAD_DIM
    return pl.pallas_call(
        _attn_prompt_kernel,
        grid=(batch, N_KV_HEADS, nq),
        in_specs=[
            pl.BlockSpec((MOBA_BLOCK, gw), lambda b, g, i: (b * nq + i, g)),
            pl.BlockSpec((seq, HEAD_DIM), lambda b, g, i: (b, kblk + g)),
            pl.BlockSpec((seq, HEAD_DIM), lambda b, g, i: (b, vblk + g)),
        ],
        out_specs=pl.BlockSpec((MOBA_BLOCK, gw), lambda b, g, i: (b * nq + i, g)),
        out_shape=jax.ShapeDtypeStruct((batch * seq, N_HEADS * HEAD_DIM), BF16),
        scratch_shapes=[
            pltpu.VMEM((seq, HEAD_DIM), BF16),
            pltpu.VMEM((seq, HEAD_DIM), BF16),
            pltpu.VMEM((seq // MOBA_BLOCK, HEAD_DIM), F32),
            pltpu.VMEM((rows, 1), F32),
            pltpu.VMEM((rows, 1), F32),
            pltpu.VMEM((rows, HEAD_DIM), F32),
        ],
        compiler_params=_cparams(("parallel", "parallel", "arbitrary"), 32),
        name="attn_prompt",
    )(proj, proj, proj)


def _conv_seq_kernel(b_ref, c_ref, h_ref, cp_ref, hp_ref, st_ref, w_ref, cb_ref, ut_ref, *, seq):
    i = pl.program_id(0)
    tm = c_ref.shape[0]
    u = c_ref[...] * h_ref[...]
    prev = jnp.where((i * tm) % seq == 0, st_ref[0], cp_ref[...] * hp_ref[...])
    p1 = prev[SUBLANES - 1:SUBLANES, :]
    p2 = prev[SUBLANES - 2:SUBLANES - 1, :]
    row = lax.broadcasted_iota(jnp.int32, u.shape, 0)
    u1 = jnp.where(row == 0, p1, pltpu.roll(u, 1, axis=0))
    u2 = jnp.where(row == 0, p2, jnp.where(row == 1, p1, pltpu.roll(u, 2, axis=0)))
    w = w_ref[...]
    conv = w[0:1, :] * u2 + w[1:2, :] * u1 + w[2:3, :] * u
    cb_ref[...] = (b_ref[...] * conv).astype(BF16)
    ut_ref[0] = u[tm - SUBLANES:tm, :]


def _conv_seq(proj, state8, conv_w, seq, b_col, cw):
    m = proj.shape[0]
    tm = _row_tile(seq, 512)
    nt = m // tm
    cb = b_col // cw
    pstep = tm // SUBLANES
    return pl.pallas_call(
        functools.partial(_conv_seq_kernel, seq=seq),
        grid=(nt,),
        in_specs=[
            pl.BlockSpec((tm, cw), lambda i: (i, cb)),
            pl.BlockSpec((tm, cw), lambda i: (i, cb + 1)),
            pl.BlockSpec((tm, cw), lambda i: (i, cb + 2)),
            pl.BlockSpec((SUBLANES, cw), lambda i: (jnp.maximum(i * pstep - 1, 0), cb + 1)),
            pl.BlockSpec((SUBLANES, cw), lambda i: (jnp.maximum(i * pstep - 1, 0), cb + 2)),
            pl.BlockSpec((1, SUBLANES, cw), lambda i: ((i * tm) // seq, 0, 0)),
            pl.BlockSpec((CONV_K, cw), lambda i: (0, 0)),
        ],
        out_specs=[
            pl.BlockSpec((tm, cw), lambda i: (i, 0)),
            pl.BlockSpec((1, SUBLANES, cw), lambda i: (i, 0, 0)),
        ],
        out_shape=[
            jax.ShapeDtypeStruct((m, cw), BF16),
            jax.ShapeDtypeStruct((nt, SUBLANES, cw), F32),
        ],
        compiler_params=_cparams(("parallel",), 32),
        name="conv_seq",
    )(proj, proj, proj, proj, proj, state8, conv_w)


def _conv_step_kernel(b_ref, c_ref, h_ref, st_ref, w_ref, cb_ref, u_ref):
    cw = c_ref.shape[1]
    u = c_ref[...] * h_ref[...]
    w = w_ref[...]
    conv = w[0:1, :] * st_ref[:, 0:cw] + w[1:2, :] * st_ref[:, cw:2 * cw] + w[2:3, :] * u
    cb_ref[...] = (b_ref[...] * conv).astype(BF16)
    u_ref[...] = u


def _conv_step(proj, state, conv_w, b_col, cw):
    m = proj.shape[0]
    cb = b_col // cw
    return pl.pallas_call(
        _conv_step_kernel,
        grid=(1,),
        in_specs=[
            pl.BlockSpec((m, cw), lambda i: (0, cb)),
            pl.BlockSpec((m, cw), lambda i: (0, cb + 1)),
            pl.BlockSpec((m, cw), lambda i: (0, cb + 2)),
            pl.BlockSpec((m, (CONV_K - 1) * cw), lambda i: (0, 0)),
            pl.BlockSpec((CONV_K, cw), lambda i: (0, 0)),
        ],
        out_specs=[pl.BlockSpec((m, cw), lambda i: (0, 0)), pl.BlockSpec((m, cw), lambda i: (0, 0))],
        out_shape=[jax.ShapeDtypeStruct((m, cw), BF16), jax.ShapeDtypeStruct((m, cw), F32)],
        compiler_params=_cparams(("arbitrary",), 32),
        name="conv_step",
    )(proj, proj, proj, state, conv_w)


def _merge_kernel(a_ref, cb_ref, ga_ref, gc_ref, wa_ref, wc_ref, o_ref):
    ya = jnp.dot(a_ref[...], wa_ref[...].astype(BF16), preferred_element_type=F32)
    yc = jnp.dot(cb_ref[...], wc_ref[...].astype(BF16), preferred_element_type=F32)
    o_ref[...] = (jax.nn.sigmoid(ga_ref[...]) * ya + jax.nn.sigmoid(gc_ref[...]) * yc).astype(o_ref.dtype)


def _merge(a, cb, proj, w_attn_br, w_conv_br, ga_col, gc_col):
    m, qw = a.shape
    cw = cb.shape[1]
    d = w_attn_br.shape[1]
    tm, tn = _row_tile(m), COL_TILE
    ga, gc = ga_col // tn, gc_col // tn
    return pl.pallas_call(
        _merge_kernel,
        grid=(m // tm, d // tn),
        in_specs=[
            pl.BlockSpec((tm, qw), lambda i, j: (i, 0)),
            pl.BlockSpec((tm, cw), lambda i, j: (i, 0)),
            pl.BlockSpec((tm, tn), lambda i, j: (i, ga + j)),
            pl.BlockSpec((tm, tn), lambda i, j: (i, gc + j)),
            pl.BlockSpec((qw, tn), lambda i, j: (0, j)),
            pl.BlockSpec((cw, tn), lambda i, j: (0, j)),
        ],
        out_specs=pl.BlockSpec((tm, tn), lambda i, j: (i, j)),
        out_shape=jax.ShapeDtypeStruct((m, d), BF16),
        compiler_params=_cparams(("parallel", "parallel"), 48),
        name="merge",
    )(a, cb, proj, proj, w_attn_br, w_conv_br)


def _outproj_kernel(m_ref, w_ref, x_ref, o_ref):
    o_ref[...] = x_ref[...] + jnp.dot(m_ref[...], w_ref[...].astype(BF16), preferred_element_type=F32)


def _outproj(merged, w_o, x):
    m, d = merged.shape
    n = w_o.shape[1]
    tm, tn = _row_tile(m), COL_TILE
    return pl.pallas_call(
        _outproj_kernel,
        grid=(m // tm, n // tn),
        in_specs=[
            pl.BlockSpec((tm, d), lambda i, j: (i, 0)),
            pl.BlockSpec((d, tn), lambda i, j: (0, j)),
            pl.BlockSpec((tm, tn), lambda i, j: (i, j)),
        ],
        out_specs=pl.BlockSpec((tm, tn), lambda i, j: (i, j)),
        out_shape=jax.ShapeDtypeStruct((m, n), F32),
        compiler_params=_cparams(("parallel", "parallel"), 48),
        name="outproj",
    )(merged, w_o, x)


def _ffn_up_kernel(x_ref, g_ref, wg_ref, wu_ref, o_ref, hn_ref):
    @pl.when(pl.program_id(1) == 0)
    def _():
        hn_ref[...] = _rmsnorm_f32(x_ref[...], g_ref[...]).astype(BF16)

    hn = hn_ref[...]
    gate = jnp.dot(hn, wg_ref[...].astype(BF16), preferred_element_type=F32)
    up = jnp.dot(hn, wu_ref[...].astype(BF16), preferred_element_type=F32)
    o_ref[...] = (gate * jax.nn.sigmoid(gate) * up).astype(o_ref.dtype)


def _ffn_up(x, ln2, w_gate, w_up):
    m, d = x.shape
    f = w_gate.shape[1]
    tm, tn = _row_tile(m), COL_TILE
    assert f % tn == 0
    return pl.pallas_call(
        _ffn_up_kernel,
        grid=(m // tm, f // tn),
        in_specs=[
            pl.BlockSpec((tm, d), lambda i, j: (i, 0)),
            pl.BlockSpec((1, d), lambda i, j: (0, 0)),
            pl.BlockSpec((d, tn), lambda i, j: (0, j)),
            pl.BlockSpec((d, tn), lambda i, j: (0, j)),
        ],
        out_specs=pl.BlockSpec((tm, tn), lambda i, j: (i, j)),
        out_shape=jax.ShapeDtypeStruct((m, f), BF16),
        scratch_shapes=[pltpu.VMEM((tm, d), BF16)],
        compiler_params=_cparams(("parallel", "arbitrary"), 48),
        name="ffn_up",
    )(x, ln2.reshape(1, d), w_gate, w_up)


def _ffn_down_kernel(h_ref, w_ref, x_ref, g_ref, o_ref, acc_ref):
    k = pl.program_id(1)

    @pl.when(k == 0)
    def _():
        acc_ref[...] = x_ref[...]

    acc_ref[...] += jnp.dot(h_ref[...], w_ref[...].astype(BF16), preferred_element_type=F32)

    @pl.when(k == pl.num_programs(1) - 1)
    def _():
        o_ref[...] = _rmsnorm_f32(acc_ref[...], g_ref[...])


def _ffn_down(hmid, w_down, x, ln_f):
    m, f = hmid.shape
    d = w_down.shape[1]
    tm, tk = _row_tile(m, 512), COL_TILE
    return pl.pallas_call(
        _ffn_down_kernel,
        grid=(m // tm, f // tk),
        in_specs=[
            pl.BlockSpec((tm, tk), lambda i, k: (i, k)),
            pl.BlockSpec((tk, d), lambda i, k: (k, 0)),
            pl.BlockSpec((tm, d), lambda i, k: (i, 0)),
            pl.BlockSpec((1, d), lambda i, k: (0, 0)),
        ],
        out_specs=pl.BlockSpec((tm, d), lambda i, k: (i, 0)),
        out_shape=jax.ShapeDtypeStruct((m, d), F32),
        scratch_shapes=[pltpu.VMEM((tm, d), F32)],
        compiler_params=_cparams(("parallel", "arbitrary"), 48),
        name="ffn_down",
    )(hmid, w_down, x, ln_f.reshape(1, d))


def _sample_select_kernel(pt_ref, q_ref, *refs):
    page_refs = refs[:PAGES_PER_STEP]
    sel_ref = refs[PAGES_PER_STEP]
    kmean_ref = refs[PAGES_PER_STEP + 1]
    p = pl.program_id(1)
    nb, kvw = kmean_ref.shape
    pages_per_block = MOBA_BLOCK // page_refs[0].shape[1]
    blocks_per_step = PAGES_PER_STEP // pages_per_block

    row = lax.broadcasted_iota(jnp.int32, (blocks_per_step, kvw), 0)
    means = jnp.zeros((blocks_per_step, kvw), F32)
    for r in range(blocks_per_step):
        tot = jnp.zeros((1, kvw), F32)
        for t in range(pages_per_block):
            tot = tot + jnp.sum(page_refs[r * pages_per_block + t][0], axis=0, keepdims=True)
        means = jnp.where(row == r, tot * (1.0 / MOBA_BLOCK), means)
    kmean_ref[pl.ds(pl.multiple_of(p * blocks_per_step, blocks_per_step), blocks_per_step), :] = means

    @pl.when(p == pl.num_programs(1) - 1)
    def _():
        q = q_ref[0]
        head_group = lax.broadcasted_iota(jnp.int32, (N_HEADS, nb), 0) // KV_GROUP
        gate = jnp.zeros((N_HEADS, nb), F32)
        for g in range(N_KV_HEADS):
            gg = lax.dot_general(q, kmean_ref[:, g * HEAD_DIM:(g + 1) * HEAD_DIM], (((1,), (1,)), ((), ())),
                                 precision=lax.Precision.HIGHEST, preferred_element_type=F32)
            gate = jnp.where(head_group == g, gg, gate)
        blk = lax.broadcasted_iota(jnp.int32, (N_HEADS, nb), 1).astype(F32)
        lane = lax.broadcasted_iota(jnp.int32, (N_HEADS, LANES), 1)
        out = jnp.zeros((N_HEADS, LANES), F32)
        for r in range(MOBA_TOPK):
            top = jnp.max(gate, axis=1, keepdims=True)
            first = jnp.min(jnp.where(gate == top, blk, float(nb)), axis=1, keepdims=True)
            out = jnp.where(lane == r, first, out)
            gate = jnp.where(blk == first, -jnp.inf, gate)
        sel_ref[0] = out.astype(jnp.int32)


def _sample_select(q_s, cache_k2, page_table):
    db, n_pages = page_table.shape
    _, page, kvw = cache_k2.shape
    assert MOBA_BLOCK % page == 0 and (n_pages * page) % MOBA_BLOCK == 0
    assert n_pages % PAGES_PER_STEP == 0 and PAGES_PER_STEP % (MOBA_BLOCK // page) == 0
    nb = n_pages * page // MOBA_BLOCK
    assert nb >= MOBA_TOPK
    steps = n_pages // PAGES_PER_STEP
    page_specs = [
        pl.BlockSpec((1, page, kvw), functools.partial(
            lambda b, p, pt, t: (pt[b, p * PAGES_PER_STEP + t], 0, 0), t=t))
        for t in range(PAGES_PER_STEP)
    ]
    grid_spec = pltpu.PrefetchScalarGridSpec(
        num_scalar_prefetch=1,
        grid=(db, steps),
        in_specs=[pl.BlockSpec((1, N_HEADS, HEAD_DIM), lambda b, p, pt: (b, 0, 0))] + page_specs,
        out_specs=pl.BlockSpec((1, N_HEADS, LANES), lambda b, p, pt: (b, 0, 0)),
        scratch_shapes=[pltpu.VMEM((nb, kvw), F32)],
    )
    sel = pl.pallas_call(
        _sample_select_kernel,
        grid_spec=grid_spec,
        out_shape=jax.ShapeDtypeStruct((db, N_HEADS, LANES), jnp.int32),
        compiler_params=_cparams(("parallel", "arbitrary"), 32),
        name="sample_select",
    )(page_table, q_s, *([cache_k2] * PAGES_PER_STEP))
    return sel[:, :, :MOBA_TOPK]


def _sample_attn_kernel(pt_ref, sel_ref, q_ref, ks_ref, vs_ref, *refs, n_pages_sel):
    k_refs = refs[:n_pages_sel]
    v_refs = refs[n_pages_sel:2 * n_pages_sel]
    o_ref = refs[2 * n_pages_sel]
    q = q_ref[0] * ATTN_SCALE
    scores = [jnp.sum(k_refs[t][0] * q, axis=1, keepdims=True) for t in range(n_pages_sel)]
    s_new = jnp.sum(ks_ref[0] * q, axis=1, keepdims=True)
    m = s_new
    for s in scores:
        m = jnp.maximum(m, jnp.max(s, axis=0, keepdims=True))
    p_new = jnp.exp(s_new - m)
    l = p_new
    o = p_new * vs_ref[0]
    for t in range(n_pages_sel):
        p = jnp.exp(scores[t] - m)
        l = l + jnp.sum(p, axis=0, keepdims=True)
        o = o + jnp.sum(p * v_refs[t][0], axis=0, keepdims=True)
    o_ref[0] = (o / l).astype(o_ref.dtype)


def _sample_attn(q_s, k_s, v_s, cache_k2, cache_v2, page_table, sel):
    db, n_pages = page_table.shape
    page = cache_k2.shape[1]
    ppb = MOBA_BLOCK // page
    n_sel = MOBA_TOPK * ppb

    def page_map(b, h, pt, sl, t):
        return (pt[b, sl[b, h * MOBA_TOPK + t // ppb] * ppb + t % ppb], 0, h // KV_GROUP)

    page_specs = [pl.BlockSpec((1, page, HEAD_DIM), functools.partial(page_map, t=t)) for t in range(n_sel)]
    grid_spec = pltpu.PrefetchScalarGridSpec(
        num_scalar_prefetch=2,
        grid=(db, N_HEADS),
        in_specs=[
            pl.BlockSpec((1, 1, HEAD_DIM), lambda b, h, pt, sl: (b * N_HEADS + h, 0, 0)),
            pl.BlockSpec((1, 1, HEAD_DIM), lambda b, h, pt, sl: (b * N_KV_HEADS + h // KV_GROUP, 0, 0)),
            pl.BlockSpec((1, 1, HEAD_DIM), lambda b, h, pt, sl: (b * N_KV_HEADS + h // KV_GROUP, 0, 0)),
        ] + page_specs + page_specs,
        out_specs=pl.BlockSpec((1, 1, HEAD_DIM), lambda b, h, pt, sl: (b * N_HEADS + h, 0, 0)),
    )
    return pl.pallas_call(
        functools.partial(_sample_attn_kernel, n_pages_sel=n_sel),
        grid_spec=grid_spec,
        out_shape=jax.ShapeDtypeStruct((db * N_HEADS, 1, HEAD_DIM), F32),
        compiler_params=_cparams(("parallel", "parallel"), 32),
        name="sample_attn",
    )(page_table, sel, q_s, k_s, v_s, *([cache_k2] * n_sel), *([cache_v2] * n_sel))


def _rope_tables(pos):
    half = HEAD_DIM // 2
    inv = 1.0 / (ROPE_THETA ** (jnp.arange(half, dtype=F32) / half))
    ang = pos.astype(F32)[:, None] * inv[None, :]
    cos, sin = jnp.cos(ang), jnp.sin(ang)
    return jnp.concatenate([cos, cos], axis=-1), jnp.concatenate([-sin, sin], axis=-1)


def _trunk_tail(x, a, cb, proj, w_attn_br, w_conv_br, w_o, ln2, w_ff_gate, w_ff_up, w_ff_down, ln_f,
                ga_col, gc_col):
    merged = _merge(a, cb, proj, w_attn_br, w_conv_br, ga_col, gc_col)
    x1 = _outproj(merged, w_o, x)
    hmid = _ffn_up(x1, ln2, w_ff_gate, w_ff_up)
    return _ffn_down(hmid, w_ff_down, x1, ln_f)


def kernel(x_prompt, x_sample, cache_k, cache_v, state_conv, page_table, ln1, w_in, conv_w, w_attn_br,
           w_conv_br, w_o, ln2, w_ff_gate, w_ff_up, w_ff_down, ln_f):
    batch, seq, d = x_prompt.shape
    db, dec_seq, _ = x_sample.shape
    depth, n_phys, page, kvh, hd = cache_k.shape
    assert depth == 1 and dec_seq == 1 and kvh == N_KV_HEADS and hd == HEAD_DIM
    qw, kvw = N_HEADS * HEAD_DIM, N_KV_HEADS * HEAD_DIM
    cw = conv_w.shape[-1]
    k_col, v_col, b_col = qw, qw + kvw, qw + 2 * kvw
    ga_col = b_col + 3 * cw
    gc_col = ga_col + d
    rope_cols = qw + kvw
    past = page_table.shape[1] * page
    tail = (ln2[0], w_ff_gate[0], w_ff_up[0], w_ff_down[0], ln_f)

    mp = batch * seq
    xp = x_prompt.reshape(mp, d)
    cos_p, sin_p = _rope_tables(jnp.tile(jnp.arange(seq, dtype=jnp.int32), batch))
    proj_p = _inproj(xp, ln1[0], w_in[0], cos_p, sin_p, rope_cols)
    a_p = _attn_prompt(proj_p, batch, seq, k_col, v_col)
    zero_state = jnp.zeros((batch, SUBLANES, cw), F32)
    cb_p, utail_p = _conv_seq(proj_p, zero_state, conv_w[0], seq, b_col, cw)
    y_p = _trunk_tail(xp, a_p, cb_p, proj_p, w_attn_br[0], w_conv_br[0], w_o[0], *tail, ga_col, gc_col)
    tiles_per_seq = utail_p.shape[0] // batch
    conv_p = utail_p.reshape(batch, tiles_per_seq, SUBLANES, cw)[:, -1, SUBLANES - (CONV_K - 1):, :]

    xs = x_sample.reshape(db, d)
    cos_s, sin_s = _rope_tables(jnp.full((db,), past, jnp.int32))
    proj_s = _inproj(xs, ln1[0], w_in[0], cos_s, sin_s, rope_cols)
    q_s = proj_s[:, :qw]
    k_s = proj_s[:, k_col:k_col + kvw]
    v_s = proj_s[:, v_col:v_col + kvw]
    cache_k2 = cache_k.reshape(n_phys, page, kvw)
    cache_v2 = cache_v.reshape(n_phys, page, kvw)
    sel = _sample_select(q_s.reshape(db, N_HEADS, HEAD_DIM), cache_k2, page_table)
    a_s = _sample_attn(q_s.reshape(db * N_HEADS, 1, HEAD_DIM), k_s.reshape(db * N_KV_HEADS, 1, HEAD_DIM),
                       v_s.reshape(db * N_KV_HEADS, 1, HEAD_DIM), cache_k2, cache_v2, page_table,
                       sel.reshape(db, N_HEADS * MOBA_TOPK)).reshape(db, qw).astype(BF16)
    state = state_conv[0].astype(F32)
    cb_s, u_s = _conv_step(proj_s, state.reshape(db, (CONV_K - 1) * cw), conv_w[0], b_col, cw)
    y_s = _trunk_tail(xs, a_s, cb_s, proj_s, w_attn_br[0], w_conv_br[0], w_o[0], *tail, ga_col, gc_col)
    conv_s = jnp.concatenate([state[:, 1:, :], u_s[:, None, :]], axis=1)

    return (
        y_p.reshape(batch, seq, d),
        y_s.reshape(db, 1, d),
        proj_p[:, k_col:k_col + kvw].reshape(1, batch, seq, N_KV_HEADS, HEAD_DIM),
        proj_p[:, v_col:v_col + kvw].reshape(1, batch, seq, N_KV_HEADS, HEAD_DIM),
        conv_p[None],
        k_s.reshape(1, db, 1, N_KV_HEADS, HEAD_DIM),
        v_s.reshape(1, db, 1, N_KV_HEADS, HEAD_DIM),
        conv_s[None],
    )
```

```python
import functools

import jax
import jax.numpy as jnp
from jax import lax
from jax.experimental import pallas as pl
from jax.experimental.pallas import tpu as pltpu

F32 = jnp.float32
BF16 = jnp.bfloat16

N_HEADS = 16
HEAD_DIM = 128
N_KV_HEADS = 4
KV_GROUP = N_HEADS // N_KV_HEADS
MOBA_BLOCK = 256
MOBA_TOPK = 3
ROPE_THETA = 10000.0
CONV_K = 3
RMS_EPS = 1e-6
ATTN_SCALE = HEAD_DIM ** -0.5
LOG2E = 1.4426950408889634
MASK_BIAS = -1e30

LANES = 128
SUBLANES = 8
MIB = 1 << 20
ROW_TILE = 1024
COL_TILE = 512
PAGES_PER_STEP = 32


def _cparams(semantics, vmem_mib):
    return pltpu.CompilerParams(dimension_semantics=semantics, vmem_limit_bytes=vmem_mib * MIB)


def _row_tile(m, cap=ROW_TILE):
    t = min(m, cap)
    assert m % t == 0 and t % SUBLANES == 0, (m, t)
    return t


def _rmsnorm_f32(x, g):
    return x * lax.rsqrt(jnp.mean(x * x, axis=-1, keepdims=True) + RMS_EPS) * g


def _inproj_kernel(x_ref, g_ref, w_ref, cos_ref, sin_ref, o_ref, xn_ref, *, rope_tiles):
    j = pl.program_id(1)

    @pl.when(j == 0)
    def _():
        xn_ref[...] = _rmsnorm_f32(x_ref[...], g_ref[...]).astype(BF16)

    acc = jnp.dot(xn_ref[...], w_ref[...].astype(BF16), preferred_element_type=F32)

    @pl.when(j < rope_tiles)
    def _():
        cos = cos_ref[...]
        sin = sin_ref[...]
        for c in range(acc.shape[1] // HEAD_DIM):
            a = acc[:, c * HEAD_DIM:(c + 1) * HEAD_DIM]
            o_ref[:, c * HEAD_DIM:(c + 1) * HEAD_DIM] = a * cos + pltpu.roll(a, HEAD_DIM // 2, axis=1) * sin

    @pl.when(j >= rope_tiles)
    def _():
        o_ref[...] = acc


def _inproj(x, ln1, w_in, cos, sin, rope_cols):
    m, d = x.shape
    n = w_in.shape[1]
    tm, tn = _row_tile(m), COL_TILE
    assert n % tn == 0 and rope_cols % tn == 0
    return pl.pallas_call(
        functools.partial(_inproj_kernel, rope_tiles=rope_cols // tn),
        grid=(m // tm, n // tn),
        in_specs=[
            pl.BlockSpec((tm, d), lambda i, j: (i, 0)),
            pl.BlockSpec((1, d), lambda i, j: (0, 0)),
            pl.BlockSpec((d, tn), lambda i, j: (0, j)),
            pl.BlockSpec((tm, HEAD_DIM), lambda i, j: (i, 0)),
            pl.BlockSpec((tm, HEAD_DIM), lambda i, j: (i, 0)),
        ],
        out_specs=pl.BlockSpec((tm, tn), lambda i, j: (i, j)),
        out_shape=jax.ShapeDtypeStruct((m, n), F32),
        scratch_shapes=[pltpu.VMEM((tm, d), BF16)],
        compiler_params=_cparams(("parallel", "arbitrary"), 48),
        name="inproj",
    )(x, ln1.reshape(1, d), w_in, cos, sin)


def _attn_prompt_kernel(q_ref, k_ref, v_ref, o_ref, ka_ref, va_ref, kmean_ref, m_ref, acc_ref):
    qi = pl.program_id(2)
    seq = k_ref.shape[0]
    nb = seq // MOBA_BLOCK
    rows = KV_GROUP * MOBA_BLOCK
    pair = 2 * MOBA_BLOCK

    @pl.when(qi == 0)
    def _():
        k = k_ref[...]
        row_blk = lax.broadcasted_iota(jnp.int32, (seq, LANES), 0) // MOBA_BLOCK
        lane = lax.broadcasted_iota(jnp.int32, (seq, LANES), 1)
        ka_ref[:, 0:HEAD_DIM] = k.astype(BF16)
        ka_ref[:, HEAD_DIM:] = jnp.where(lane == row_blk, 1.0, 0.0).astype(BF16)
        va_ref[:, 0:HEAD_DIM] = v_ref[...].astype(BF16)
        va_ref[:, HEAD_DIM:] = jnp.ones((seq, LANES), BF16)
        kmean_ref[...] = jnp.zeros_like(kmean_ref)
        kmean_ref[0:nb, :] = jnp.mean(k.reshape(nb, MOBA_BLOCK, HEAD_DIM), axis=1)

    q = q_ref[...]
    q4 = jnp.concatenate([q[:, h * HEAD_DIM:(h + 1) * HEAD_DIM] for h in range(KV_GROUP)], axis=0)
    qs = (q4 * (ATTN_SCALE * LOG2E)).astype(BF16)

    gate = lax.dot_general(q4, kmean_ref[...], (((1,), (1,)), ((), ())),
                           precision=lax.Precision.HIGHEST, preferred_element_type=F32)
    lane_i = lax.broadcasted_iota(jnp.int32, (rows, LANES), 1)
    lane = lane_i.astype(F32)
    past = lane_i < qi
    gate = jnp.where(past, gate, -jnp.inf)
    bias = jnp.full((rows, LANES), MASK_BIAS, F32)
    for _ in range(MOBA_TOPK):
        top = jnp.max(gate, axis=1, keepdims=True)
        pick = lane == jnp.min(jnp.where(gate == top, lane, float(LANES)), axis=1, keepdims=True)
        bias = jnp.where(pick & past, 0.0, bias)
        gate = jnp.where(pick, -jnp.inf, gate)
    qa = jnp.concatenate([qs, bias.astype(BF16)], axis=1)

    own = pl.multiple_of(qi * MOBA_BLOCK, MOBA_BLOCK)
    s = lax.dot_general(qs, ka_ref[pl.ds(own, MOBA_BLOCK), 0:HEAD_DIM], (((1,), (1,)), ((), ())),
                        preferred_element_type=F32)
    qrow = lax.broadcasted_iota(jnp.int32, (rows, MOBA_BLOCK), 0) & (MOBA_BLOCK - 1)
    kcol = lax.broadcasted_iota(jnp.int32, (rows, MOBA_BLOCK), 1)
    s = jnp.where(kcol <= qrow, s, MASK_BIAS)
    m0 = jnp.broadcast_to(jnp.max(s, axis=1, keepdims=True), (rows, LANES))
    p = jnp.concatenate([jnp.exp2(s[:, c * LANES:(c + 1) * LANES] - m0)
                         for c in range(MOBA_BLOCK // LANES)], axis=1)
    m_ref[...] = m0
    acc_ref[...] = jnp.dot(p.astype(BF16), va_ref[pl.ds(own, MOBA_BLOCK), :], preferred_element_type=F32)

    def past_pair(t, carry):
        start = pl.multiple_of(t * pair, pair)
        s = lax.dot_general(qa, ka_ref[pl.ds(start, pair), :], (((1,), (1,)), ((), ())),
                            preferred_element_type=F32)
        m_old = m_ref[...]
        m_new = jnp.maximum(m_old, jnp.max(s, axis=1, keepdims=True))
        alpha = jnp.exp2(m_old - m_new)
        p = jnp.concatenate([jnp.exp2(s[:, c * LANES:(c + 1) * LANES] - m_new)
                             for c in range(pair // LANES)], axis=1)
        pv = jnp.dot(p.astype(BF16), va_ref[pl.ds(start, pair), :], preferred_element_type=F32)
        for c in range(2):
            acc_ref[:, c * LANES:(c + 1) * LANES] = (alpha * acc_ref[:, c * LANES:(c + 1) * LANES]
                                                     + pv[:, c * LANES:(c + 1) * LANES])
        m_ref[...] = m_new
        return carry

    lax.fori_loop(0, (qi + 1) // 2, past_pair, 0)

    o = acc_ref[:, 0:HEAD_DIM] / acc_ref[:, HEAD_DIM:]
    for h in range(KV_GROUP):
        o_ref[:, h * HEAD_DIM:(h + 1) * HEAD_DIM] = o[h * MOBA_BLOCK:(h + 1) * MOBA_BLOCK].astype(o_ref.dtype)


def _attn_prompt(proj, batch, seq, k_col, v_col):
    nq = seq // MOBA_BLOCK
    assert seq % (2 * MOBA_BLOCK) == 0 and MOBA_TOPK <= nq <= LANES
    gw = KV_GROUP * HEAD_DIM
    rows = KV_GROUP * MOBA_BLOCK
    kblk, vblk = k_col // HEAD_DIM, v_col // HEAD_DIM
    return pl.pallas_call(
        _attn_prompt_kernel,
        grid=(batch, N_KV_HEADS, nq),
        in_specs=[
            pl.BlockSpec((MOBA_BLOCK, gw), lambda b, g, i: (b * nq + i, g)),
            pl.BlockSpec((seq, HEAD_DIM), lambda b, g, i: (b, kblk + g)),
            pl.BlockSpec((seq, HEAD_DIM), lambda b, g, i: (b, vblk + g)),
        ],
        out_specs=pl.BlockSpec((MOBA_BLOCK, gw), lambda b, g, i: (b * nq + i, g)),
        out_shape=jax.ShapeDtypeStruct((batch * seq, N_HEADS * HEAD_DIM), BF16),
        scratch_shapes=[
            pltpu.VMEM((seq, HEAD_DIM + LANES), BF16),
            pltpu.VMEM((seq, HEAD_DIM + LANES), BF16),
            pltpu.VMEM((LANES, HEAD_DIM), F32),
            pltpu.VMEM((rows, LANES), F32),
            pltpu.VMEM((rows, HEAD_DIM + LANES), F32),
        ],
        compiler_params=_cparams(("parallel", "parallel", "arbitrary"), 40),
        name="attn_prompt",
    )(proj, proj, proj)


def _conv_seq_kernel(b_ref, c_ref, h_ref, cp_ref, hp_ref, st_ref, w_ref, cb_ref, ut_ref, *, seq):
    i = pl.program_id(0)
    tm = c_ref.shape[0]
    u = c_ref[...] * h_ref[...]
    prev = jnp.where((i * tm) % seq == 0, st_ref[0], cp_ref[...] * hp_ref[...])
    p1 = prev[SUBLANES - 1:SUBLANES, :]
    p2 = prev[SUBLANES - 2:SUBLANES - 1, :]
    row = lax.broadcasted_iota(jnp.int32, u.shape, 0)
    u1 = jnp.where(row == 0, p1, pltpu.roll(u, 1, axis=0))
    u2 = jnp.where(row == 0, p2, jnp.where(row == 1, p1, pltpu.roll(u, 2, axis=0)))
    w = w_ref[...]
    conv = w[0:1, :] * u2 + w[1:2, :] * u1 + w[2:3, :] * u
    cb_ref[...] = (b_ref[...] * conv).astype(BF16)
    ut_ref[0] = u[tm - SUBLANES:tm, :]


def _conv_seq(proj, state8, conv_w, seq, b_col, cw):
    m = proj.shape[0]
    tm = _row_tile(seq, 512)
    nt = m // tm
    cb = b_col // cw
    pstep = tm // SUBLANES
    return pl.pallas_call(
        functools.partial(_conv_seq_kernel, seq=seq),
        grid=(nt,),
        in_specs=[
            pl.BlockSpec((tm, cw), lambda i: (i, cb)),
            pl.BlockSpec((tm, cw), lambda i: (i, cb + 1)),
            pl.BlockSpec((tm, cw), lambda i: (i, cb + 2)),
            pl.BlockSpec((SUBLANES, cw), lambda i: (jnp.maximum(i * pstep - 1, 0), cb + 1)),
            pl.BlockSpec((SUBLANES, cw), lambda i: (jnp.maximum(i * pstep - 1, 0), cb + 2)),
            pl.BlockSpec((1, SUBLANES, cw), lambda i: ((i * tm) // seq, 0, 0)),
            pl.BlockSpec((CONV_K, cw), lambda i: (0, 0)),
        ],
        out_specs=[
            pl.BlockSpec((tm, cw), lambda i: (i, 0)),
            pl.BlockSpec((1, SUBLANES, cw), lambda i: (i, 0, 0)),
        ],
        out_shape=[
            jax.ShapeDtypeStruct((m, cw), BF16),
            jax.ShapeDtypeStruct((nt, SUBLANES, cw), F32),
        ],
        compiler_params=_cparams(("parallel",), 32),
        name="conv_seq",
    )(proj, proj, proj, proj, proj, state8, conv_w)


def _conv_step_kernel(b_ref, c_ref, h_ref, st_ref, w_ref, cb_ref, u_ref):
    cw = c_ref.shape[1]
    u = c_ref[...] * h_ref[...]
    w = w_ref[...]
    conv = w[0:1, :] * st_ref[:, 0:cw] + w[1:2, :] * st_ref[:, cw:2 * cw] + w[2:3, :] * u
    cb_ref[...] = (b_ref[...] * conv).astype(BF16)
    u_ref[...] = u


def _conv_step(proj, state, conv_w, b_col, cw):
    m = proj.shape[0]
    cb = b_col // cw
    return pl.pallas_call(
        _conv_step_kernel,
        grid=(1,),
        in_specs=[
            pl.BlockSpec((m, cw), lambda i: (0, cb)),
            pl.BlockSpec((m, cw), lambda i: (0, cb + 1)),
            pl.BlockSpec((m, cw), lambda i: (0, cb + 2)),
            pl.BlockSpec((m, (CONV_K - 1) * cw), lambda i: (0, 0)),
            pl.BlockSpec((CONV_K, cw), lambda i: (0, 0)),
        ],
        out_specs=[pl.BlockSpec((m, cw), lambda i: (0, 0)), pl.BlockSpec((m, cw), lambda i: (0, 0))],
        out_shape=[jax.ShapeDtypeStruct((m, cw), BF16), jax.ShapeDtypeStruct((m, cw), F32)],
        compiler_params=_cparams(("arbitrary",), 32),
        name="conv_step",
    )(proj, proj, proj, state, conv_w)


def _merge_kernel(a_ref, cb_ref, ga_ref, gc_ref, wa_ref, wc_ref, o_ref):
    ya = jnp.dot(a_ref[...], wa_ref[...].astype(BF16), preferred_element_type=F32)
    yc = jnp.dot(cb_ref[...], wc_ref[...].astype(BF16), preferred_element_type=F32)
    o_ref[...] = (jax.nn.sigmoid(ga_ref[...]) * ya + jax.nn.sigmoid(gc_ref[...]) * yc).astype(o_ref.dtype)


def _merge(a, cb, proj, w_attn_br, w_conv_br, ga_col, gc_col):
    m, qw = a.shape
    cw = cb.shape[1]
    d = w_attn_br.shape[1]
    tm, tn = _row_tile(m), COL_TILE
    ga, gc = ga_col // tn, gc_col // tn
    return pl.pallas_call(
        _merge_kernel,
        grid=(m // tm, d // tn),
        in_specs=[
            pl.BlockSpec((tm, qw), lambda i, j: (i, 0)),
            pl.BlockSpec((tm, cw), lambda i, j: (i, 0)),
            pl.BlockSpec((tm, tn), lambda i, j: (i, ga + j)),
            pl.BlockSpec((tm, tn), lambda i, j: (i, gc + j)),
            pl.BlockSpec((qw, tn), lambda i, j: (0, j)),
            pl.BlockSpec((cw, tn), lambda i, j: (0, j)),
        ],
        out_specs=pl.BlockSpec((tm, tn), lambda i, j: (i, j)),
        out_shape=jax.ShapeDtypeStruct((m, d), BF16),
        compiler_params=_cparams(("parallel", "parallel"), 48),
        name="merge",
    )(a, cb, proj, proj, w_attn_br, w_conv_br)


def _outproj_kernel(m_ref, w_ref, x_ref, o_ref):
    o_ref[...] = x_ref[...] + jnp.dot(m_ref[...], w_ref[...].astype(BF16), preferred_element_type=F32)


def _outproj(merged, w_o, x):
    m, d = merged.shape
    n = w_o.shape[1]
    tm, tn = _row_tile(m), COL_TILE
    return pl.pallas_call(
        _outproj_kernel,
        grid=(m // tm, n // tn),
        in_specs=[
            pl.BlockSpec((tm, d), lambda i, j: (i, 0)),
            pl.BlockSpec((d, tn), lambda i, j: (0, j)),
            pl.BlockSpec((tm, tn), lambda i, j: (i, j)),
        ],
        out_specs=pl.BlockSpec((tm, tn), lambda i, j: (i, j)),
        out_shape=jax.ShapeDtypeStruct((m, n), F32),
        compiler_params=_cparams(("parallel", "parallel"), 48),
        name="outproj",
    )(merged, w_o, x)


def _ffn_up_kernel(x_ref, g_ref, wg_ref, wu_ref, o_ref, hn_ref):
    @pl.when(pl.program_id(1) == 0)
    def _():
        hn_ref[...] = _rmsnorm_f32(x_ref[...], g_ref[...]).astype(BF16)

    hn = hn_ref[...]
    gate = jnp.dot(hn, wg_ref[...].astype(BF16), preferred_element_type=F32)
    up = jnp.dot(hn, wu_ref[...].astype(BF16), preferred_element_type=F32)
    o_ref[...] = (gate * jax.nn.sigmoid(gate) * up).astype(o_ref.dtype)


def _ffn_up(x, ln2, w_gate, w_up):
    m, d = x.shape
    f = w_gate.shape[1]
    tm, tn = _row_tile(m), COL_TILE
    assert f % tn == 0
    return pl.pallas_call(
        _ffn_up_kernel,
        grid=(m // tm, f // tn),
        in_specs=[
            pl.BlockSpec((tm, d), lambda i, j: (i, 0)),
            pl.BlockSpec((1, d), lambda i, j: (0, 0)),
            pl.BlockSpec((d, tn), lambda i, j: (0, j)),
            pl.BlockSpec((d, tn), lambda i, j: (0, j)),
        ],
        out_specs=pl.BlockSpec((tm, tn), lambda i, j: (i, j)),
        out_shape=jax.ShapeDtypeStruct((m, f), BF16),
        scratch_shapes=[pltpu.VMEM((tm, d), BF16)],
        compiler_params=_cparams(("parallel", "arbitrary"), 48),
        name="ffn_up",
    )(x, ln2.reshape(1, d), w_gate, w_up)


def _ffn_down_kernel(h_ref, w_ref, x_ref, g_ref, o_ref, acc_ref):
    k = pl.program_id(1)

    @pl.when(k == 0)
    def _():
        acc_ref[...] = x_ref[...]

    acc_ref[...] += jnp.dot(h_ref[...], w_ref[...].astype(BF16), preferred_element_type=F32)

    @pl.when(k == pl.num_programs(1) - 1)
    def _():
        o_ref[...] = _rmsnorm_f32(acc_ref[...], g_ref[...])


def _ffn_down(hmid, w_down, x, ln_f):
    m, f = hmid.shape
    d = w_down.shape[1]
    tm, tk = _row_tile(m, 512), COL_TILE
    return pl.pallas_call(
        _ffn_down_kernel,
        grid=(m // tm, f // tk),
        in_specs=[
            pl.BlockSpec((tm, tk), lambda i, k: (i, k)),
            pl.BlockSpec((tk, d), lambda i, k: (k, 0)),
            pl.BlockSpec((tm, d), lambda i, k: (i, 0)),
            pl.BlockSpec((1, d), lambda i, k: (0, 0)),
        ],
        out_specs=pl.BlockSpec((tm, d), lambda i, k: (i, 0)),
        out_shape=jax.ShapeDtypeStruct((m, d), F32),
        scratch_shapes=[pltpu.VMEM((tm, d), F32)],
        compiler_params=_cparams(("parallel", "arbitrary"), 48),
        name="ffn_down",
    )(hmid, w_down, x, ln_f.reshape(1, d))


def _sample_select_kernel(pt_ref, q_ref, *refs):
    page_refs = refs[:PAGES_PER_STEP]
    sel_ref = refs[PAGES_PER_STEP]
    kmean_ref = refs[PAGES_PER_STEP + 1]
    p = pl.program_id(1)
    nb = kmean_ref.shape[1]
    page_rows = page_refs[0].shape[1]
    pages_per_block = MOBA_BLOCK * N_KV_HEADS // page_rows
    blocks_per_step = PAGES_PER_STEP // pages_per_block

    row = lax.broadcasted_iota(jnp.int32, (blocks_per_step, HEAD_DIM), 0)
    means = [jnp.zeros((blocks_per_step, HEAD_DIM), F32) for _ in range(N_KV_HEADS)]
    for r in range(blocks_per_step):
        tot = jnp.zeros((SUBLANES, HEAD_DIM), F32)
        for t in range(pages_per_block):
            x = page_refs[r * pages_per_block + t][0]
            tot = tot + jnp.sum(x.reshape(page_rows // SUBLANES, SUBLANES, HEAD_DIM), axis=0)
        head_sum = tot[0:N_KV_HEADS, :]
        for c in range(1, SUBLANES // N_KV_HEADS):
            head_sum = head_sum + tot[c * N_KV_HEADS:(c + 1) * N_KV_HEADS, :]
        head_mean = head_sum * (1.0 / MOBA_BLOCK)
        for g in range(N_KV_HEADS):
            means[g] = jnp.where(row == r, head_mean[g:g + 1, :], means[g])
    base = pl.multiple_of(p * blocks_per_step, blocks_per_step)
    for g in range(N_KV_HEADS):
        kmean_ref[g, pl.ds(base, blocks_per_step), :] = means[g]

    @pl.when(p == pl.num_programs(1) - 1)
    def _():
        q = q_ref[0]
        head_group = lax.broadcasted_iota(jnp.int32, (N_HEADS, nb), 0) // KV_GROUP
        gate = jnp.zeros((N_HEADS, nb), F32)
        for g in range(N_KV_HEADS):
            gg = lax.dot_general(q, kmean_ref[g], (((1,), (1,)), ((), ())),
                                 precision=lax.Precision.HIGHEST, preferred_element_type=F32)
            gate = jnp.where(head_group == g, gg, gate)
        blk = lax.broadcasted_iota(jnp.int32, (N_HEADS, nb), 1).astype(F32)
        lane = lax.broadcasted_iota(jnp.int32, (N_HEADS, LANES), 1)
        out = jnp.zeros((N_HEADS, LANES), F32)
        for r in range(MOBA_TOPK):
            top = jnp.max(gate, axis=1, keepdims=True)
            first = jnp.min(jnp.where(gate == top, blk, float(nb)), axis=1, keepdims=True)
            out = jnp.where(lane == r, first, out)
            gate = jnp.where(blk == first, -jnp.inf, gate)
        sel_ref[0] = out.astype(jnp.int32)


def _sample_select(q_s, cache_k3, page_table):
    db, n_pages = page_table.shape
    _, page_rows, hd = cache_k3.shape
    page = page_rows // N_KV_HEADS
    assert MOBA_BLOCK % page == 0 and (n_pages * page) % MOBA_BLOCK == 0 and SUBLANES % N_KV_HEADS == 0
    assert n_pages % PAGES_PER_STEP == 0 and PAGES_PER_STEP % (MOBA_BLOCK // page) == 0
    assert (PAGES_PER_STEP * page // MOBA_BLOCK) % SUBLANES == 0
    nb = n_pages * page // MOBA_BLOCK
    assert MOBA_TOPK <= nb
    steps = n_pages // PAGES_PER_STEP
    page_specs = [
        pl.BlockSpec((1, page_rows, hd), functools.partial(
            lambda b, p, pt, t: (pt[b, p * PAGES_PER_STEP + t], 0, 0), t=t))
        for t in range(PAGES_PER_STEP)
    ]
    grid_spec = pltpu.PrefetchScalarGridSpec(
        num_scalar_prefetch=1,
        grid=(db, steps),
        in_specs=[pl.BlockSpec((1, N_HEADS, HEAD_DIM), lambda b, p, pt: (b, 0, 0))] + page_specs,
        out_specs=pl.BlockSpec((1, N_HEADS, LANES), lambda b, p, pt: (b, 0, 0)),
        scratch_shapes=[pltpu.VMEM((N_KV_HEADS, nb, HEAD_DIM), F32)],
    )
    sel = pl.pallas_call(
        _sample_select_kernel,
        grid_spec=grid_spec,
        out_shape=jax.ShapeDtypeStruct((db, N_HEADS, LANES), jnp.int32),
        compiler_params=_cparams(("parallel", "arbitrary"), 40),
        name="sample_select",
    )(page_table, q_s, *([cache_k3] * PAGES_PER_STEP))
    return sel[:, :, :MOBA_TOPK]


def _sample_attn_kernel(pt_ref, sel_ref, q_ref, ks_ref, vs_ref, kc_ref, vc_ref, o_ref, kbuf, vbuf, sem, *,
                        pages_per_block):
    n_sel = kbuf.shape[1]
    n_heads = pl.num_programs(1)
    step = pl.program_id(0) * n_heads + pl.program_id(1)
    n_steps = pl.num_programs(0) * n_heads
    slot = step % 2

    def slab_copies(s, sl):
        b, h = s // n_heads, s % n_heads
        g = h // KV_GROUP
        out = []
        for t in range(n_sel):
            blk = sel_ref[b, h * MOBA_TOPK + t // pages_per_block]
            pg = pt_ref[b, blk * pages_per_block + t % pages_per_block]
            out.append(pltpu.make_async_copy(kc_ref.at[pg, :, g, :], kbuf.at[sl, t], sem.at[0, sl, t]))
            out.append(pltpu.make_async_copy(vc_ref.at[pg, :, g, :], vbuf.at[sl, t], sem.at[1, sl, t]))
        return out

    @pl.when(step == 0)
    def _():
        for c in slab_copies(step, slot):
            c.start()

    @pl.when(step + 1 < n_steps)
    def _():
        for c in slab_copies(step + 1, 1 - slot):
            c.start()

    for c in slab_copies(step, slot):
        c.wait()

    q = q_ref[0] * ATTN_SCALE
    scores = [jnp.sum(kbuf[slot, t] * q, axis=1, keepdims=True) for t in range(n_sel)]
    s_new = jnp.sum(ks_ref[0] * q, axis=1, keepdims=True)
    m = s_new
    for s in scores:
        m = jnp.maximum(m, jnp.max(s, axis=0, keepdims=True))
    p_new = jnp.exp(s_new - m)
    l = p_new
    o = p_new * vs_ref[0]
    for t in range(n_sel):
        p = jnp.exp(scores[t] - m)
        l = l + jnp.sum(p, axis=0, keepdims=True)
        o = o + jnp.sum(p * vbuf[slot, t], axis=0, keepdims=True)
    o_ref[0] = (o / l).astype(o_ref.dtype)


def _sample_attn(q_s, k_s, v_s, cache_k4, cache_v4, page_table, sel):
    db, n_pages = page_table.shape
    page = cache_k4.shape[1]
    ppb = MOBA_BLOCK // page
    n_sel = MOBA_TOPK * ppb

    grid_spec = pltpu.PrefetchScalarGridSpec(
        num_scalar_prefetch=2,
        grid=(db, N_HEADS),
        in_specs=[
            pl.BlockSpec((1, 1, HEAD_DIM), lambda b, h, pt, sl: (b * N_HEADS + h, 0, 0)),
            pl.BlockSpec((1, 1, HEAD_DIM), lambda b, h, pt, sl: (b * N_KV_HEADS + h // KV_GROUP, 0, 0)),
            pl.BlockSpec((1, 1, HEAD_DIM), lambda b, h, pt, sl: (b * N_KV_HEADS + h // KV_GROUP, 0, 0)),
            pl.BlockSpec(memory_space=pl.ANY),
            pl.BlockSpec(memory_space=pl.ANY),
        ],
        out_specs=pl.BlockSpec((1, 1, HEAD_DIM), lambda b, h, pt, sl: (b * N_HEADS + h, 0, 0)),
        scratch_shapes=[
            pltpu.VMEM((2, n_sel, page, HEAD_DIM), F32),
            pltpu.VMEM((2, n_sel, page, HEAD_DIM), F32),
            pltpu.SemaphoreType.DMA((2, 2, n_sel)),
        ],
    )
    return pl.pallas_call(
        functools.partial(_sample_attn_kernel, pages_per_block=ppb),
        grid_spec=grid_spec,
        out_shape=jax.ShapeDtypeStruct((db * N_HEADS, 1, HEAD_DIM), F32),
        compiler_params=_cparams(("arbitrary", "arbitrary"), 32),
        name="sample_attn",
    )(page_table, sel, q_s, k_s, v_s, cache_k4, cache_v4)


def _rope_tables(pos):
    half = HEAD_DIM // 2
    inv = 1.0 / (ROPE_THETA ** (jnp.arange(half, dtype=F32) / half))
    ang = pos.astype(F32)[:, None] * inv[None, :]
    cos, sin = jnp.cos(ang), jnp.sin(ang)
    return jnp.concatenate([cos, cos], axis=-1), jnp.concatenate([-sin, sin], axis=-1)


def _trunk_tail(x, a, cb, proj, w_attn_br, w_conv_br, w_o, ln2, w_ff_gate, w_ff_up, w_ff_down, ln_f,
                ga_col, gc_col):
    merged = _merge(a, cb, proj, w_attn_br, w_conv_br, ga_col, gc_col)
    x1 = _outproj(merged, w_o, x)
    hmid = _ffn_up(x1, ln2, w_ff_gate, w_ff_up)
    return _ffn_down(hmid, w_ff_down, x1, ln_f)


def kernel(x_prompt, x_sample, cache_k, cache_v, state_conv, page_table, ln1, w_in, conv_w, w_attn_br,
           w_conv_br, w_o, ln2, w_ff_gate, w_ff_up, w_ff_down, ln_f):
    batch, seq, d = x_prompt.shape
    db, dec_seq, _ = x_sample.shape
    depth, n_phys, page, kvh, hd = cache_k.shape
    assert depth == 1 and dec_seq == 1 and kvh == N_KV_HEADS and hd == HEAD_DIM
    qw, kvw = N_HEADS * HEAD_DIM, N_KV_HEADS * HEAD_DIM
    cw = conv_w.shape[-1]
    k_col, v_col, b_col = qw, qw + kvw, qw + 2 * kvw
    ga_col = b_col + 3 * cw
    gc_col = ga_col + d
    rope_cols = qw + kvw
    past = page_table.shape[1] * page
    tail = (ln2[0], w_ff_gate[0], w_ff_up[0], w_ff_down[0], ln_f)

    mp = batch * seq
    xp = x_prompt.reshape(mp, d)
    cos_p, sin_p = _rope_tables(jnp.tile(jnp.arange(seq, dtype=jnp.int32), batch))
    proj_p = _inproj(xp, ln1[0], w_in[0], cos_p, sin_p, rope_cols)
    a_p = _attn_prompt(proj_p, batch, seq, k_col, v_col)
    zero_state = jnp.zeros((batch, SUBLANES, cw), F32)
    cb_p, utail_p = _conv_seq(proj_p, zero_state, conv_w[0], seq, b_col, cw)
    y_p = _trunk_tail(xp, a_p, cb_p, proj_p, w_attn_br[0], w_conv_br[0], w_o[0], *tail, ga_col, gc_col)
    tiles_per_seq = utail_p.shape[0] // batch
    conv_p = utail_p.reshape(batch, tiles_per_seq, SUBLANES, cw)[:, -1, SUBLANES - (CONV_K - 1):, :]

    xs = x_sample.reshape(db, d)
    cos_s, sin_s = _rope_tables(jnp.full((db,), past, jnp.int32))
    proj_s = _inproj(xs, ln1[0], w_in[0], cos_s, sin_s, rope_cols)
    q_s = proj_s[:, :qw]
    k_s = proj_s[:, k_col:k_col + kvw]
    v_s = proj_s[:, v_col:v_col + kvw]
    cache_k3 = cache_k.reshape(n_phys, page * N_KV_HEADS, HEAD_DIM)
    cache_k4 = cache_k.reshape(n_phys, page, N_KV_HEADS, HEAD_DIM)
    cache_v4 = cache_v.reshape(n_phys, page, N_KV_HEADS, HEAD_DIM)
    sel = _sample_select(q_s.reshape(db, N_HEADS, HEAD_DIM), cache_k3, page_table)
    a_s = _sample_attn(q_s.reshape(db * N_HEADS, 1, HEAD_DIM), k_s.reshape(db * N_KV_HEADS, 1, HEAD_DIM),
                       v_s.reshape(db * N_KV_HEADS, 1, HEAD_DIM), cache_k4, cache_v4, page_table,
                       sel.reshape(db, N_HEADS * MOBA_TOPK)).reshape(db, qw).astype(BF16)
    state = state_conv[0].astype(F32)
    cb_s, u_s = _conv_step(proj_s, state.reshape(db, (CONV_K - 1) * cw), conv_w[0], b_col, cw)
    y_s = _trunk_tail(xs, a_s, cb_s, proj_s, w_attn_br[0], w_conv_br[0], w_o[0], *tail, ga_col, gc_col)
    conv_s = jnp.concatenate([state[:, 1:, :], u_s[:, None, :]], axis=1)

    return (
        y_p.reshape(batch, seq, d),
        y_s.reshape(db, 1, d),
        proj_p[:, k_col:k_col + kvw].reshape(1, batch, seq, N_KV_HEADS, HEAD_DIM),
        proj_p[:, v_col:v_col + kvw].reshape(1, batch, seq, N_KV_HEADS, HEAD_DIM),
        conv_p[None],
        k_s.reshape(1, db, 1, N_KV_HEADS, HEAD_DIM),
        v_s.reshape(1, db, 1, N_KV_HEADS, HEAD_DIM),
        conv_s[None],
    )
```

```python
import functools

import jax
import jax.numpy as jnp
from jax import lax
from jax.experimental import pallas as pl
from jax.experimental.pallas import tpu as pltpu

F32 = jnp.float32
BF16 = jnp.bfloat16

N_HEADS = 16
HEAD_DIM = 128
N_KV_HEADS = 4
KV_GROUP = N_HEADS // N_KV_HEADS
MOBA_BLOCK = 256
MOBA_TOPK = 3
ROPE_THETA = 10000.0
CONV_K = 3
RMS_EPS = 1e-6
ATTN_SCALE = HEAD_DIM ** -0.5
LOG2E = 1.4426950408889634
MASK_BIAS = -1e30

LANES = 128
SUBLANES = 8
MIB = 1 << 20
ROW_TILE = 1024
COL_TILE = 512
PAGES_PER_STEP = 32
ROW_GROUP = 256


def _cparams(semantics, vmem_mib):
    return pltpu.CompilerParams(dimension_semantics=semantics, vmem_limit_bytes=vmem_mib * MIB)


def _row_tile(m, cap=ROW_TILE):
    t = min(m, cap)
    assert m % t == 0 and t % SUBLANES == 0, (m, t)
    return t


def _rmsnorm_f32(x, g):
    return x * lax.rsqrt(jnp.mean(x * x, axis=-1, keepdims=True) + RMS_EPS) * g


def _inproj_kernel(x_ref, g_ref, w_ref, cos_ref, sin_ref, o_ref, xn_ref, *, rope_tiles):
    j = pl.program_id(1)

    @pl.when(j == 0)
    def _():
        xn_ref[...] = _rmsnorm_f32(x_ref[...], g_ref[...]).astype(BF16)

    acc = jnp.dot(xn_ref[...], w_ref[...], preferred_element_type=F32)

    @pl.when(j < rope_tiles)
    def _():
        cos = cos_ref[...]
        sin = sin_ref[...]
        for c in range(acc.shape[1] // HEAD_DIM):
            a = acc[:, c * HEAD_DIM:(c + 1) * HEAD_DIM]
            o_ref[:, c * HEAD_DIM:(c + 1) * HEAD_DIM] = a * cos + pltpu.roll(a, HEAD_DIM // 2, axis=1) * sin

    @pl.when(j >= rope_tiles)
    def _():
        o_ref[...] = acc


def _inproj(x, ln1, w_in, cos, sin, rope_cols):
    m, d = x.shape
    n = w_in.shape[1]
    tm, tn = _row_tile(m), COL_TILE
    assert n % tn == 0 and rope_cols % tn == 0
    return pl.pallas_call(
        functools.partial(_inproj_kernel, rope_tiles=rope_cols // tn),
        grid=(m // tm, n // tn),
        in_specs=[
            pl.BlockSpec((tm, d), lambda i, j: (i, 0)),
            pl.BlockSpec((1, d), lambda i, j: (0, 0)),
            pl.BlockSpec((d, tn), lambda i, j: (0, j)),
            pl.BlockSpec((tm, HEAD_DIM), lambda i, j: (i, 0)),
            pl.BlockSpec((tm, HEAD_DIM), lambda i, j: (i, 0)),
        ],
        out_specs=pl.BlockSpec((tm, tn), lambda i, j: (i, j)),
        out_shape=jax.ShapeDtypeStruct((m, n), F32),
        scratch_shapes=[pltpu.VMEM((tm, d), BF16)],
        compiler_params=_cparams(("parallel", "arbitrary"), 48),
        name="inproj",
    )(x, ln1.reshape(1, d), w_in, cos, sin)


def _attn_prompt_kernel(q_ref, k_ref, v_ref, o_ref, ka_ref, va_ref, kmean_ref, qa_ref, m_ref, acc_ref):
    qi = pl.program_id(2)
    seq = k_ref.shape[0]
    nb = seq // MOBA_BLOCK
    rows = KV_GROUP * MOBA_BLOCK
    pair = 2 * MOBA_BLOCK

    @pl.when(qi == 0)
    def _():
        k = k_ref[...]
        row_blk = lax.broadcasted_iota(jnp.int32, (seq, LANES), 0) // MOBA_BLOCK
        lane = lax.broadcasted_iota(jnp.int32, (seq, LANES), 1)
        ka_ref[:, 0:HEAD_DIM] = k.astype(BF16)
        ka_ref[:, HEAD_DIM:] = jnp.where(lane == row_blk, 1.0, 0.0).astype(BF16)
        va_ref[:, 0:HEAD_DIM] = v_ref[...].astype(BF16)
        va_ref[:, HEAD_DIM:] = jnp.ones((seq, LANES), BF16)
        kmean_ref[...] = jnp.zeros_like(kmean_ref)
        kmean_ref[0:nb, :] = jnp.mean(k.reshape(nb, MOBA_BLOCK, HEAD_DIM), axis=1)

    q = q_ref[...]
    q4 = jnp.concatenate([q[:, h * HEAD_DIM:(h + 1) * HEAD_DIM] for h in range(KV_GROUP)], axis=0)
    qs = (q4 * (ATTN_SCALE * LOG2E)).astype(BF16)

    gate = lax.dot_general(q4, kmean_ref[...], (((1,), (1,)), ((), ())),
                           precision=lax.Precision.HIGHEST, preferred_element_type=F32)
    lane_i = lax.broadcasted_iota(jnp.int32, (rows, LANES), 1)
    lane = lane_i.astype(F32)
    past = lane_i < qi
    gate = jnp.where(past, gate, -jnp.inf)
    bias = jnp.full((rows, LANES), MASK_BIAS, F32)
    for _ in range(MOBA_TOPK):
        top = jnp.max(gate, axis=1, keepdims=True)
        pick = lane == jnp.min(jnp.where(gate == top, lane, float(LANES)), axis=1, keepdims=True)
        bias = jnp.where(pick & past, 0.0, bias)
        gate = jnp.where(pick, -jnp.inf, gate)
    qa = jnp.concatenate([qs, bias.astype(BF16)], axis=1)

    own = pl.multiple_of(qi * MOBA_BLOCK, MOBA_BLOCK)
    s = lax.dot_general(qs, ka_ref[pl.ds(own, MOBA_BLOCK), 0:HEAD_DIM], (((1,), (1,)), ((), ())),
                        preferred_element_type=F32)
    qrow = lax.broadcasted_iota(jnp.int32, (rows, MOBA_BLOCK), 0) & (MOBA_BLOCK - 1)
    kcol = lax.broadcasted_iota(jnp.int32, (rows, MOBA_BLOCK), 1)
    s = jnp.where(kcol <= qrow, s, MASK_BIAS)
    m0 = jnp.broadcast_to(jnp.max(s, axis=1, keepdims=True), (rows, LANES))
    p = jnp.concatenate([jnp.exp2(s[:, c * LANES:(c + 1) * LANES] - m0)
                         for c in range(MOBA_BLOCK // LANES)], axis=1)
    m_ref[...] = m0
    acc_ref[...] = jnp.dot(p.astype(BF16), va_ref[pl.ds(own, MOBA_BLOCK), :], preferred_element_type=F32)

    qa_ref[...] = qa

    def past_pair(t, carry):
        start = pl.multiple_of(t * pair, pair)
        kt = ka_ref[pl.ds(start, pair), :]
        vt = va_ref[pl.ds(start, pair), :]
        for r0 in range(0, rows, ROW_GROUP):
            rs = slice(r0, r0 + ROW_GROUP)
            s = lax.dot_general(qa_ref[rs, :], kt, (((1,), (1,)), ((), ())), preferred_element_type=F32)
            m_old = m_ref[rs, :]
            m_new = jnp.maximum(m_old, jnp.max(s, axis=1, keepdims=True))
            alpha = jnp.exp2(m_old - m_new)
            p = jnp.concatenate([jnp.exp2(s[:, c * LANES:(c + 1) * LANES] - m_new)
                                 for c in range(pair // LANES)], axis=1)
            pv = jnp.dot(p.astype(BF16), vt, preferred_element_type=F32)
            for c in range(2):
                cs = slice(c * LANES, (c + 1) * LANES)
                acc_ref[rs, cs] = alpha * acc_ref[rs, cs] + pv[:, cs]
            m_ref[rs, :] = m_new
        return carry

    lax.fori_loop(0, (qi + 1) // 2, past_pair, 0)

    o = acc_ref[:, 0:HEAD_DIM] / acc_ref[:, HEAD_DIM:]
    for h in range(KV_GROUP):
        o_ref[:, h * HEAD_DIM:(h + 1) * HEAD_DIM] = o[h * MOBA_BLOCK:(h + 1) * MOBA_BLOCK].astype(o_ref.dtype)


def _attn_prompt(proj, batch, seq, k_col, v_col):
    nq = seq // MOBA_BLOCK
    assert seq % (2 * MOBA_BLOCK) == 0 and MOBA_TOPK <= nq <= LANES
    gw = KV_GROUP * HEAD_DIM
    rows = KV_GROUP * MOBA_BLOCK
    kblk, vblk = k_col // HEAD_DIM, v_col // HEAD_DIM
    return pl.pallas_call(
        _attn_prompt_kernel,
        grid=(batch, N_KV_HEADS, nq),
        in_specs=[
            pl.BlockSpec((MOBA_BLOCK, gw), lambda b, g, i: (b * nq + i, g)),
            pl.BlockSpec((seq, HEAD_DIM), lambda b, g, i: (b, kblk + g)),
            pl.BlockSpec((seq, HEAD_DIM), lambda b, g, i: (b, vblk + g)),
        ],
        out_specs=pl.BlockSpec((MOBA_BLOCK, gw), lambda b, g, i: (b * nq + i, g)),
        out_shape=jax.ShapeDtypeStruct((batch * seq, N_HEADS * HEAD_DIM), BF16),
        scratch_shapes=[
            pltpu.VMEM((seq, HEAD_DIM + LANES), BF16),
            pltpu.VMEM((seq, HEAD_DIM + LANES), BF16),
            pltpu.VMEM((LANES, HEAD_DIM), F32),
            pltpu.VMEM((rows, HEAD_DIM + LANES), BF16),
            pltpu.VMEM((rows, LANES), F32),
            pltpu.VMEM((rows, HEAD_DIM + LANES), F32),
        ],
        compiler_params=_cparams(("parallel", "parallel", "arbitrary"), 40),
        name="attn_prompt",
    )(proj, proj, proj)


def _conv_seq_kernel(b_ref, c_ref, h_ref, cp_ref, hp_ref, st_ref, w_ref, cb_ref, ut_ref, *, seq):
    i = pl.program_id(0)
    tm = c_ref.shape[0]
    u = c_ref[...] * h_ref[...]
    prev = jnp.where((i * tm) % seq == 0, st_ref[0], cp_ref[...] * hp_ref[...])
    p1 = prev[SUBLANES - 1:SUBLANES, :]
    p2 = prev[SUBLANES - 2:SUBLANES - 1, :]
    row = lax.broadcasted_iota(jnp.int32, u.shape, 0)
    u1 = jnp.where(row == 0, p1, pltpu.roll(u, 1, axis=0))
    u2 = jnp.where(row == 0, p2, jnp.where(row == 1, p1, pltpu.roll(u, 2, axis=0)))
    w = w_ref[...]
    conv = w[0:1, :] * u2 + w[1:2, :] * u1 + w[2:3, :] * u
    cb_ref[...] = (b_ref[...] * conv).astype(BF16)
    ut_ref[0] = u[tm - SUBLANES:tm, :]


def _conv_seq(proj, state8, conv_w, seq, b_col, cw):
    m = proj.shape[0]
    tm = _row_tile(seq, 512)
    nt = m // tm
    cb = b_col // cw
    pstep = tm // SUBLANES
    return pl.pallas_call(
        functools.partial(_conv_seq_kernel, seq=seq),
        grid=(nt,),
        in_specs=[
            pl.BlockSpec((tm, cw), lambda i: (i, cb)),
            pl.BlockSpec((tm, cw), lambda i: (i, cb + 1)),
            pl.BlockSpec((tm, cw), lambda i: (i, cb + 2)),
            pl.BlockSpec((SUBLANES, cw), lambda i: (jnp.maximum(i * pstep - 1, 0), cb + 1)),
            pl.BlockSpec((SUBLANES, cw), lambda i: (jnp.maximum(i * pstep - 1, 0), cb + 2)),
            pl.BlockSpec((1, SUBLANES, cw), lambda i: ((i * tm) // seq, 0, 0)),
            pl.BlockSpec((CONV_K, cw), lambda i: (0, 0)),
        ],
        out_specs=[
            pl.BlockSpec((tm, cw), lambda i: (i, 0)),
            pl.BlockSpec((1, SUBLANES, cw), lambda i: (i, 0, 0)),
        ],
        out_shape=[
            jax.ShapeDtypeStruct((m, cw), BF16),
            jax.ShapeDtypeStruct((nt, SUBLANES, cw), F32),
        ],
        compiler_params=_cparams(("parallel",), 32),
        name="conv_seq",
    )(proj, proj, proj, proj, proj, state8, conv_w)


def _conv_step_kernel(b_ref, c_ref, h_ref, st_ref, w_ref, cb_ref, u_ref):
    cw = c_ref.shape[1]
    u = c_ref[...] * h_ref[...]
    w = w_ref[...]
    conv = w[0:1, :] * st_ref[:, 0:cw] + w[1:2, :] * st_ref[:, cw:2 * cw] + w[2:3, :] * u
    cb_ref[...] = (b_ref[...] * conv).astype(BF16)
    u_ref[...] = u


def _conv_step(proj, state, conv_w, b_col, cw):
    m = proj.shape[0]
    cb = b_col // cw
    return pl.pallas_call(
        _conv_step_kernel,
        grid=(1,),
        in_specs=[
            pl.BlockSpec((m, cw), lambda i: (0, cb)),
            pl.BlockSpec((m, cw), lambda i: (0, cb + 1)),
            pl.BlockSpec((m, cw), lambda i: (0, cb + 2)),
            pl.BlockSpec((m, (CONV_K - 1) * cw), lambda i: (0, 0)),
            pl.BlockSpec((CONV_K, cw), lambda i: (0, 0)),
        ],
        out_specs=[pl.BlockSpec((m, cw), lambda i: (0, 0)), pl.BlockSpec((m, cw), lambda i: (0, 0))],
        out_shape=[jax.ShapeDtypeStruct((m, cw), BF16), jax.ShapeDtypeStruct((m, cw), F32)],
        compiler_params=_cparams(("arbitrary",), 32),
        name="conv_step",
    )(proj, proj, proj, state, conv_w)


def _merge_kernel(a_ref, cb_ref, ga_ref, gc_ref, wa_ref, wc_ref, o_ref):
    ya = jnp.dot(a_ref[...], wa_ref[...], preferred_element_type=F32)
    yc = jnp.dot(cb_ref[...], wc_ref[...], preferred_element_type=F32)
    o_ref[...] = (jax.nn.sigmoid(ga_ref[...]) * ya + jax.nn.sigmoid(gc_ref[...]) * yc).astype(o_ref.dtype)


def _merge(a, cb, proj, w_attn_br, w_conv_br, ga_col, gc_col):
    m, qw = a.shape
    cw = cb.shape[1]
    d = w_attn_br.shape[1]
    tm, tn = _row_tile(m), COL_TILE
    ga, gc = ga_col // tn, gc_col // tn
    return pl.pallas_call(
        _merge_kernel,
        grid=(m // tm, d // tn),
        in_specs=[
            pl.BlockSpec((tm, qw), lambda i, j: (i, 0)),
            pl.BlockSpec((tm, cw), lambda i, j: (i, 0)),
            pl.BlockSpec((tm, tn), lambda i, j: (i, ga + j)),
            pl.BlockSpec((tm, tn), lambda i, j: (i, gc + j)),
            pl.BlockSpec((qw, tn), lambda i, j: (0, j)),
            pl.BlockSpec((cw, tn), lambda i, j: (0, j)),
        ],
        out_specs=pl.BlockSpec((tm, tn), lambda i, j: (i, j)),
        out_shape=jax.ShapeDtypeStruct((m, d), BF16),
        compiler_params=_cparams(("parallel", "parallel"), 48),
        name="merge",
    )(a, cb, proj, proj, w_attn_br, w_conv_br)


def _outproj_kernel(m_ref, w_ref, x_ref, o_ref):
    o_ref[...] = x_ref[...] + jnp.dot(m_ref[...], w_ref[...], preferred_element_type=F32)


def _outproj(merged, w_o, x):
    m, d = merged.shape
    n = w_o.shape[1]
    tm, tn = _row_tile(m), COL_TILE
    return pl.pallas_call(
        _outproj_kernel,
        grid=(m // tm, n // tn),
        in_specs=[
            pl.BlockSpec((tm, d), lambda i, j: (i, 0)),
            pl.BlockSpec((d, tn), lambda i, j: (0, j)),
            pl.BlockSpec((tm, tn), lambda i, j: (i, j)),
        ],
        out_specs=pl.BlockSpec((tm, tn), lambda i, j: (i, j)),
        out_shape=jax.ShapeDtypeStruct((m, n), F32),
        compiler_params=_cparams(("parallel", "parallel"), 48),
        name="outproj",
    )(merged, w_o, x)


def _ffn_up_kernel(x_ref, g_ref, wg_ref, wu_ref, o_ref, hn_ref):
    @pl.when(pl.program_id(1) == 0)
    def _():
        hn_ref[...] = _rmsnorm_f32(x_ref[...], g_ref[...]).astype(BF16)

    hn = hn_ref[...]
    gate = jnp.dot(hn, wg_ref[...], preferred_element_type=F32)
    up = jnp.dot(hn, wu_ref[...], preferred_element_type=F32)
    o_ref[...] = (gate * jax.nn.sigmoid(gate) * up).astype(o_ref.dtype)


def _ffn_up(x, ln2, w_gate, w_up):
    m, d = x.shape
    f = w_gate.shape[1]
    tm, tn = _row_tile(m), COL_TILE
    assert f % tn == 0
    return pl.pallas_call(
        _ffn_up_kernel,
        grid=(m // tm, f // tn),
        in_specs=[
            pl.BlockSpec((tm, d), lambda i, j: (i, 0)),
            pl.BlockSpec((1, d), lambda i, j: (0, 0)),
            pl.BlockSpec((d, tn), lambda i, j: (0, j)),
            pl.BlockSpec((d, tn), lambda i, j: (0, j)),
        ],
        out_specs=pl.BlockSpec((tm, tn), lambda i, j: (i, j)),
        out_shape=jax.ShapeDtypeStruct((m, f), BF16),
        scratch_shapes=[pltpu.VMEM((tm, d), BF16)],
        compiler_params=_cparams(("parallel", "arbitrary"), 48),
        name="ffn_up",
    )(x, ln2.reshape(1, d), w_gate, w_up)


def _ffn_down_kernel(h_ref, w_ref, x_ref, g_ref, o_ref):
    k = pl.program_id(1)

    @pl.when(k == 0)
    def _():
        o_ref[...] = x_ref[...]

    o_ref[...] += jnp.dot(h_ref[...], w_ref[...], preferred_element_type=F32)

    @pl.when(k == pl.num_programs(1) - 1)
    def _():
        o_ref[...] = _rmsnorm_f32(o_ref[...], g_ref[...])


def _ffn_down(hmid, w_down, x, ln_f):
    m, f = hmid.shape
    d = w_down.shape[1]
    tm, tk = _row_tile(m), COL_TILE
    return pl.pallas_call(
        _ffn_down_kernel,
        grid=(m // tm, f // tk),
        in_specs=[
            pl.BlockSpec((tm, tk), lambda i, k: (i, k)),
            pl.BlockSpec((tk, d), lambda i, k: (k, 0)),
            pl.BlockSpec((tm, d), lambda i, k: (i, 0)),
            pl.BlockSpec((1, d), lambda i, k: (0, 0)),
        ],
        out_specs=pl.BlockSpec((tm, d), lambda i, k: (i, 0)),
        out_shape=jax.ShapeDtypeStruct((m, d), F32),
        compiler_params=_cparams(("parallel", "arbitrary"), 48),
        name="ffn_down",
    )(hmid, w_down, x, ln_f.reshape(1, d))


def _sample_select_kernel(pt_ref, q_ref, *refs):
    page_refs = refs[:PAGES_PER_STEP]
    sel_ref = refs[PAGES_PER_STEP]
    kmean_ref = refs[PAGES_PER_STEP + 1]
    p = pl.program_id(1)
    nb = kmean_ref.shape[1]
    page_rows = page_refs[0].shape[1]
    pages_per_block = MOBA_BLOCK * N_KV_HEADS // page_rows
    blocks_per_step = PAGES_PER_STEP // pages_per_block

    row = lax.broadcasted_iota(jnp.int32, (blocks_per_step, HEAD_DIM), 0)
    means = [jnp.zeros((blocks_per_step, HEAD_DIM), F32) for _ in range(N_KV_HEADS)]
    for r in range(blocks_per_step):
        tot = jnp.zeros((SUBLANES, HEAD_DIM), F32)
        for t in range(pages_per_block):
            x = page_refs[r * pages_per_block + t][0]
            tot = tot + jnp.sum(x.reshape(page_rows // SUBLANES, SUBLANES, HEAD_DIM), axis=0)
        head_sum = tot[0:N_KV_HEADS, :]
        for c in range(1, SUBLANES // N_KV_HEADS):
            head_sum = head_sum + tot[c * N_KV_HEADS:(c + 1) * N_KV_HEADS, :]
        head_mean = head_sum * (1.0 / MOBA_BLOCK)
        for g in range(N_KV_HEADS):
            means[g] = jnp.where(row == r, head_mean[g:g + 1, :], means[g])
    base = pl.multiple_of(p * blocks_per_step, blocks_per_step)
    for g in range(N_KV_HEADS):
        kmean_ref[g, pl.ds(base, blocks_per_step), :] = means[g]

    @pl.when(p == pl.num_programs(1) - 1)
    def _():
        q = q_ref[0]
        head_group = lax.broadcasted_iota(jnp.int32, (N_HEADS, nb), 0) // KV_GROUP
        gate = jnp.zeros((N_HEADS, nb), F32)
        for g in range(N_KV_HEADS):
            gg = lax.dot_general(q, kmean_ref[g], (((1,), (1,)), ((), ())),
                                 precision=lax.Precision.HIGHEST, preferred_element_type=F32)
            gate = jnp.where(head_group == g, gg, gate)
        blk = lax.broadcasted_iota(jnp.int32, (N_HEADS, nb), 1).astype(F32)
        lane = lax.broadcasted_iota(jnp.int32, (N_HEADS, LANES), 1)
        out = jnp.zeros((N_HEADS, LANES), F32)
        for r in range(MOBA_TOPK):
            top = jnp.max(gate, axis=1, keepdims=True)
            first = jnp.min(jnp.where(gate == top, blk, float(nb)), axis=1, keepdims=True)
            out = jnp.where(lane == r, first, out)
            gate = jnp.where(blk == first, -jnp.inf, gate)
        sel_ref[0] = out.astype(jnp.int32)


def _sample_select(q_s, cache_k3, page_table):
    db, n_pages = page_table.shape
    _, page_rows, hd = cache_k3.shape
    page = page_rows // N_KV_HEADS
    assert MOBA_BLOCK % page == 0 and (n_pages * page) % MOBA_BLOCK == 0 and SUBLANES % N_KV_HEADS == 0
    assert n_pages % PAGES_PER_STEP == 0 and PAGES_PER_STEP % (MOBA_BLOCK // page) == 0
    assert (PAGES_PER_STEP * page // MOBA_BLOCK) % SUBLANES == 0
    nb = n_pages * page // MOBA_BLOCK
    assert MOBA_TOPK <= nb
    steps = n_pages // PAGES_PER_STEP
    page_specs = [
        pl.BlockSpec((1, page_rows, hd), functools.partial(
            lambda b, p, pt, t: (pt[b, p * PAGES_PER_STEP + t], 0, 0), t=t))
        for t in range(PAGES_PER_STEP)
    ]
    grid_spec = pltpu.PrefetchScalarGridSpec(
        num_scalar_prefetch=1,
        grid=(db, steps),
        in_specs=[pl.BlockSpec((1, N_HEADS, HEAD_DIM), lambda b, p, pt: (b, 0, 0))] + page_specs,
        out_specs=pl.BlockSpec((1, N_HEADS, LANES), lambda b, p, pt: (b, 0, 0)),
        scratch_shapes=[pltpu.VMEM((N_KV_HEADS, nb, HEAD_DIM), F32)],
    )
    sel = pl.pallas_call(
        _sample_select_kernel,
        grid_spec=grid_spec,
        out_shape=jax.ShapeDtypeStruct((db, N_HEADS, LANES), jnp.int32),
        compiler_params=_cparams(("parallel", "arbitrary"), 40),
        name="sample_select",
    )(page_table, q_s, *([cache_k3] * PAGES_PER_STEP))
    return sel[:, :, :MOBA_TOPK]


def _sample_attn_kernel(pt_ref, sel_ref, q_ref, ks_ref, vs_ref, kc_ref, vc_ref, o_ref, kbuf, vbuf, sem, *,
                        pages_per_block):
    b = pl.program_id(0)
    slot = b % 2
    _, n_heads, n_sel, page, hd = kbuf.shape

    def head_copies(bb, h, sl):
        g = h // KV_GROUP
        out = []
        for t in range(n_sel):
            blk = sel_ref[bb, h * MOBA_TOPK + t // pages_per_block]
            pg = pt_ref[bb, blk * pages_per_block + t % pages_per_block]
            out.append(pltpu.make_async_copy(kc_ref.at[pg, :, g, :], kbuf.at[sl, h, t], sem.at[0, sl, h]))
            out.append(pltpu.make_async_copy(vc_ref.at[pg, :, g, :], vbuf.at[sl, h, t], sem.at[1, sl, h]))
        return out

    def start_all(bb, sl):
        def start_head(h, carry):
            for c in head_copies(bb, h, sl):
                c.start()
            return carry
        lax.fori_loop(0, n_heads, start_head, 0)

    @pl.when(b == 0)
    def _():
        start_all(b, slot)

    @pl.when(b + 1 < pl.num_programs(0))
    def _():
        start_all(b + 1, 1 - slot)

    ones = jnp.ones((2 * hd, LANES), BF16)

    def one_head(h):
        g = h // KV_GROUP
        q = q_ref[0, pl.ds(h, 1), :] * ATTN_SCALE
        prod = kbuf[slot, h].reshape(n_sel * page, hd) * q
        hi = prod.astype(BF16)
        lo = (prod - hi.astype(F32)).astype(BF16)
        s = jnp.dot(jnp.concatenate([hi, lo], axis=1), ones, preferred_element_type=F32)
        s_new = jnp.sum(ks_ref[0, pl.ds(g, 1), :] * q, axis=1, keepdims=True)
        m = jnp.maximum(jnp.max(s, axis=0, keepdims=True), s_new)
        p = jnp.exp(s - m)
        p_new = jnp.exp(s_new - m)
        l = jnp.sum(p, axis=0, keepdims=True) + p_new
        o = jnp.sum(p * vbuf[slot, h].reshape(n_sel * page, hd), axis=0, keepdims=True)
        o = o + p_new * vs_ref[0, pl.ds(g, 1), :]
        o_ref[0, pl.ds(h, 1), :] = (o / l).astype(o_ref.dtype)

    def head_group(g, carry):
        for j in range(KV_GROUP):
            for c in head_copies(b, g * KV_GROUP + j, slot):
                c.wait()
        for j in range(KV_GROUP):
            one_head(g * KV_GROUP + j)
        return carry

    lax.fori_loop(0, n_heads // KV_GROUP, head_group, 0)


def _sample_attn(q_s, k_s, v_s, cache_k4, cache_v4, page_table, sel):
    db, n_pages = page_table.shape
    page = cache_k4.shape[1]
    ppb = MOBA_BLOCK // page
    n_sel = MOBA_TOPK * ppb
    assert HEAD_DIM == LANES

    grid_spec = pltpu.PrefetchScalarGridSpec(
        num_scalar_prefetch=2,
        grid=(db,),
        in_specs=[
            pl.BlockSpec((1, N_HEADS, HEAD_DIM), lambda b, pt, sl: (b, 0, 0)),
            pl.BlockSpec((1, N_KV_HEADS, HEAD_DIM), lambda b, pt, sl: (b, 0, 0)),
            pl.BlockSpec((1, N_KV_HEADS, HEAD_DIM), lambda b, pt, sl: (b, 0, 0)),
            pl.BlockSpec(memory_space=pl.ANY),
            pl.BlockSpec(memory_space=pl.ANY),
        ],
        out_specs=pl.BlockSpec((1, N_HEADS, HEAD_DIM), lambda b, pt, sl: (b, 0, 0)),
        scratch_shapes=[
            pltpu.VMEM((2, N_HEADS, n_sel, page, HEAD_DIM), F32),
            pltpu.VMEM((2, N_HEADS, n_sel, page, HEAD_DIM), F32),
            pltpu.SemaphoreType.DMA((2, 2, N_HEADS)),
        ],
    )
    return pl.pallas_call(
        functools.partial(_sample_attn_kernel, pages_per_block=ppb),
        grid_spec=grid_spec,
        out_shape=jax.ShapeDtypeStruct((db, N_HEADS, HEAD_DIM), F32),
        compiler_params=_cparams(("arbitrary",), 40),
        name="sample_attn",
    )(page_table, sel, q_s, k_s, v_s, cache_k4, cache_v4)


def _rope_tables(pos):
    half = HEAD_DIM // 2
    inv = 1.0 / (ROPE_THETA ** (jnp.arange(half, dtype=F32) / half))
    ang = pos.astype(F32)[:, None] * inv[None, :]
    cos, sin = jnp.cos(ang), jnp.sin(ang)
    return jnp.concatenate([cos, cos], axis=-1), jnp.concatenate([-sin, sin], axis=-1)


def _trunk_tail(x, a, cb, proj, w_attn_br, w_conv_br, w_o, ln2, w_ff_gate, w_ff_up, w_ff_down, ln_f,
                ga_col, gc_col):
    merged = _merge(a, cb, proj, w_attn_br, w_conv_br, ga_col, gc_col)
    x1 = _outproj(merged, w_o, x)
    hmid = _ffn_up(x1, ln2, w_ff_gate, w_ff_up)
    return _ffn_down(hmid, w_ff_down, x1, ln_f)


def kernel(x_prompt, x_sample, cache_k, cache_v, state_conv, page_table, ln1, w_in, conv_w, w_attn_br,
           w_conv_br, w_o, ln2, w_ff_gate, w_ff_up, w_ff_down, ln_f):
    batch, seq, d = x_prompt.shape
    db, dec_seq, _ = x_sample.shape
    depth, n_phys, page, kvh, hd = cache_k.shape
    assert depth == 1 and dec_seq == 1 and kvh == N_KV_HEADS and hd == HEAD_DIM
    qw, kvw = N_HEADS * HEAD_DIM, N_KV_HEADS * HEAD_DIM
    cw = conv_w.shape[-1]
    k_col, v_col, b_col = qw, qw + kvw, qw + 2 * kvw
    ga_col = b_col + 3 * cw
    gc_col = ga_col + d
    rope_cols = qw + kvw
    past = page_table.shape[1] * page
    w_in_b = w_in[0].astype(BF16)
    trunk = (w_attn_br[0].astype(BF16), w_conv_br[0].astype(BF16), w_o[0].astype(BF16), ln2[0],
             w_ff_gate[0].astype(BF16), w_ff_up[0].astype(BF16), w_ff_down[0].astype(BF16), ln_f)

    mp = batch * seq
    xp = x_prompt.reshape(mp, d)
    cos_p, sin_p = _rope_tables(jnp.tile(jnp.arange(seq, dtype=jnp.int32), batch))
    proj_p = _inproj(xp, ln1[0], w_in_b, cos_p, sin_p, rope_cols)
    a_p = _attn_prompt(proj_p, batch, seq, k_col, v_col)
    zero_state = jnp.zeros((batch, SUBLANES, cw), F32)
    cb_p, utail_p = _conv_seq(proj_p, zero_state, conv_w[0], seq, b_col, cw)
    y_p = _trunk_tail(xp, a_p, cb_p, proj_p, *trunk, ga_col, gc_col)
    tiles_per_seq = utail_p.shape[0] // batch
    conv_p = utail_p.reshape(batch, tiles_per_seq, SUBLANES, cw)[:, -1, SUBLANES - (CONV_K - 1):, :]

    xs = x_sample.reshape(db, d)
    cos_s, sin_s = _rope_tables(jnp.full((db,), past, jnp.int32))
    proj_s = _inproj(xs, ln1[0], w_in_b, cos_s, sin_s, rope_cols)
    q_s = proj_s[:, :qw]
    k_s = proj_s[:, k_col:k_col + kvw]
    v_s = proj_s[:, v_col:v_col + kvw]
    cache_k3 = cache_k.reshape(n_phys, page * N_KV_HEADS, HEAD_DIM)
    cache_k4 = cache_k.reshape(n_phys, page, N_KV_HEADS, HEAD_DIM)
    cache_v4 = cache_v.reshape(n_phys, page, N_KV_HEADS, HEAD_DIM)
    sel = _sample_select(q_s.reshape(db, N_HEADS, HEAD_DIM), cache_k3, page_table)
    a_s = _sample_attn(q_s.reshape(db, N_HEADS, HEAD_DIM), k_s.reshape(db, N_KV_HEADS, HEAD_DIM),
                       v_s.reshape(db, N_KV_HEADS, HEAD_DIM), cache_k4, cache_v4, page_table,
                       sel.reshape(db, N_HEADS * MOBA_TOPK)).reshape(db, qw).astype(BF16)
    state = state_conv[0].astype(F32)
    cb_s, u_s = _conv_step(proj_s, state.reshape(db, (CONV_K - 1) * cw), conv_w[0], b_col, cw)
    y_s = _trunk_tail(xs, a_s, cb_s, proj_s, *trunk, ga_col, gc_col)
    conv_s = jnp.concatenate([state[:, 1:, :], u_s[:, None, :]], axis=1)

    return (
        y_p.reshape(batch, seq, d),
        y_s.reshape(db, 1, d),
        proj_p[:, k_col:k_col + kvw].reshape(1, batch, seq, N_KV_HEADS, HEAD_DIM),
        proj_p[:, v_col:v_col + kvw].reshape(1, batch, seq, N_KV_HEADS, HEAD_DIM),
        conv_p[None],
        k_s.reshape(1, db, 1, N_KV_HEADS, HEAD_DIM),
        v_s.reshape(1, db, 1, N_KV_HEADS, HEAD_DIM),
        conv_s[None],
    )
```

```python
import functools

import jax
import jax.numpy as jnp
from jax import lax
from jax.experimental import pallas as pl
from jax.experimental.pallas import tpu as pltpu

F32 = jnp.float32
BF16 = jnp.bfloat16

N_HEADS = 16
HEAD_DIM = 128
N_KV_HEADS = 4
KV_GROUP = N_HEADS // N_KV_HEADS
MOBA_BLOCK = 256
MOBA_TOPK = 3
ROPE_THETA = 10000.0
CONV_K = 3
RMS_EPS = 1e-6
ATTN_SCALE = HEAD_DIM ** -0.5
LOG2E = 1.4426950408889634
MASK_BIAS = -1e30

LANES = 128
SUBLANES = 8
MIB = 1 << 20
ROW_TILE = 1024
COL_TILE = 512
PAGES_PER_STEP = 32
ROW_GROUP = 256


def _cparams(semantics, vmem_mib):
    return pltpu.CompilerParams(dimension_semantics=semantics, vmem_limit_bytes=vmem_mib * MIB)


def _row_tile(m, cap=ROW_TILE):
    t = min(m, cap)
    assert m % t == 0 and t % SUBLANES == 0, (m, t)
    return t


def _rmsnorm_f32(x, g):
    return x * lax.rsqrt(jnp.mean(x * x, axis=-1, keepdims=True) + RMS_EPS) * g


def _project_tile(x_ref, g_ref, w_ref, cos_ref, sin_ref, o_ref, xn_ref, rope_tiles):
    j = pl.program_id(1)

    @pl.when(j == 0)
    def _():
        xn_ref[...] = _rmsnorm_f32(x_ref[...], g_ref[...]).astype(BF16)

    acc = jnp.dot(xn_ref[...], w_ref[...], preferred_element_type=F32)

    @pl.when(j < rope_tiles)
    def _():
        cos = cos_ref[...]
        sin = sin_ref[...]
        for c in range(acc.shape[1] // HEAD_DIM):
            a = acc[:, c * HEAD_DIM:(c + 1) * HEAD_DIM]
            o_ref[:, c * HEAD_DIM:(c + 1) * HEAD_DIM] = a * cos + pltpu.roll(a, HEAD_DIM // 2, axis=1) * sin

    @pl.when(j >= rope_tiles)
    def _():
        o_ref[...] = acc


def _inproj_kernel(x_ref, g_ref, w_ref, cos_ref, sin_ref, o_ref, xn_ref, *, rope_tiles):
    _project_tile(x_ref, g_ref, w_ref, cos_ref, sin_ref, o_ref, xn_ref, rope_tiles)


def _block_means(get_page, n_pages, page_rows):
    pages_per_block = MOBA_BLOCK * N_KV_HEADS // page_rows
    blocks = n_pages // pages_per_block
    row = lax.broadcasted_iota(jnp.int32, (blocks, HEAD_DIM), 0)
    means = [jnp.zeros((blocks, HEAD_DIM), F32) for _ in range(N_KV_HEADS)]
    for r in range(blocks):
        tot = jnp.zeros((SUBLANES, HEAD_DIM), F32)
        for t in range(pages_per_block):
            x = get_page(r * pages_per_block + t)
            tot = tot + jnp.sum(x.reshape(page_rows // SUBLANES, SUBLANES, HEAD_DIM), axis=0)
        head_sum = tot[0:N_KV_HEADS, :]
        for c in range(1, SUBLANES // N_KV_HEADS):
            head_sum = head_sum + tot[c * N_KV_HEADS:(c + 1) * N_KV_HEADS, :]
        head_mean = head_sum * (1.0 / MOBA_BLOCK)
        for g in range(N_KV_HEADS):
            means[g] = jnp.where(row == r, head_mean[g:g + 1, :], means[g])
    return means


def _inproj_stream_kernel(pt_ref, x_ref, g_ref, w_ref, cos_ref, sin_ref, kc_ref, o_ref, km_ref, xn_ref,
                          pbuf, psem, *, rope_tiles, chunks_per_seq, n_chunks):
    step = pl.program_id(0) * pl.num_programs(1) + pl.program_id(1)
    slot = step % 2
    pages = pbuf.shape[1]

    def chunk_copies(s, sl):
        b, c = s // chunks_per_seq, s % chunks_per_seq
        return [pltpu.make_async_copy(kc_ref.at[pt_ref[b, c * pages + t]], pbuf.at[sl, t], psem.at[sl])
                for t in range(pages)]

    @pl.when(step == 0)
    def _():
        for c in chunk_copies(step, slot):
            c.start()

    @pl.when(step + 1 < n_chunks)
    def _():
        for c in chunk_copies(step + 1, 1 - slot):
            c.start()

    _project_tile(x_ref, g_ref, w_ref, cos_ref, sin_ref, o_ref, xn_ref, rope_tiles)

    @pl.when(step < n_chunks)
    def _():
        for c in chunk_copies(step, slot):
            c.wait()
        means = _block_means(lambda t: pbuf[slot, t], pages, pbuf.shape[2])
        for g in range(N_KV_HEADS):
            km_ref[0, g] = means[g]


def _inproj(x, ln1, w_in, cos, sin, rope_cols, stream=None):
    m, d = x.shape
    n = w_in.shape[1]
    tm, tn = _row_tile(m), COL_TILE
    assert n % tn == 0 and rope_cols % tn == 0
    grid = (m // tm, n // tn)
    rope_tiles = rope_cols // tn
    args = (x, ln1.reshape(1, d), w_in, cos, sin)
    block_shapes = [(tm, d), (1, d), (d, tn), (tm, HEAD_DIM), (tm, HEAD_DIM)]
    assert cos.shape[0] % tm == 0 and m % cos.shape[0] == 0
    table_tiles = cos.shape[0] // tm
    block_maps = [lambda i, j: (i, 0), lambda i, j: (0, 0), lambda i, j: (0, j),
                  lambda i, j: (i % table_tiles, 0), lambda i, j: (i % table_tiles, 0)]
    if stream is None:
        return pl.pallas_call(
            functools.partial(_inproj_kernel, rope_tiles=rope_tiles),
            grid=grid,
            in_specs=[pl.BlockSpec(s, f) for s, f in zip(block_shapes, block_maps)],
            out_specs=pl.BlockSpec((tm, tn), lambda i, j: (i, j)),
            out_shape=jax.ShapeDtypeStruct((m, n), F32),
            scratch_shapes=[pltpu.VMEM((tm, d), BF16)],
            compiler_params=_cparams(("parallel", "arbitrary"), 48),
            name="inproj",
        )(*args)

    cache_k3, page_table = stream
    db, n_pages = page_table.shape
    _, page_rows, hd = cache_k3.shape
    page = page_rows // N_KV_HEADS
    pages = PAGES_PER_STEP
    assert MOBA_BLOCK % page == 0 and n_pages % pages == 0 and SUBLANES % N_KV_HEADS == 0
    blocks = pages * page // MOBA_BLOCK
    assert blocks % SUBLANES == 0 and hd == HEAD_DIM
    chunks_per_seq = n_pages // pages
    n_chunks = db * chunks_per_seq
    nb = n_pages * page // MOBA_BLOCK
    assert n_chunks <= grid[0] * grid[1], "not enough projection steps to stream the key cache"

    def km_map(i, j, pt):
        s = jnp.minimum(i * grid[1] + j, n_chunks - 1)
        return (s // chunks_per_seq, 0, s % chunks_per_seq, 0)

    grid_spec = pltpu.PrefetchScalarGridSpec(
        num_scalar_prefetch=1,
        grid=grid,
        in_specs=[pl.BlockSpec(s, functools.partial(lambda i, j, pt, f: f(i, j), f=f))
                  for s, f in zip(block_shapes, block_maps)] + [pl.BlockSpec(memory_space=pl.ANY)],
        out_specs=[
            pl.BlockSpec((tm, tn), lambda i, j, pt: (i, j)),
            pl.BlockSpec((1, N_KV_HEADS, blocks, HEAD_DIM), km_map),
        ],
        scratch_shapes=[
            pltpu.VMEM((tm, d), BF16),
            pltpu.VMEM((2, pages, page_rows, HEAD_DIM), F32),
            pltpu.SemaphoreType.DMA((2,)),
        ],
    )
    return pl.pallas_call(
        functools.partial(_inproj_stream_kernel, rope_tiles=rope_tiles, chunks_per_seq=chunks_per_seq,
                          n_chunks=n_chunks),
        grid_spec=grid_spec,
        out_shape=[jax.ShapeDtypeStruct((m, n), F32),
                   jax.ShapeDtypeStruct((db, N_KV_HEADS, nb, HEAD_DIM), F32)],
        compiler_params=_cparams(("arbitrary", "arbitrary"), 56),
        name="inproj_stream",
    )(page_table, *args, cache_k3)


def _attn_prompt_kernel(q_ref, k_ref, v_ref, o_ref, ka_ref, va_ref, kmean_ref, qa_ref, m_ref, acc_ref):
    qi = pl.program_id(2)
    seq = k_ref.shape[0]
    nb = seq // MOBA_BLOCK
    rows = KV_GROUP * MOBA_BLOCK
    pair = 2 * MOBA_BLOCK

    @pl.when(qi == 0)
    def _():
        k = k_ref[...]
        row_blk = lax.broadcasted_iota(jnp.int32, (seq, LANES), 0) // MOBA_BLOCK
        lane = lax.broadcasted_iota(jnp.int32, (seq, LANES), 1)
        ka_ref[:, 0:HEAD_DIM] = k.astype(BF16)
        ka_ref[:, HEAD_DIM:] = jnp.where(lane == row_blk, 1.0, 0.0).astype(BF16)
        va_ref[:, 0:HEAD_DIM] = v_ref[...].astype(BF16)
        va_ref[:, HEAD_DIM:] = jnp.ones((seq, LANES), BF16)
        kmean = jnp.mean(k.reshape(nb, MOBA_BLOCK, HEAD_DIM), axis=1)
        km_hi = kmean.astype(BF16)
        km_lo = (kmean - km_hi.astype(F32)).astype(BF16)
        kmean_ref[...] = jnp.zeros_like(kmean_ref)
        kmean_ref[0:nb, :] = jnp.concatenate([km_hi, km_hi, km_lo], axis=1)

    q = q_ref[...]
    q4 = jnp.concatenate([q[:, h * HEAD_DIM:(h + 1) * HEAD_DIM] for h in range(KV_GROUP)], axis=0)
    qs = (q4 * (ATTN_SCALE * LOG2E)).astype(BF16)

    q_hi = q4.astype(BF16)
    q_lo = (q4 - q_hi.astype(F32)).astype(BF16)
    gate = lax.dot_general(jnp.concatenate([q_hi, q_lo, q_hi], axis=1), kmean_ref[...],
                           (((1,), (1,)), ((), ())), preferred_element_type=F32)
    lane_i = lax.broadcasted_iota(jnp.int32, (rows, LANES), 1)
    lane = lane_i.astype(F32)
    past = lane_i < qi
    gate = jnp.where(past, gate, -jnp.inf)
    bias = jnp.full((rows, LANES), MASK_BIAS, F32)
    for _ in range(MOBA_TOPK):
        top = jnp.max(gate, axis=1, keepdims=True)
        pick = lane == jnp.min(jnp.where(gate == top, lane, float(LANES)), axis=1, keepdims=True)
        bias = jnp.where(pick & past, 0.0, bias)
        gate = jnp.where(pick, -jnp.inf, gate)
    qa = jnp.concatenate([qs, bias.astype(BF16)], axis=1)

    own = pl.multiple_of(qi * MOBA_BLOCK, MOBA_BLOCK)
    s = lax.dot_general(qs, ka_ref[pl.ds(own, MOBA_BLOCK), 0:HEAD_DIM], (((1,), (1,)), ((), ())),
                        preferred_element_type=F32)
    qrow = lax.broadcasted_iota(jnp.int32, (rows, MOBA_BLOCK), 0) & (MOBA_BLOCK - 1)
    kcol = lax.broadcasted_iota(jnp.int32, (rows, MOBA_BLOCK), 1)
    s = jnp.where(kcol <= qrow, s, MASK_BIAS)
    m0 = jnp.broadcast_to(jnp.max(s, axis=1, keepdims=True), (rows, LANES))
    p = jnp.concatenate([jnp.exp2(s[:, c * LANES:(c + 1) * LANES] - m0)
                         for c in range(MOBA_BLOCK // LANES)], axis=1)
    m_ref[...] = m0
    acc_ref[...] = jnp.dot(p.astype(BF16), va_ref[pl.ds(own, MOBA_BLOCK), :], preferred_element_type=F32)

    qa_ref[...] = qa

    def past_pair(t, carry):
        start = pl.multiple_of(t * pair, pair)
        kt = ka_ref[pl.ds(start, pair), :]
        vt = va_ref[pl.ds(start, pair), :]
        for r0 in range(0, rows, ROW_GROUP):
            rs = slice(r0, r0 + ROW_GROUP)
            s = lax.dot_general(qa_ref[rs, :], kt, (((1,), (1,)), ((), ())), preferred_element_type=F32)
            m_old = m_ref[rs, :]
            m_new = jnp.maximum(m_old, jnp.max(s, axis=1, keepdims=True))
            alpha = jnp.exp2(m_old - m_new)
            p = jnp.concatenate([jnp.exp2(s[:, c * LANES:(c + 1) * LANES] - m_new)
                                 for c in range(pair // LANES)], axis=1)
            pv = jnp.dot(p.astype(BF16), vt, preferred_element_type=F32)
            for c in range(2):
                cs = slice(c * LANES, (c + 1) * LANES)
                acc_ref[rs, cs] = alpha * acc_ref[rs, cs] + pv[:, cs]
            m_ref[rs, :] = m_new
        return carry

    lax.fori_loop(0, (qi + 1) // 2, past_pair, 0)

    o = acc_ref[:, 0:HEAD_DIM] / acc_ref[:, HEAD_DIM:]
    for h in range(KV_GROUP):
        o_ref[:, h * HEAD_DIM:(h + 1) * HEAD_DIM] = o[h * MOBA_BLOCK:(h + 1) * MOBA_BLOCK].astype(o_ref.dtype)


def _attn_prompt(proj, batch, seq, k_col, v_col):
    nq = seq // MOBA_BLOCK
    assert seq % (2 * MOBA_BLOCK) == 0 and MOBA_TOPK <= nq <= LANES
    gw = KV_GROUP * HEAD_DIM
    rows = KV_GROUP * MOBA_BLOCK
    kblk, vblk = k_col // HEAD_DIM, v_col // HEAD_DIM
    return pl.pallas_call(
        _attn_prompt_kernel,
        grid=(batch, N_KV_HEADS, nq),
        in_specs=[
            pl.BlockSpec((MOBA_BLOCK, gw), lambda b, g, i: (b * nq + i, g)),
            pl.BlockSpec((seq, HEAD_DIM), lambda b, g, i: (b, kblk + g)),
            pl.BlockSpec((seq, HEAD_DIM), lambda b, g, i: (b, vblk + g)),
        ],
        out_specs=pl.BlockSpec((MOBA_BLOCK, gw), lambda b, g, i: (b * nq + i, g)),
        out_shape=jax.ShapeDtypeStruct((batch * seq, N_HEADS * HEAD_DIM), BF16),
        scratch_shapes=[
            pltpu.VMEM((seq, HEAD_DIM + LANES), BF16),
            pltpu.VMEM((seq, HEAD_DIM + LANES), BF16),
            pltpu.VMEM((LANES, 3 * HEAD_DIM), BF16),
            pltpu.VMEM((rows, HEAD_DIM + LANES), BF16),
            pltpu.VMEM((rows, LANES), F32),
            pltpu.VMEM((rows, HEAD_DIM + LANES), F32),
        ],
        compiler_params=_cparams(("parallel", "parallel", "arbitrary"), 40),
        name="attn_prompt",
    )(proj, proj, proj)


def _conv_seq_kernel(b_ref, c_ref, h_ref, cp_ref, hp_ref, st_ref, w_ref, cb_ref, ut_ref, *, seq):
    i = pl.program_id(0)
    tm = c_ref.shape[0]
    u = c_ref[...] * h_ref[...]
    prev = jnp.where((i * tm) % seq == 0, st_ref[0], cp_ref[...] * hp_ref[...])
    p1 = prev[SUBLANES - 1:SUBLANES, :]
    p2 = prev[SUBLANES - 2:SUBLANES - 1, :]
    row = lax.broadcasted_iota(jnp.int32, u.shape, 0)
    u1 = jnp.where(row == 0, p1, pltpu.roll(u, 1, axis=0))
    u2 = jnp.where(row == 0, p2, jnp.where(row == 1, p1, pltpu.roll(u, 2, axis=0)))
    w = w_ref[...]
    conv = w[0:1, :] * u2 + w[1:2, :] * u1 + w[2:3, :] * u
    cb_ref[...] = (b_ref[...] * conv).astype(BF16)
    ut_ref[0] = u[tm - SUBLANES:tm, :]


def _conv_seq(proj, state8, conv_w, seq, b_col, cw):
    m = proj.shape[0]
    tm = _row_tile(seq, 512)
    nt = m // tm
    cb = b_col // cw
    pstep = tm // SUBLANES
    return pl.pallas_call(
        functools.partial(_conv_seq_kernel, seq=seq),
        grid=(nt,),
        in_specs=[
            pl.BlockSpec((tm, cw), lambda i: (i, cb)),
            pl.BlockSpec((tm, cw), lambda i: (i, cb + 1)),
            pl.BlockSpec((tm, cw), lambda i: (i, cb + 2)),
            pl.BlockSpec((SUBLANES, cw), lambda i: (jnp.maximum(i * pstep - 1, 0), cb + 1)),
            pl.BlockSpec((SUBLANES, cw), lambda i: (jnp.maximum(i * pstep - 1, 0), cb + 2)),
            pl.BlockSpec((1, SUBLANES, cw), lambda i: ((i * tm) // seq, 0, 0)),
            pl.BlockSpec((CONV_K, cw), lambda i: (0, 0)),
        ],
        out_specs=[
            pl.BlockSpec((tm, cw), lambda i: (i, 0)),
            pl.BlockSpec((1, SUBLANES, cw), lambda i: (i, 0, 0)),
        ],
        out_shape=[
            jax.ShapeDtypeStruct((m, cw), BF16),
            jax.ShapeDtypeStruct((nt, SUBLANES, cw), F32),
        ],
        compiler_params=_cparams(("parallel",), 32),
        name="conv_seq",
    )(proj, proj, proj, proj, proj, state8, conv_w)


def _conv_step_kernel(b_ref, c_ref, h_ref, st_ref, w_ref, cb_ref, u_ref):
    cw = c_ref.shape[1]
    u = c_ref[...] * h_ref[...]
    w = w_ref[...]
    conv = w[0:1, :] * st_ref[:, 0:cw] + w[1:2, :] * st_ref[:, cw:2 * cw] + w[2:3, :] * u
    cb_ref[...] = (b_ref[...] * conv).astype(BF16)
    u_ref[...] = u


def _conv_step(proj, state, conv_w, b_col, cw):
    m = proj.shape[0]
    cb = b_col // cw
    return pl.pallas_call(
        _conv_step_kernel,
        grid=(1,),
        in_specs=[
            pl.BlockSpec((m, cw), lambda i: (0, cb)),
            pl.BlockSpec((m, cw), lambda i: (0, cb + 1)),
            pl.BlockSpec((m, cw), lambda i: (0, cb + 2)),
            pl.BlockSpec((m, (CONV_K - 1) * cw), lambda i: (0, 0)),
            pl.BlockSpec((CONV_K, cw), lambda i: (0, 0)),
        ],
        out_specs=[pl.BlockSpec((m, cw), lambda i: (0, 0)), pl.BlockSpec((m, cw), lambda i: (0, 0))],
        out_shape=[jax.ShapeDtypeStruct((m, cw), BF16), jax.ShapeDtypeStruct((m, cw), F32)],
        compiler_params=_cparams(("arbitrary",), 32),
        name="conv_step",
    )(proj, proj, proj, state, conv_w)


def _merge_kernel(a_ref, cb_ref, ga_ref, gc_ref, wa_ref, wc_ref, o_ref):
    ya = jnp.dot(a_ref[...], wa_ref[...], preferred_element_type=F32)
    yc = jnp.dot(cb_ref[...], wc_ref[...], preferred_element_type=F32)
    o_ref[...] = (jax.nn.sigmoid(ga_ref[...]) * ya + jax.nn.sigmoid(gc_ref[...]) * yc).astype(o_ref.dtype)


def _merge(a, cb, proj, w_attn_br, w_conv_br, ga_col, gc_col):
    m, qw = a.shape
    cw = cb.shape[1]
    d = w_attn_br.shape[1]
    tm, tn = _row_tile(m), COL_TILE
    ga, gc = ga_col // tn, gc_col // tn
    return pl.pallas_call(
        _merge_kernel,
        grid=(m // tm, d // tn),
        in_specs=[
            pl.BlockSpec((tm, qw), lambda i, j: (i, 0)),
            pl.BlockSpec((tm, cw), lambda i, j: (i, 0)),
            pl.BlockSpec((tm, tn), lambda i, j: (i, ga + j)),
            pl.BlockSpec((tm, tn), lambda i, j: (i, gc + j)),
            pl.BlockSpec((qw, tn), lambda i, j: (0, j)),
            pl.BlockSpec((cw, tn), lambda i, j: (0, j)),
        ],
        out_specs=pl.BlockSpec((tm, tn), lambda i, j: (i, j)),
        out_shape=jax.ShapeDtypeStruct((m, d), BF16),
        compiler_params=_cparams(("parallel", "parallel"), 48),
        name="merge",
    )(a, cb, proj, proj, w_attn_br, w_conv_br)


def _outproj_kernel(m_ref, w_ref, x_ref, o_ref):
    o_ref[...] = x_ref[...] + jnp.dot(m_ref[...], w_ref[...], preferred_element_type=F32)


def _outproj(merged, w_o, x):
    m, d = merged.shape
    n = w_o.shape[1]
    tm, tn = _row_tile(m), COL_TILE
    return pl.pallas_call(
        _outproj_kernel,
        grid=(m // tm, n // tn),
        in_specs=[
            pl.BlockSpec((tm, d), lambda i, j: (i, 0)),
            pl.BlockSpec((d, tn), lambda i, j: (0, j)),
            pl.BlockSpec((tm, tn), lambda i, j: (i, j)),
        ],
        out_specs=pl.BlockSpec((tm, tn), lambda i, j: (i, j)),
        out_shape=jax.ShapeDtypeStruct((m, n), F32),
        compiler_params=_cparams(("parallel", "parallel"), 48),
        name="outproj",
    )(merged, w_o, x)


def _ffn_up_kernel(x_ref, g_ref, wg_ref, wu_ref, o_ref, hn_ref):
    @pl.when(pl.program_id(1) == 0)
    def _():
        hn_ref[...] = _rmsnorm_f32(x_ref[...], g_ref[...]).astype(BF16)

    hn = hn_ref[...]
    gate = jnp.dot(hn, wg_ref[...], preferred_element_type=F32)
    up = jnp.dot(hn, wu_ref[...], preferred_element_type=F32)
    o_ref[...] = (gate * jax.nn.sigmoid(gate) * up).astype(o_ref.dtype)


def _ffn_up(x, ln2, w_gate, w_up):
    m, d = x.shape
    f = w_gate.shape[1]
    tm, tn = _row_tile(m), COL_TILE
    assert f % tn == 0
    return pl.pallas_call(
        _ffn_up_kernel,
        grid=(m // tm, f // tn),
        in_specs=[
            pl.BlockSpec((tm, d), lambda i, j: (i, 0)),
            pl.BlockSpec((1, d), lambda i, j: (0, 0)),
            pl.BlockSpec((d, tn), lambda i, j: (0, j)),
            pl.BlockSpec((d, tn), lambda i, j: (0, j)),
        ],
        out_specs=pl.BlockSpec((tm, tn), lambda i, j: (i, j)),
        out_shape=jax.ShapeDtypeStruct((m, f), BF16),
        scratch_shapes=[pltpu.VMEM((tm, d), BF16)],
        compiler_params=_cparams(("parallel", "arbitrary"), 48),
        name="ffn_up",
    )(x, ln2.reshape(1, d), w_gate, w_up)


def _ffn_down_kernel(h_ref, w_ref, x_ref, g_ref, o_ref):
    k = pl.program_id(1)

    @pl.when(k == 0)
    def _():
        o_ref[...] = x_ref[...]

    o_ref[...] += jnp.dot(h_ref[...], w_ref[...], preferred_element_type=F32)

    @pl.when(k == pl.num_programs(1) - 1)
    def _():
        o_ref[...] = _rmsnorm_f32(o_ref[...], g_ref[...])


def _ffn_down(hmid, w_down, x, ln_f):
    m, f = hmid.shape
    d = w_down.shape[1]
    tm, tk = _row_tile(m), COL_TILE
    return pl.pallas_call(
        _ffn_down_kernel,
        grid=(m // tm, f // tk),
        in_specs=[
            pl.BlockSpec((tm, tk), lambda i, k: (i, k)),
            pl.BlockSpec((tk, d), lambda i, k: (k, 0)),
            pl.BlockSpec((tm, d), lambda i, k: (i, 0)),
            pl.BlockSpec((1, d), lambda i, k: (0, 0)),
        ],
        out_specs=pl.BlockSpec((tm, d), lambda i, k: (i, 0)),
        out_shape=jax.ShapeDtypeStruct((m, d), F32),
        compiler_params=_cparams(("parallel", "arbitrary"), 48),
        name="ffn_down",
    )(hmid, w_down, x, ln_f.reshape(1, d))


def _sample_gate_kernel(q_ref, km_ref, sel_ref):
    q = q_ref[0]
    nb = km_ref.shape[2]
    head_group = lax.broadcasted_iota(jnp.int32, (N_HEADS, nb), 0) // KV_GROUP
    gate = jnp.zeros((N_HEADS, nb), F32)
    for g in range(N_KV_HEADS):
        gg = lax.dot_general(q, km_ref[0, g], (((1,), (1,)), ((), ())),
                             precision=lax.Precision.HIGHEST, preferred_element_type=F32)
        gate = jnp.where(head_group == g, gg, gate)
    blk = lax.broadcasted_iota(jnp.int32, (N_HEADS, nb), 1).astype(F32)
    lane = lax.broadcasted_iota(jnp.int32, (N_HEADS, LANES), 1)
    out = jnp.zeros((N_HEADS, LANES), F32)
    for r in range(MOBA_TOPK):
        top = jnp.max(gate, axis=1, keepdims=True)
        first = jnp.min(jnp.where(gate == top, blk, float(nb)), axis=1, keepdims=True)
        out = jnp.where(lane == r, first, out)
        gate = jnp.where(blk == first, -jnp.inf, gate)
    sel_ref[0] = out.astype(jnp.int32)


def _sample_gate(q_s, kmean):
    db, _, nb, _ = kmean.shape
    assert MOBA_TOPK <= nb
    sel = pl.pallas_call(
        _sample_gate_kernel,
        grid=(db,),
        in_specs=[pl.BlockSpec((1, N_HEADS, HEAD_DIM), lambda b: (b, 0, 0)),
                  pl.BlockSpec((1, N_KV_HEADS, nb, HEAD_DIM), lambda b: (b, 0, 0, 0))],
        out_specs=pl.BlockSpec((1, N_HEADS, LANES), lambda b: (b, 0, 0)),
        out_shape=jax.ShapeDtypeStruct((db, N_HEADS, LANES), jnp.int32),
        compiler_params=_cparams(("parallel",), 32),
        name="sample_gate",
    )(q_s, kmean)
    return sel[:, :, :MOBA_TOPK]


def _sample_attn_kernel(pt_ref, sel_ref, q_ref, ks_ref, vs_ref, kc_ref, vc_ref, o_ref, kbuf, vbuf, sem, *,
                        pages_per_block):
    b = pl.program_id(0)
    slot = b % 2
    _, n_heads, n_sel, page, hd = kbuf.shape

    def head_copies(bb, h, sl):
        g = h // KV_GROUP
        out = []
        for t in range(n_sel):
            blk = sel_ref[bb, h * MOBA_TOPK + t // pages_per_block]
            pg = pt_ref[bb, blk * pages_per_block + t % pages_per_block]
            out.append(pltpu.make_async_copy(kc_ref.at[pg, :, g, :], kbuf.at[sl, h, t], sem.at[0, sl, h]))
            out.append(pltpu.make_async_copy(vc_ref.at[pg, :, g, :], vbuf.at[sl, h, t], sem.at[1, sl, h]))
        return out

    def start_all(bb, sl):
        def start_head(h, carry):
            for c in head_copies(bb, h, sl):
                c.start()
            return carry
        lax.fori_loop(0, n_heads, start_head, 0)

    @pl.when(b == 0)
    def _():
        start_all(b, slot)

    @pl.when(b + 1 < pl.num_programs(0))
    def _():
        start_all(b + 1, 1 - slot)

    ones = jnp.ones((2 * hd, LANES), BF16)

    def one_head(h):
        g = h // KV_GROUP
        q = q_ref[0, pl.ds(h, 1), :] * ATTN_SCALE
        prod = kbuf[slot, h].reshape(n_sel * page, hd) * q
        hi = prod.astype(BF16)
        lo = (prod - hi.astype(F32)).astype(BF16)
        s = jnp.dot(jnp.concatenate([hi, lo], axis=1), ones, preferred_element_type=F32)
        s_new = jnp.sum(ks_ref[0, pl.ds(g, 1), :] * q, axis=1, keepdims=True)
        m = jnp.maximum(jnp.max(s, axis=0, keepdims=True), s_new)
        p = jnp.exp(s - m)
        p_new = jnp.exp(s_new - m)
        l = jnp.sum(p, axis=0, keepdims=True) + p_new
        o = jnp.sum(p * vbuf[slot, h].reshape(n_sel * page, hd), axis=0, keepdims=True)
        o = o + p_new * vs_ref[0, pl.ds(g, 1), :]
        o_ref[0, pl.ds(h, 1), :] = (o / l).astype(o_ref.dtype)

    def head_group(g, carry):
        for j in range(KV_GROUP):
            for c in head_copies(b, g * KV_GROUP + j, slot):
                c.wait()
        for j in range(KV_GROUP):
            one_head(g * KV_GROUP + j)
        return carry

    lax.fori_loop(0, n_heads // KV_GROUP, head_group, 0)


def _sample_attn(q_s, k_s, v_s, cache_k4, cache_v4, page_table, sel):
    db, n_pages = page_table.shape
    page = cache_k4.shape[1]
    ppb = MOBA_BLOCK // page
    n_sel = MOBA_TOPK * ppb
    assert HEAD_DIM == LANES

    grid_spec = pltpu.PrefetchScalarGridSpec(
        num_scalar_prefetch=2,
        grid=(db,),
        in_specs=[
            pl.BlockSpec((1, N_HEADS, HEAD_DIM), lambda b, pt, sl: (b, 0, 0)),
            pl.BlockSpec((1, N_KV_HEADS, HEAD_DIM), lambda b, pt, sl: (b, 0, 0)),
            pl.BlockSpec((1, N_KV_HEADS, HEAD_DIM), lambda b, pt, sl: (b, 0, 0)),
            pl.BlockSpec(memory_space=pl.ANY),
            pl.BlockSpec(memory_space=pl.ANY),
        ],
        out_specs=pl.BlockSpec((1, N_HEADS, HEAD_DIM), lambda b, pt, sl: (b, 0, 0)),
        scratch_shapes=[
            pltpu.VMEM((2, N_HEADS, n_sel, page, HEAD_DIM), F32),
            pltpu.VMEM((2, N_HEADS, n_sel, page, HEAD_DIM), F32),
            pltpu.SemaphoreType.DMA((2, 2, N_HEADS)),
        ],
    )
    return pl.pallas_call(
        functools.partial(_sample_attn_kernel, pages_per_block=ppb),
        grid_spec=grid_spec,
        out_shape=jax.ShapeDtypeStruct((db, N_HEADS, HEAD_DIM), F32),
        compiler_params=_cparams(("arbitrary",), 40),
        name="sample_attn",
    )(page_table, sel, q_s, k_s, v_s, cache_k4, cache_v4)


def _rope_tables(pos):
    half = HEAD_DIM // 2
    inv = 1.0 / (ROPE_THETA ** (jnp.arange(half, dtype=F32) / half))
    ang = pos.astype(F32)[:, None] * inv[None, :]
    cos, sin = jnp.cos(ang), jnp.sin(ang)
    return jnp.concatenate([cos, cos], axis=-1), jnp.concatenate([-sin, sin], axis=-1)


def _trunk_tail(x, a, cb, proj, w_attn_br, w_conv_br, w_o, ln2, w_ff_gate, w_ff_up, w_ff_down, ln_f,
                ga_col, gc_col):
    merged = _merge(a, cb, proj, w_attn_br, w_conv_br, ga_col, gc_col)
    x1 = _outproj(merged, w_o, x)
    hmid = _ffn_up(x1, ln2, w_ff_gate, w_ff_up)
    return _ffn_down(hmid, w_ff_down, x1, ln_f)


def kernel(x_prompt, x_sample, cache_k, cache_v, state_conv, page_table, ln1, w_in, conv_w, w_attn_br,
           w_conv_br, w_o, ln2, w_ff_gate, w_ff_up, w_ff_down, ln_f):
    batch, seq, d = x_prompt.shape
    db, dec_seq, _ = x_sample.shape
    depth, n_phys, page, kvh, hd = cache_k.shape
    assert depth == 1 and dec_seq == 1 and kvh == N_KV_HEADS and hd == HEAD_DIM
    qw, kvw = N_HEADS * HEAD_DIM, N_KV_HEADS * HEAD_DIM
    cw = conv_w.shape[-1]
    k_col, v_col, b_col = qw, qw + kvw, qw + 2 * kvw
    ga_col = b_col + 3 * cw
    gc_col = ga_col + d
    rope_cols = qw + kvw
    past = page_table.shape[1] * page
    w_in_b = w_in[0].astype(BF16)
    trunk = (w_attn_br[0].astype(BF16), w_conv_br[0].astype(BF16), w_o[0].astype(BF16), ln2[0],
             w_ff_gate[0].astype(BF16), w_ff_up[0].astype(BF16), w_ff_down[0].astype(BF16), ln_f)

    mp = batch * seq
    xp = x_prompt.reshape(mp, d)
    cos_p, sin_p = _rope_tables(jnp.arange(seq, dtype=jnp.int32))
    cache_k3 = cache_k.reshape(n_phys, page * N_KV_HEADS, HEAD_DIM)
    cache_k4 = cache_k.reshape(n_phys, page, N_KV_HEADS, HEAD_DIM)
    cache_v4 = cache_v.reshape(n_phys, page, N_KV_HEADS, HEAD_DIM)
    proj_p, kmean_s = _inproj(xp, ln1[0], w_in_b, cos_p, sin_p, rope_cols, stream=(cache_k3, page_table))
    a_p = _attn_prompt(proj_p, batch, seq, k_col, v_col)
    zero_state = jnp.zeros((batch, SUBLANES, cw), F32)
    cb_p, utail_p = _conv_seq(proj_p, zero_state, conv_w[0], seq, b_col, cw)
    y_p = _trunk_tail(xp, a_p, cb_p, proj_p, *trunk, ga_col, gc_col)
    tiles_per_seq = utail_p.shape[0] // batch
    conv_p = utail_p.reshape(batch, tiles_per_seq, SUBLANES, cw)[:, -1, SUBLANES - (CONV_K - 1):, :]

    xs = x_sample.reshape(db, d)
    cos_s, sin_s = _rope_tables(jnp.full((db,), past, jnp.int32))
    proj_s = _inproj(xs, ln1[0], w_in_b, cos_s, sin_s, rope_cols)
    q_s = proj_s[:, :qw]
    k_s = proj_s[:, k_col:k_col + kvw]
    v_s = proj_s[:, v_col:v_col + kvw]
    sel = _sample_gate(q_s.reshape(db, N_HEADS, HEAD_DIM), kmean_s)
    a_s = _sample_attn(q_s.reshape(db, N_HEADS, HEAD_DIM), k_s.reshape(db, N_KV_HEADS, HEAD_DIM),
                       v_s.reshape(db, N_KV_HEADS, HEAD_DIM), cache_k4, cache_v4, page_table,
                       sel.reshape(db, N_HEADS * MOBA_TOPK)).reshape(db, qw).astype(BF16)
    state = state_conv[0].astype(F32)
    cb_s, u_s = _conv_step(proj_s, state.reshape(db, (CONV_K - 1) * cw), conv_w[0], b_col, cw)
    y_s = _trunk_tail(xs, a_s, cb_s, proj_s, *trunk, ga_col, gc_col)
    conv_s = jnp.concatenate([state[:, 1:, :], u_s[:, None, :]], axis=1)

    return (
        y_p.reshape(batch, seq, d),
        y_s.reshape(db, 1, d),
        proj_p[:, k_col:k_col + kvw].reshape(1, batch, seq, N_KV_HEADS, HEAD_DIM),
        proj_p[:, v_col:v_col + kvw].reshape(1, batch, seq, N_KV_HEADS, HEAD_DIM),
        conv_p[None],
        k_s.reshape(1, db, 1, N_KV_HEADS, HEAD_DIM),
        v_s.reshape(1, db, 1, N_KV_HEADS, HEAD_DIM),
        conv_s[None],
    )
```

```python
import functools

import jax
import jax.numpy as jnp
from jax import lax
from jax.experimental import pallas as pl
from jax.experimental.pallas import tpu as pltpu

F32 = jnp.float32
BF16 = jnp.bfloat16

N_HEADS = 16
HEAD_DIM = 128
N_KV_HEADS = 4
KV_GROUP = N_HEADS // N_KV_HEADS
MOBA_BLOCK = 256
MOBA_TOPK = 3
ROPE_THETA = 10000.0
CONV_K = 3
RMS_EPS = 1e-6
ATTN_SCALE = HEAD_DIM ** -0.5
LOG2E = 1.4426950408889634
MASK_BIAS = -1e30

LANES = 128
SUBLANES = 8
MIB = 1 << 20
ROW_TILE = 1024
COL_TILE = 512
PAGES_PER_STEP = 32
ROW_GROUP = 256


def _cparams(semantics, vmem_mib):
    return pltpu.CompilerParams(dimension_semantics=semantics, vmem_limit_bytes=vmem_mib * MIB)


def _row_tile(m, cap=ROW_TILE):
    t = min(m, cap)
    assert m % t == 0 and t % SUBLANES == 0, (m, t)
    return t


def _rmsnorm_f32(x, g):
    return x * lax.rsqrt(jnp.mean(x * x, axis=-1, keepdims=True) + RMS_EPS) * g


def _inproj_kernel(x_ref, g_ref, w_ref, cos_ref, sin_ref, o_ref, xn_ref, *, rope_tiles):
    j = pl.program_id(1)

    @pl.when(j == 0)
    def _():
        xn_ref[...] = _rmsnorm_f32(x_ref[...], g_ref[...]).astype(BF16)

    acc = jnp.dot(xn_ref[...], w_ref[...], preferred_element_type=F32)

    @pl.when(j < rope_tiles)
    def _():
        cos = cos_ref[...]
        sin = sin_ref[...]
        for c in range(acc.shape[1] // HEAD_DIM):
            a = acc[:, c * HEAD_DIM:(c + 1) * HEAD_DIM]
            o_ref[:, c * HEAD_DIM:(c + 1) * HEAD_DIM] = a * cos + pltpu.roll(a, HEAD_DIM // 2, axis=1) * sin

    @pl.when(j >= rope_tiles)
    def _():
        o_ref[...] = acc


def _block_means(get_page, n_pages, page_rows):
    pages_per_block = MOBA_BLOCK * N_KV_HEADS // page_rows
    blocks = n_pages // pages_per_block
    row = lax.broadcasted_iota(jnp.int32, (blocks, HEAD_DIM), 0)
    means = [jnp.zeros((blocks, HEAD_DIM), F32) for _ in range(N_KV_HEADS)]
    for r in range(blocks):
        tot = jnp.zeros((SUBLANES, HEAD_DIM), F32)
        for t in range(pages_per_block):
            x = get_page(r * pages_per_block + t)
            tot = tot + jnp.sum(x.reshape(page_rows // SUBLANES, SUBLANES, HEAD_DIM), axis=0)
        head_sum = tot[0:N_KV_HEADS, :]
        for c in range(1, SUBLANES // N_KV_HEADS):
            head_sum = head_sum + tot[c * N_KV_HEADS:(c + 1) * N_KV_HEADS, :]
        head_mean = head_sum * (1.0 / MOBA_BLOCK)
        for g in range(N_KV_HEADS):
            means[g] = jnp.where(row == r, head_mean[g:g + 1, :], means[g])
    return means


class _KeyStream:
    def __init__(self, pt_ref, kc_ref, km_ref, pbuf, psem, step, chunks_per_seq, n_chunks, every_step):
        self.pt_ref, self.kc_ref, self.km_ref, self.pbuf, self.psem = pt_ref, kc_ref, km_ref, pbuf, psem
        self.step, self.chunks_per_seq, self.n_chunks = step, chunks_per_seq, n_chunks
        self.every_step = every_step
        self.slot = step % 2
        self.pages = pbuf.shape[1]

    def _copies(self, s, sl):
        b, c = s // self.chunks_per_seq, s % self.chunks_per_seq
        return [pltpu.make_async_copy(self.kc_ref.at[self.pt_ref[b, c * self.pages + t]],
                                      self.pbuf.at[sl, t], self.psem.at[sl])
                for t in range(self.pages)]

    def prefetch(self):
        @pl.when(self.step == 0)
        def _():
            for c in self._copies(self.step, self.slot):
                c.start()

        @pl.when(self.step + 1 < self.n_chunks)
        def _():
            for c in self._copies(self.step + 1, 1 - self.slot):
                c.start()

    def reduce(self):
        def body():
            for c in self._copies(self.step, self.slot):
                c.wait()
            means = _block_means(lambda t: self.pbuf[self.slot, t], self.pages, self.pbuf.shape[2])
            for g in range(N_KV_HEADS):
                self.km_ref[0, g] = means[g]

        if self.every_step:
            body()
        else:
            pl.when(self.step < self.n_chunks)(body)


def _key_stream_specs(cache_k3, page_table, n_steps, step_of):
    db, n_pages = page_table.shape
    _, page_rows, hd = cache_k3.shape
    page = page_rows // N_KV_HEADS
    pages = PAGES_PER_STEP
    assert MOBA_BLOCK % page == 0 and n_pages % pages == 0 and SUBLANES % N_KV_HEADS == 0
    blocks = pages * page // MOBA_BLOCK
    assert blocks % SUBLANES == 0 and hd == HEAD_DIM
    chunks_per_seq = n_pages // pages
    n_chunks = db * chunks_per_seq
    nb = n_pages * page // MOBA_BLOCK
    assert n_chunks <= n_steps, "not enough host grid steps to stream the key cache"

    def km_map(*ids_and_pt):
        s = jnp.minimum(step_of(*ids_and_pt[:-1]), n_chunks - 1)
        return (s // chunks_per_seq, 0, s % chunks_per_seq, 0)

    return (dict(chunks_per_seq=chunks_per_seq, n_chunks=n_chunks, every_step=n_chunks == n_steps),
            pl.BlockSpec((1, N_KV_HEADS, blocks, HEAD_DIM), km_map),
            jax.ShapeDtypeStruct((db, N_KV_HEADS, nb, HEAD_DIM), F32),
            [pltpu.VMEM((2, pages, page_rows, HEAD_DIM), F32), pltpu.SemaphoreType.DMA((2,))])


def _inproj(x, ln1, w_in, cos, sin, rope_cols):
    m, d = x.shape
    n = w_in.shape[1]
    tm, tn = _row_tile(m), COL_TILE
    assert n % tn == 0 and rope_cols % tn == 0
    assert cos.shape[0] % tm == 0 and m % cos.shape[0] == 0
    table_tiles = cos.shape[0] // tm
    return pl.pallas_call(
        functools.partial(_inproj_kernel, rope_tiles=rope_cols // tn),
        grid=(m // tm, n // tn),
        in_specs=[
            pl.BlockSpec((tm, d), lambda i, j: (i, 0)),
            pl.BlockSpec((1, d), lambda i, j: (0, 0)),
            pl.BlockSpec((d, tn), lambda i, j: (0, j)),
            pl.BlockSpec((tm, HEAD_DIM), lambda i, j: (i % table_tiles, 0)),
            pl.BlockSpec((tm, HEAD_DIM), lambda i, j: (i % table_tiles, 0)),
        ],
        out_specs=pl.BlockSpec((tm, tn), lambda i, j: (i, j)),
        out_shape=jax.ShapeDtypeStruct((m, n), F32),
        scratch_shapes=[pltpu.VMEM((tm, d), BF16)],
        compiler_params=_cparams(("parallel", "arbitrary"), 48),
        name="inproj",
    )(x, ln1.reshape(1, d), w_in, cos, sin)


def _attn_prompt_kernel(pt_ref, q_ref, k_ref, v_ref, kc_ref, o_ref, km_ref, ka_ref, va_ref, kmean_ref, qa_ref,
                        m_ref, acc_ref, pbuf, psem, *, chunks_per_seq, n_chunks, every_step):
    qi = pl.program_id(2)
    seq = k_ref.shape[0]
    nb = seq // MOBA_BLOCK
    rows = KV_GROUP * MOBA_BLOCK
    pair = 2 * MOBA_BLOCK
    step = (pl.program_id(0) * pl.num_programs(1) + pl.program_id(1)) * pl.num_programs(2) + qi
    stream = _KeyStream(pt_ref, kc_ref, km_ref, pbuf, psem, step, chunks_per_seq, n_chunks, every_step)
    stream.prefetch()

    @pl.when(qi == 0)
    def _():
        k = k_ref[...]
        row_blk = lax.broadcasted_iota(jnp.int32, (seq, LANES), 0) // MOBA_BLOCK
        lane = lax.broadcasted_iota(jnp.int32, (seq, LANES), 1)
        ka_ref[:, 0:HEAD_DIM] = k.astype(BF16)
        ka_ref[:, HEAD_DIM:] = jnp.where(lane == row_blk, 1.0, 0.0).astype(BF16)
        va_ref[:, 0:HEAD_DIM] = v_ref[...].astype(BF16)
        va_ref[:, HEAD_DIM:] = jnp.ones((seq, LANES), BF16)
        kmean = jnp.mean(k.reshape(nb, MOBA_BLOCK, HEAD_DIM), axis=1)
        km_hi = kmean.astype(BF16)
        km_lo = (kmean - km_hi.astype(F32)).astype(BF16)
        kmean_ref[...] = jnp.zeros_like(kmean_ref)
        kmean_ref[0:nb, :] = jnp.concatenate([km_hi, km_hi, km_lo], axis=1)

    stream.reduce()

    q = q_ref[...]
    q4 = jnp.concatenate([q[:, h * HEAD_DIM:(h + 1) * HEAD_DIM] for h in range(KV_GROUP)], axis=0)
    qs = (q4 * (ATTN_SCALE * LOG2E)).astype(BF16)

    q_hi = q4.astype(BF16)
    q_lo = (q4 - q_hi.astype(F32)).astype(BF16)
    gate = lax.dot_general(jnp.concatenate([q_hi, q_lo, q_hi], axis=1), kmean_ref[...],
                           (((1,), (1,)), ((), ())), preferred_element_type=F32)
    lane_i = lax.broadcasted_iota(jnp.int32, (rows, LANES), 1)
    lane = lane_i.astype(F32)
    past = lane_i < qi
    gate = jnp.where(past, gate, -jnp.inf)
    bias = jnp.full((rows, LANES), MASK_BIAS, F32)
    for _ in range(MOBA_TOPK):
        top = jnp.max(gate, axis=1, keepdims=True)
        pick = lane == jnp.min(jnp.where(gate == top, lane, float(LANES)), axis=1, keepdims=True)
        bias = jnp.where(pick & past, 0.0, bias)
        gate = jnp.where(pick, -jnp.inf, gate)
    qa = jnp.concatenate([qs, bias.astype(BF16)], axis=1)

    own = pl.multiple_of(qi * MOBA_BLOCK, MOBA_BLOCK)
    s = lax.dot_general(qs, ka_ref[pl.ds(own, MOBA_BLOCK), 0:HEAD_DIM], (((1,), (1,)), ((), ())),
                        preferred_element_type=F32)
    qrow = lax.broadcasted_iota(jnp.int32, (rows, MOBA_BLOCK), 0) & (MOBA_BLOCK - 1)
    kcol = lax.broadcasted_iota(jnp.int32, (rows, MOBA_BLOCK), 1)
    s = jnp.where(kcol <= qrow, s, MASK_BIAS)
    m0 = jnp.broadcast_to(jnp.max(s, axis=1, keepdims=True), (rows, LANES))
    p = jnp.concatenate([jnp.exp2(s[:, c * LANES:(c + 1) * LANES] - m0)
                         for c in range(MOBA_BLOCK // LANES)], axis=1)
    m_ref[...] = m0
    acc_ref[...] = jnp.dot(p.astype(BF16), va_ref[pl.ds(own, MOBA_BLOCK), :], preferred_element_type=F32)

    qa_ref[...] = qa

    def past_pair(t, carry):
        start = pl.multiple_of(t * pair, pair)
        kt = ka_ref[pl.ds(start, pair), :]
        vt = va_ref[pl.ds(start, pair), :]
        groups = [slice(r0, r0 + ROW_GROUP) for r0 in range(0, rows, ROW_GROUP)]

        def scores(rs):
            return lax.dot_general(qa_ref[rs, :], kt, (((1,), (1,)), ((), ())), preferred_element_type=F32)

        m_olds = [m_ref[rs, :] for rs in groups]
        s_next = scores(groups[0])
        updates = []
        for gi, rs in enumerate(groups):
            s = s_next
            if gi + 1 < len(groups):
                s_next = scores(groups[gi + 1])
            m_old = m_olds[gi]
            m_new = jnp.maximum(m_old, jnp.max(s, axis=1, keepdims=True))
            alpha = jnp.exp2(m_old - m_new)
            p = jnp.concatenate([jnp.exp2(s[:, c * LANES:(c + 1) * LANES] - m_new)
                                 for c in range(pair // LANES)], axis=1)
            pv = jnp.dot(p.astype(BF16), vt, preferred_element_type=F32)
            updates.append((rs, m_new, alpha, pv))
        for rs, m_new, alpha, pv in updates:
            for c in range(2):
                cs = slice(c * LANES, (c + 1) * LANES)
                acc_ref[rs, cs] = alpha * acc_ref[rs, cs] + pv[:, cs]
            m_ref[rs, :] = m_new
        return carry

    lax.fori_loop(0, (qi + 1) // 2, past_pair, 0)

    o = acc_ref[:, 0:HEAD_DIM] / acc_ref[:, HEAD_DIM:]
    for h in range(KV_GROUP):
        o_ref[:, h * HEAD_DIM:(h + 1) * HEAD_DIM] = o[h * MOBA_BLOCK:(h + 1) * MOBA_BLOCK].astype(o_ref.dtype)


def _attn_prompt(proj, batch, seq, k_col, v_col, cache_k3, page_table):
    nq = seq // MOBA_BLOCK
    assert seq % (2 * MOBA_BLOCK) == 0 and MOBA_TOPK <= nq <= LANES
    gw = KV_GROUP * HEAD_DIM
    rows = KV_GROUP * MOBA_BLOCK
    kblk, vblk = k_col // HEAD_DIM, v_col // HEAD_DIM
    grid = (batch, N_KV_HEADS, nq)
    stream_kwargs, km_spec, km_shape, stream_scratch = _key_stream_specs(
        cache_k3, page_table, batch * N_KV_HEADS * nq, lambda b, g, i: (b * N_KV_HEADS + g) * nq + i)
    grid_spec = pltpu.PrefetchScalarGridSpec(
        num_scalar_prefetch=1,
        grid=grid,
        in_specs=[
            pl.BlockSpec((MOBA_BLOCK, gw), lambda b, g, i, pt: (b * nq + i, g)),
            pl.BlockSpec((seq, HEAD_DIM), lambda b, g, i, pt: (b, kblk + g)),
            pl.BlockSpec((seq, HEAD_DIM), lambda b, g, i, pt: (b, vblk + g)),
            pl.BlockSpec(memory_space=pl.ANY),
        ],
        out_specs=[pl.BlockSpec((MOBA_BLOCK, gw), lambda b, g, i, pt: (b * nq + i, g)), km_spec],
        scratch_shapes=[
            pltpu.VMEM((seq, HEAD_DIM + LANES), BF16),
            pltpu.VMEM((seq, HEAD_DIM + LANES), BF16),
            pltpu.VMEM((LANES, 3 * HEAD_DIM), BF16),
            pltpu.VMEM((rows, HEAD_DIM + LANES), BF16),
            pltpu.VMEM((rows, LANES), F32),
            pltpu.VMEM((rows, HEAD_DIM + LANES), F32),
        ] + stream_scratch,
    )
    return pl.pallas_call(
        functools.partial(_attn_prompt_kernel, **stream_kwargs),
        grid_spec=grid_spec,
        out_shape=[jax.ShapeDtypeStruct((batch * seq, N_HEADS * HEAD_DIM), BF16), km_shape],
        compiler_params=_cparams(("arbitrary", "arbitrary", "arbitrary"), 48),
        name="attn_prompt",
    )(page_table, proj, proj, proj, cache_k3)


def _conv_seq_kernel(b_ref, c_ref, h_ref, cp_ref, hp_ref, st_ref, w_ref, cb_ref, ut_ref, *, seq):
    i = pl.program_id(0)
    tm = c_ref.shape[0]
    u = c_ref[...] * h_ref[...]
    prev = jnp.where((i * tm) % seq == 0, st_ref[0], cp_ref[...] * hp_ref[...])
    p1 = prev[SUBLANES - 1:SUBLANES, :]
    p2 = prev[SUBLANES - 2:SUBLANES - 1, :]
    row = lax.broadcasted_iota(jnp.int32, u.shape, 0)
    u1 = jnp.where(row == 0, p1, pltpu.roll(u, 1, axis=0))
    u2 = jnp.where(row == 0, p2, jnp.where(row == 1, p1, pltpu.roll(u, 2, axis=0)))
    w = w_ref[...]
    conv = w[0:1, :] * u2 + w[1:2, :] * u1 + w[2:3, :] * u
    cb_ref[...] = (b_ref[...] * conv).astype(BF16)
    ut_ref[0] = u[tm - SUBLANES:tm, :]


def _conv_seq(proj, state8, conv_w, seq, b_col, cw):
    m = proj.shape[0]
    tm = _row_tile(seq, 512)
    nt = m // tm
    cb = b_col // cw
    pstep = tm // SUBLANES
    return pl.pallas_call(
        functools.partial(_conv_seq_kernel, seq=seq),
        grid=(nt,),
        in_specs=[
            pl.BlockSpec((tm, cw), lambda i: (i, cb)),
            pl.BlockSpec((tm, cw), lambda i: (i, cb + 1)),
            pl.BlockSpec((tm, cw), lambda i: (i, cb + 2)),
            pl.BlockSpec((SUBLANES, cw), lambda i: (jnp.maximum(i * pstep - 1, 0), cb + 1)),
            pl.BlockSpec((SUBLANES, cw), lambda i: (jnp.maximum(i * pstep - 1, 0), cb + 2)),
            pl.BlockSpec((1, SUBLANES, cw), lambda i: ((i * tm) // seq, 0, 0)),
            pl.BlockSpec((CONV_K, cw), lambda i: (0, 0)),
        ],
        out_specs=[
            pl.BlockSpec((tm, cw), lambda i: (i, 0)),
            pl.BlockSpec((1, SUBLANES, cw), lambda i: (i, 0, 0)),
        ],
        out_shape=[
            jax.ShapeDtypeStruct((m, cw), BF16),
            jax.ShapeDtypeStruct((nt, SUBLANES, cw), F32),
        ],
        compiler_params=_cparams(("parallel",), 32),
        name="conv_seq",
    )(proj, proj, proj, proj, proj, state8, conv_w)


def _conv_step_kernel(b_ref, c_ref, h_ref, st_ref, w_ref, cb_ref, u_ref):
    cw = c_ref.shape[1]
    u = c_ref[...] * h_ref[...]
    w = w_ref[...]
    conv = w[0:1, :] * st_ref[:, 0:cw] + w[1:2, :] * st_ref[:, cw:2 * cw] + w[2:3, :] * u
    cb_ref[...] = (b_ref[...] * conv).astype(BF16)
    u_ref[...] = u


def _conv_step(proj, state, conv_w, b_col, cw):
    m = proj.shape[0]
    cb = b_col // cw
    return pl.pallas_call(
        _conv_step_kernel,
        grid=(1,),
        in_specs=[
            pl.BlockSpec((m, cw), lambda i: (0, cb)),
            pl.BlockSpec((m, cw), lambda i: (0, cb + 1)),
            pl.BlockSpec((m, cw), lambda i: (0, cb + 2)),
            pl.BlockSpec((m, (CONV_K - 1) * cw), lambda i: (0, 0)),
            pl.BlockSpec((CONV_K, cw), lambda i: (0, 0)),
        ],
        out_specs=[pl.BlockSpec((m, cw), lambda i: (0, 0)), pl.BlockSpec((m, cw), lambda i: (0, 0))],
        out_shape=[jax.ShapeDtypeStruct((m, cw), BF16), jax.ShapeDtypeStruct((m, cw), F32)],
        compiler_params=_cparams(("arbitrary",), 32),
        name="conv_step",
    )(proj, proj, proj, state, conv_w)


def _merge_kernel(a_ref, cb_ref, ga_ref, gc_ref, wa_ref, wc_ref, o_ref):
    ya = jnp.dot(a_ref[...], wa_ref[...], preferred_element_type=F32)
    yc = jnp.dot(cb_ref[...], wc_ref[...], preferred_element_type=F32)
    o_ref[...] = (jax.nn.sigmoid(ga_ref[...]) * ya + jax.nn.sigmoid(gc_ref[...]) * yc).astype(o_ref.dtype)


def _merge(a, cb, proj, w_attn_br, w_conv_br, ga_col, gc_col):
    m, qw = a.shape
    cw = cb.shape[1]
    d = w_attn_br.shape[1]
    tm, tn = _row_tile(m), COL_TILE
    ga, gc = ga_col // tn, gc_col // tn
    return pl.pallas_call(
        _merge_kernel,
        grid=(m // tm, d // tn),
        in_specs=[
            pl.BlockSpec((tm, qw), lambda i, j: (i, 0)),
            pl.BlockSpec((tm, cw), lambda i, j: (i, 0)),
            pl.BlockSpec((tm, tn), lambda i, j: (i, ga + j)),
            pl.BlockSpec((tm, tn), lambda i, j: (i, gc + j)),
            pl.BlockSpec((qw, tn), lambda i, j: (0, j)),
            pl.BlockSpec((cw, tn), lambda i, j: (0, j)),
        ],
        out_specs=pl.BlockSpec((tm, tn), lambda i, j: (i, j)),
        out_shape=jax.ShapeDtypeStruct((m, d), BF16),
        compiler_params=_cparams(("parallel", "parallel"), 48),
        name="merge",
    )(a, cb, proj, proj, w_attn_br, w_conv_br)


def _outproj_kernel(m_ref, w_ref, x_ref, o_ref):
    o_ref[...] = x_ref[...] + jnp.dot(m_ref[...], w_ref[...], preferred_element_type=F32)


def _outproj(merged, w_o, x):
    m, d = merged.shape
    n = w_o.shape[1]
    tm, tn = _row_tile(m), COL_TILE
    return pl.pallas_call(
        _outproj_kernel,
        grid=(m // tm, n // tn),
        in_specs=[
            pl.BlockSpec((tm, d), lambda i, j: (i, 0)),
            pl.BlockSpec((d, tn), lambda i, j: (0, j)),
            pl.BlockSpec((tm, tn), lambda i, j: (i, j)),
        ],
        out_specs=pl.BlockSpec((tm, tn), lambda i, j: (i, j)),
        out_shape=jax.ShapeDtypeStruct((m, n), F32),
        compiler_params=_cparams(("parallel", "parallel"), 48),
        name="outproj",
    )(merged, w_o, x)


def _ffn_up_kernel(x_ref, g_ref, wg_ref, wu_ref, o_ref, hn_ref):
    @pl.when(pl.program_id(1) == 0)
    def _():
        hn_ref[...] = _rmsnorm_f32(x_ref[...], g_ref[...]).astype(BF16)

    hn = hn_ref[...]
    gate = jnp.dot(hn, wg_ref[...], preferred_element_type=F32)
    up = jnp.dot(hn, wu_ref[...], preferred_element_type=F32)
    o_ref[...] = (gate * jax.nn.sigmoid(gate) * up).astype(o_ref.dtype)


def _ffn_up(x, ln2, w_gate, w_up):
    m, d = x.shape
    f = w_gate.shape[1]
    tm, tn = _row_tile(m), COL_TILE
    assert f % tn == 0
    return pl.pallas_call(
        _ffn_up_kernel,
        grid=(m // tm, f // tn),
        in_specs=[
            pl.BlockSpec((tm, d), lambda i, j: (i, 0)),
            pl.BlockSpec((1, d), lambda i, j: (0, 0)),
            pl.BlockSpec((d, tn), lambda i, j: (0, j)),
            pl.BlockSpec((d, tn), lambda i, j: (0, j)),
        ],
        out_specs=pl.BlockSpec((tm, tn), lambda i, j: (i, j)),
        out_shape=jax.ShapeDtypeStruct((m, f), BF16),
        scratch_shapes=[pltpu.VMEM((tm, d), BF16)],
        compiler_params=_cparams(("parallel", "arbitrary"), 48),
        name="ffn_up",
    )(x, ln2.reshape(1, d), w_gate, w_up)


def _ffn_down_kernel(h_ref, w_ref, x_ref, g_ref, o_ref):
    k = pl.program_id(1)

    @pl.when(k == 0)
    def _():
        o_ref[...] = x_ref[...]

    o_ref[...] += jnp.dot(h_ref[...], w_ref[...], preferred_element_type=F32)

    @pl.when(k == pl.num_programs(1) - 1)
    def _():
        o_ref[...] = _rmsnorm_f32(o_ref[...], g_ref[...])


def _ffn_down(hmid, w_down, x, ln_f):
    m, f = hmid.shape
    d = w_down.shape[1]
    tm, tk = _row_tile(m), COL_TILE
    return pl.pallas_call(
        _ffn_down_kernel,
        grid=(m // tm, f // tk),
        in_specs=[
            pl.BlockSpec((tm, tk), lambda i, k: (i, k)),
            pl.BlockSpec((tk, d), lambda i, k: (k, 0)),
            pl.BlockSpec((tm, d), lambda i, k: (i, 0)),
            pl.BlockSpec((1, d), lambda i, k: (0, 0)),
        ],
        out_specs=pl.BlockSpec((tm, d), lambda i, k: (i, 0)),
        out_shape=jax.ShapeDtypeStruct((m, d), F32),
        compiler_params=_cparams(("parallel", "arbitrary"), 48),
        name="ffn_down",
    )(hmid, w_down, x, ln_f.reshape(1, d))


def _sample_gate_kernel(q_ref, km_ref, sel_ref):
    q = q_ref[0]
    nb = km_ref.shape[2]
    head_group = lax.broadcasted_iota(jnp.int32, (N_HEADS, nb), 0) // KV_GROUP
    gate = jnp.zeros((N_HEADS, nb), F32)
    for g in range(N_KV_HEADS):
        gg = lax.dot_general(q, km_ref[0, g], (((1,), (1,)), ((), ())),
                             precision=lax.Precision.HIGHEST, preferred_element_type=F32)
        gate = jnp.where(head_group == g, gg, gate)
    blk = lax.broadcasted_iota(jnp.int32, (N_HEADS, nb), 1).astype(F32)
    lane = lax.broadcasted_iota(jnp.int32, (N_HEADS, LANES), 1)
    out = jnp.zeros((N_HEADS, LANES), F32)
    for r in range(MOBA_TOPK):
        top = jnp.max(gate, axis=1, keepdims=True)
        first = jnp.min(jnp.where(gate == top, blk, float(nb)), axis=1, keepdims=True)
        out = jnp.where(lane == r, first, out)
        gate = jnp.where(blk == first, -jnp.inf, gate)
    sel_ref[0] = out.astype(jnp.int32)


def _sample_gate(q_s, kmean):
    db, _, nb, _ = kmean.shape
    assert MOBA_TOPK <= nb
    sel = pl.pallas_call(
        _sample_gate_kernel,
        grid=(db,),
        in_specs=[pl.BlockSpec((1, N_HEADS, HEAD_DIM), lambda b: (b, 0, 0)),
                  pl.BlockSpec((1, N_KV_HEADS, nb, HEAD_DIM), lambda b: (b, 0, 0, 0))],
        out_specs=pl.BlockSpec((1, N_HEADS, LANES), lambda b: (b, 0, 0)),
        out_shape=jax.ShapeDtypeStruct((db, N_HEADS, LANES), jnp.int32),
        compiler_params=_cparams(("parallel",), 32),
        name="sample_gate",
    )(q_s, kmean)
    return sel[:, :, :MOBA_TOPK]


def _sample_attn_kernel(pt_ref, sel_ref, q_ref, ks_ref, vs_ref, kc_ref, vc_ref, o_ref, kbuf, vbuf, sem, *,
                        pages_per_block):
    b = pl.program_id(0)
    slot = b % 2
    _, n_heads, n_sel, page, hd = kbuf.shape

    def head_copies(bb, h, sl):
        g = h // KV_GROUP
        out = []
        for t in range(n_sel):
            blk = sel_ref[bb, h * MOBA_TOPK + t // pages_per_block]
            pg = pt_ref[bb, blk * pages_per_block + t % pages_per_block]
            out.append(pltpu.make_async_copy(kc_ref.at[pg, :, g, :], kbuf.at[sl, h, t], sem.at[0, sl, h]))
            out.append(pltpu.make_async_copy(vc_ref.at[pg, :, g, :], vbuf.at[sl, h, t], sem.at[1, sl, h]))
        return out

    def start_all(bb, sl):
        def start_head(h, carry):
            for c in head_copies(bb, h, sl):
                c.start()
            return carry
        lax.fori_loop(0, n_heads, start_head, 0)

    @pl.when(b == 0)
    def _():
        start_all(b, slot)

    @pl.when(b + 1 < pl.num_programs(0))
    def _():
        start_all(b + 1, 1 - slot)

    ones = jnp.ones((2 * hd, LANES), BF16)

    def one_head(h):
        g = h // KV_GROUP
        q = q_ref[0, pl.ds(h, 1), :] * ATTN_SCALE
        prod = kbuf[slot, h].reshape(n_sel * page, hd) * q
        hi = prod.astype(BF16)
        lo = (prod - hi.astype(F32)).astype(BF16)
        s = jnp.dot(jnp.concatenate([hi, lo], axis=1), ones, preferred_element_type=F32)
        s_new = jnp.sum(ks_ref[0, pl.ds(g, 1), :] * q, axis=1, keepdims=True)
        m = jnp.maximum(jnp.max(s, axis=0, keepdims=True), s_new)
        p = jnp.exp(s - m)
        p_new = jnp.exp(s_new - m)
        l = jnp.sum(p, axis=0, keepdims=True) + p_new
        o = jnp.sum(p * vbuf[slot, h].reshape(n_sel * page, hd), axis=0, keepdims=True)
        o = o + p_new * vs_ref[0, pl.ds(g, 1), :]
        o_ref[0, pl.ds(h, 1), :] = (o / l).astype(o_ref.dtype)

    def head_group(g, carry):
        for j in range(KV_GROUP):
            for c in head_copies(b, g * KV_GROUP + j, slot):
                c.wait()
        for j in range(KV_GROUP):
            one_head(g * KV_GROUP + j)
        return carry

    lax.fori_loop(0, n_heads // KV_GROUP, head_group, 0)


def _sample_attn(q_s, k_s, v_s, cache_k4, cache_v4, page_table, sel):
    db, n_pages = page_table.shape
    page = cache_k4.shape[1]
    ppb = MOBA_BLOCK // page
    n_sel = MOBA_TOPK * ppb
    assert HEAD_DIM == LANES

    grid_spec = pltpu.PrefetchScalarGridSpec(
        num_scalar_prefetch=2,
        grid=(db,),
        in_specs=[
            pl.BlockSpec((1, N_HEADS, HEAD_DIM), lambda b, pt, sl: (b, 0, 0)),
            pl.BlockSpec((1, N_KV_HEADS, HEAD_DIM), lambda b, pt, sl: (b, 0, 0)),
            pl.BlockSpec((1, N_KV_HEADS, HEAD_DIM), lambda b, pt, sl: (b, 0, 0)),
            pl.BlockSpec(memory_space=pl.ANY),
            pl.BlockSpec(memory_space=pl.ANY),
        ],
        out_specs=pl.BlockSpec((1, N_HEADS, HEAD_DIM), lambda b, pt, sl: (b, 0, 0)),
        scratch_shapes=[
            pltpu.VMEM((2, N_HEADS, n_sel, page, HEAD_DIM), F32),
            pltpu.VMEM((2, N_HEADS, n_sel, page, HEAD_DIM), F32),
            pltpu.SemaphoreType.DMA((2, 2, N_HEADS)),
        ],
    )
    return pl.pallas_call(
        functools.partial(_sample_attn_kernel, pages_per_block=ppb),
        grid_spec=grid_spec,
        out_shape=jax.ShapeDtypeStruct((db, N_HEADS, HEAD_DIM), F32),
        compiler_params=_cparams(("arbitrary",), 40),
        name="sample_attn",
    )(page_table, sel, q_s, k_s, v_s, cache_k4, cache_v4)


def _rope_tables(pos):
    half = HEAD_DIM // 2
    inv = 1.0 / (ROPE_THETA ** (jnp.arange(half, dtype=F32) / half))
    ang = pos.astype(F32)[:, None] * inv[None, :]
    cos, sin = jnp.cos(ang), jnp.sin(ang)
    return jnp.concatenate([cos, cos], axis=-1), jnp.concatenate([-sin, sin], axis=-1)


def _trunk_tail(x, a, cb, proj, w_attn_br, w_conv_br, w_o, ln2, w_ff_gate, w_ff_up, w_ff_down, ln_f,
                ga_col, gc_col):
    merged = _merge(a, cb, proj, w_attn_br, w_conv_br, ga_col, gc_col)
    x1 = _outproj(merged, w_o, x)
    hmid = _ffn_up(x1, ln2, w_ff_gate, w_ff_up)
    return _ffn_down(hmid, w_ff_down, x1, ln_f)


def kernel(x_prompt, x_sample, cache_k, cache_v, state_conv, page_table, ln1, w_in, conv_w, w_attn_br,
           w_conv_br, w_o, ln2, w_ff_gate, w_ff_up, w_ff_down, ln_f):
    batch, seq, d = x_prompt.shape
    db, dec_seq, _ = x_sample.shape
    depth, n_phys, page, kvh, hd = cache_k.shape
    assert depth == 1 and dec_seq == 1 and kvh == N_KV_HEADS and hd == HEAD_DIM
    qw, kvw = N_HEADS * HEAD_DIM, N_KV_HEADS * HEAD_DIM
    cw = conv_w.shape[-1]
    k_col, v_col, b_col = qw, qw + kvw, qw + 2 * kvw
    ga_col = b_col + 3 * cw
    gc_col = ga_col + d
    rope_cols = qw + kvw
    past = page_table.shape[1] * page
    w_in_b = w_in[0].astype(BF16)
    trunk = (w_attn_br[0].astype(BF16), w_conv_br[0].astype(BF16), w_o[0].astype(BF16), ln2[0],
             w_ff_gate[0].astype(BF16), w_ff_up[0].astype(BF16), w_ff_down[0].astype(BF16), ln_f)

    mp = batch * seq
    xp = x_prompt.reshape(mp, d)
    cos_p, sin_p = _rope_tables(jnp.arange(seq, dtype=jnp.int32))
    cache_k3 = cache_k.reshape(n_phys, page * N_KV_HEADS, HEAD_DIM)
    cache_k4 = cache_k.reshape(n_phys, page, N_KV_HEADS, HEAD_DIM)
    cache_v4 = cache_v.reshape(n_phys, page, N_KV_HEADS, HEAD_DIM)
    proj_p = _inproj(xp, ln1[0], w_in_b, cos_p, sin_p, rope_cols)
    a_p, kmean_s = _attn_prompt(proj_p, batch, seq, k_col, v_col, cache_k3, page_table)
    zero_state = jnp.zeros((batch, SUBLANES, cw), F32)
    cb_p, utail_p = _conv_seq(proj_p, zero_state, conv_w[0], seq, b_col, cw)
    y_p = _trunk_tail(xp, a_p, cb_p, proj_p, *trunk, ga_col, gc_col)
    tiles_per_seq = utail_p.shape[0] // batch
    conv_p = utail_p.reshape(batch, tiles_per_seq, SUBLANES, cw)[:, -1, SUBLANES - (CONV_K - 1):, :]

    xs = x_sample.reshape(db, d)
    cos_s, sin_s = _rope_tables(jnp.full((db,), past, jnp.int32))
    proj_s = _inproj(xs, ln1[0], w_in_b, cos_s, sin_s, rope_cols)
    q_s = proj_s[:, :qw]
    k_s = proj_s[:, k_col:k_col + kvw]
    v_s = proj_s[:, v_col:v_col + kvw]
    sel = _sample_gate(q_s.reshape(db, N_HEADS, HEAD_DIM), kmean_s)
    a_s = _sample_attn(q_s.reshape(db, N_HEADS, HEAD_DIM), k_s.reshape(db, N_KV_HEADS, HEAD_DIM),
                       v_s.reshape(db, N_KV_HEADS, HEAD_DIM), cache_k4, cache_v4, page_table,
                       sel.reshape(db, N_HEADS * MOBA_TOPK)).reshape(db, qw).astype(BF16)
    state = state_conv[0].astype(F32)
    cb_s, u_s = _conv_step(proj_s, state.reshape(db, (CONV_K - 1) * cw), conv_w[0], b_col, cw)
    y_s = _trunk_tail(xs, a_s, cb_s, proj_s, *trunk, ga_col, gc_col)
    conv_s = jnp.concatenate([state[:, 1:, :], u_s[:, None, :]], axis=1)

    return (
        y_p.reshape(batch, seq, d),
        y_s.reshape(db, 1, d),
        proj_p[:, k_col:k_col + kvw].reshape(1, batch, seq, N_KV_HEADS, HEAD_DIM),
        proj_p[:, v_col:v_col + kvw].reshape(1, batch, seq, N_KV_HEADS, HEAD_DIM),
        conv_p[None],
        k_s.reshape(1, db, 1, N_KV_HEADS, HEAD_DIM),
        v_s.reshape(1, db, 1, N_KV_HEADS, HEAD_DIM),
        conv_s[None],
    )
```

```python
import functools

import jax
import jax.numpy as jnp
from jax import lax
from jax.experimental import pallas as pl
from jax.experimental.pallas import tpu as pltpu

F32 = jnp.float32
BF16 = jnp.bfloat16

N_HEADS = 16
HEAD_DIM = 128
N_KV_HEADS = 4
KV_GROUP = N_HEADS // N_KV_HEADS
MOBA_BLOCK = 256
MOBA_TOPK = 3
ROPE_THETA = 10000.0
CONV_K = 3
RMS_EPS = 1e-6
ATTN_SCALE = HEAD_DIM ** -0.5
LOG2E = 1.4426950408889634
MASK_BIAS = -1e30

LANES = 128
SUBLANES = 8
MIB = 1 << 20
ROW_TILE = 1024
WIDE_ROW_TILE = 2048
COL_TILE = 512
PAGES_PER_STEP = 32
ROW_GROUP = 256


def _cparams(semantics, vmem_mib):
    return pltpu.CompilerParams(dimension_semantics=semantics, vmem_limit_bytes=vmem_mib * MIB)


def _row_tile(m, cap=ROW_TILE):
    t = min(m, cap)
    assert m % t == 0 and t % SUBLANES == 0, (m, t)
    return t


def _rmsnorm_f32(x, g):
    return x * lax.rsqrt(jnp.mean(x * x, axis=-1, keepdims=True) + RMS_EPS) * g


def _mxu_weight(w_ref, wb_refs):
    if not wb_refs:
        return w_ref[...]
    w = w_ref[...].astype(BF16)
    wb_refs[0][...] = w
    return w


def _weight_out(w, block_shape, index_map):
    if w.dtype == BF16:
        return [], []
    assert w.dtype == F32
    return [pl.BlockSpec(block_shape, index_map)], [jax.ShapeDtypeStruct(w.shape, BF16)]


def _inproj_kernel(x_ref, g_ref, w_ref, cos_ref, sin_ref, o_ref, *rest, rope_tiles):
    *wb_refs, xn_ref = rest
    j = pl.program_id(1)

    @pl.when(j == 0)
    def _():
        xn_ref[...] = _rmsnorm_f32(x_ref[...], g_ref[...]).astype(BF16)

    acc = jnp.dot(xn_ref[...], _mxu_weight(w_ref, wb_refs), preferred_element_type=F32)

    @pl.when(j < rope_tiles)
    def _():
        cos = cos_ref[...]
        sin = sin_ref[...]
        for c in range(acc.shape[1] // HEAD_DIM):
            a = acc[:, c * HEAD_DIM:(c + 1) * HEAD_DIM]
            o_ref[:, c * HEAD_DIM:(c + 1) * HEAD_DIM] = a * cos + pltpu.roll(a, HEAD_DIM // 2, axis=1) * sin

    @pl.when(j >= rope_tiles)
    def _():
        o_ref[...] = acc


def _block_means(get_page, n_pages, page_rows):
    pages_per_block = MOBA_BLOCK * N_KV_HEADS // page_rows
    blocks = n_pages // pages_per_block
    row = lax.broadcasted_iota(jnp.int32, (blocks, HEAD_DIM), 0)
    means = [jnp.zeros((blocks, HEAD_DIM), F32) for _ in range(N_KV_HEADS)]
    for r in range(blocks):
        tot = jnp.zeros((SUBLANES, HEAD_DIM), F32)
        for t in range(pages_per_block):
            x = get_page(r * pages_per_block + t)
            tot = tot + jnp.sum(x.reshape(page_rows // SUBLANES, SUBLANES, HEAD_DIM), axis=0)
        head_sum = tot[0:N_KV_HEADS, :]
        for c in range(1, SUBLANES // N_KV_HEADS):
            head_sum = head_sum + tot[c * N_KV_HEADS:(c + 1) * N_KV_HEADS, :]
        head_mean = head_sum * (1.0 / MOBA_BLOCK)
        for g in range(N_KV_HEADS):
            means[g] = jnp.where(row == r, head_mean[g:g + 1, :], means[g])
    return means


class _KeyStream:
    def __init__(self, pt_ref, kc_ref, km_ref, pbuf, psem, step, chunks_per_seq, n_chunks, every_step):
        self.pt_ref, self.kc_ref, self.km_ref, self.pbuf, self.psem = pt_ref, kc_ref, km_ref, pbuf, psem
        self.step, self.chunks_per_seq, self.n_chunks = step, chunks_per_seq, n_chunks
        self.every_step = every_step
        self.slot = step % 2
        self.pages = pbuf.shape[1]

    def _copies(self, s, sl):
        b, c = s // self.chunks_per_seq, s % self.chunks_per_seq
        return [pltpu.make_async_copy(self.kc_ref.at[self.pt_ref[b, c * self.pages + t]],
                                      self.pbuf.at[sl, t], self.psem.at[sl])
                for t in range(self.pages)]

    def prefetch(self):
        @pl.when(self.step == 0)
        def _():
            for c in self._copies(self.step, self.slot):
                c.start()

        @pl.when(self.step + 1 < self.n_chunks)
        def _():
            for c in self._copies(self.step + 1, 1 - self.slot):
                c.start()

    def reduce(self):
        def body():
            for c in self._copies(self.step, self.slot):
                c.wait()
            means = _block_means(lambda t: self.pbuf[self.slot, t], self.pages, self.pbuf.shape[2])
            for g in range(N_KV_HEADS):
                self.km_ref[0, g] = means[g]

        if self.every_step:
            body()
        else:
            pl.when(self.step < self.n_chunks)(body)


def _key_stream_specs(cache_k3, page_table, n_steps, step_of):
    db, n_pages = page_table.shape
    _, page_rows, hd = cache_k3.shape
    page = page_rows // N_KV_HEADS
    pages = PAGES_PER_STEP
    assert MOBA_BLOCK % page == 0 and n_pages % pages == 0 and SUBLANES % N_KV_HEADS == 0
    blocks = pages * page // MOBA_BLOCK
    assert blocks % SUBLANES == 0 and hd == HEAD_DIM
    chunks_per_seq = n_pages // pages
    n_chunks = db * chunks_per_seq
    nb = n_pages * page // MOBA_BLOCK
    assert n_chunks <= n_steps, "not enough host grid steps to stream the key cache"

    def km_map(*ids_and_pt):
        s = jnp.minimum(step_of(*ids_and_pt[:-1]), n_chunks - 1)
        return (s // chunks_per_seq, 0, s % chunks_per_seq, 0)

    return (dict(chunks_per_seq=chunks_per_seq, n_chunks=n_chunks, every_step=n_chunks == n_steps),
            pl.BlockSpec((1, N_KV_HEADS, blocks, HEAD_DIM), km_map),
            jax.ShapeDtypeStruct((db, N_KV_HEADS, nb, HEAD_DIM), F32),
            [pltpu.VMEM((2, pages, page_rows, HEAD_DIM), F32), pltpu.SemaphoreType.DMA((2,))])


def _inproj(x, ln1, w_in, cos, sin, rope_cols):
    m, d = x.shape
    n = w_in.shape[1]
    tm, tn = _row_tile(m, WIDE_ROW_TILE), COL_TILE
    assert n % tn == 0 and rope_cols % tn == 0
    assert cos.shape[0] % tm == 0 and m % cos.shape[0] == 0
    table_tiles = cos.shape[0] // tm
    wb_specs, wb_shapes = _weight_out(w_in, (d, tn), lambda i, j: (0, j))
    assert not wb_specs or m == tm, "the bf16 weight copy is written once per weight tile"
    out = pl.pallas_call(
        functools.partial(_inproj_kernel, rope_tiles=rope_cols // tn),
        grid=(m // tm, n // tn),
        in_specs=[
            pl.BlockSpec((tm, d), lambda i, j: (i, 0), pipeline_mode=pl.Buffered(1)),
            pl.BlockSpec((1, d), lambda i, j: (0, 0)),
            pl.BlockSpec((d, tn), lambda i, j: (0, j)),
            pl.BlockSpec((tm, HEAD_DIM), lambda i, j: (i % table_tiles, 0)),
            pl.BlockSpec((tm, HEAD_DIM), lambda i, j: (i % table_tiles, 0)),
        ],
        out_specs=[pl.BlockSpec((tm, tn), lambda i, j: (i, j))] + wb_specs,
        out_shape=[jax.ShapeDtypeStruct((m, n), F32)] + wb_shapes,
        scratch_shapes=[pltpu.VMEM((tm, d), BF16)],
        compiler_params=_cparams(("parallel", "arbitrary"), 56),
        name="inproj",
    )(x, ln1.reshape(1, d), w_in, cos, sin)
    return out[0], (out[1] if wb_specs else w_in)


def _attn_prompt_kernel(pt_ref, q_ref, k_ref, v_ref, kc_ref, o_ref, km_ref, ka_ref, va_ref, kmean_ref, qa_ref,
                        m_ref, acc_ref, pbuf, psem, *, chunks_per_seq, n_chunks, every_step):
    qi = pl.program_id(2)
    seq = k_ref.shape[0]
    nb = seq // MOBA_BLOCK
    rows = KV_GROUP * MOBA_BLOCK
    pair = 2 * MOBA_BLOCK
    step = (pl.program_id(0) * pl.num_programs(1) + pl.program_id(1)) * pl.num_programs(2) + qi
    stream = _KeyStream(pt_ref, kc_ref, km_ref, pbuf, psem, step, chunks_per_seq, n_chunks, every_step)
    stream.prefetch()

    @pl.when(qi == 0)
    def _():
        k = k_ref[...]
        row_blk = lax.broadcasted_iota(jnp.int32, (seq, LANES), 0) // MOBA_BLOCK
        lane = lax.broadcasted_iota(jnp.int32, (seq, LANES), 1)
        ka_ref[:, 0:HEAD_DIM] = k.astype(BF16)
        ka_ref[:, HEAD_DIM:] = jnp.where(lane == row_blk, 1.0, 0.0).astype(BF16)
        va_ref[:, 0:HEAD_DIM] = v_ref[...].astype(BF16)
        va_ref[:, HEAD_DIM:] = jnp.ones((seq, LANES), BF16)
        kmean = jnp.mean(k.reshape(nb, MOBA_BLOCK, HEAD_DIM), axis=1)
        km_hi = kmean.astype(BF16)
        km_lo = (kmean - km_hi.astype(F32)).astype(BF16)
        kmean_ref[...] = jnp.zeros_like(kmean_ref)
        kmean_ref[0:nb, :] = jnp.concatenate([km_hi, km_hi, km_lo], axis=1)

    stream.reduce()

    q = q_ref[...]
    q4 = jnp.concatenate([q[:, h * HEAD_DIM:(h + 1) * HEAD_DIM] for h in range(KV_GROUP)], axis=0)
    qs = (q4 * (ATTN_SCALE * LOG2E)).astype(BF16)

    q_hi = q4.astype(BF16)
    q_lo = (q4 - q_hi.astype(F32)).astype(BF16)
    gate = lax.dot_general(jnp.concatenate([q_hi, q_lo, q_hi], axis=1), kmean_ref[...],
                           (((1,), (1,)), ((), ())), preferred_element_type=F32)
    lane_i = lax.broadcasted_iota(jnp.int32, (rows, LANES), 1)
    lane = lane_i.astype(F32)
    past = lane_i < qi
    gate = jnp.where(past, gate, -jnp.inf)
    bias = jnp.full((rows, LANES), MASK_BIAS, F32)
    for _ in range(MOBA_TOPK):
        top = jnp.max(gate, axis=1, keepdims=True)
        pick = lane == jnp.min(jnp.where(gate == top, lane, float(LANES)), axis=1, keepdims=True)
        bias = jnp.where(pick & past, 0.0, bias)
        gate = jnp.where(pick, -jnp.inf, gate)
    qa = jnp.concatenate([qs, bias.astype(BF16)], axis=1)

    own = pl.multiple_of(qi * MOBA_BLOCK, MOBA_BLOCK)
    s = lax.dot_general(qs, ka_ref[pl.ds(own, MOBA_BLOCK), 0:HEAD_DIM], (((1,), (1,)), ((), ())),
                        preferred_element_type=F32)
    qrow = lax.broadcasted_iota(jnp.int32, (rows, MOBA_BLOCK), 0) & (MOBA_BLOCK - 1)
    kcol = lax.broadcasted_iota(jnp.int32, (rows, MOBA_BLOCK), 1)
    s = jnp.where(kcol <= qrow, s, MASK_BIAS)
    m0 = jnp.broadcast_to(jnp.max(s, axis=1, keepdims=True), (rows, LANES))
    p = jnp.concatenate([jnp.exp2(s[:, c * LANES:(c + 1) * LANES] - m0)
                         for c in range(MOBA_BLOCK // LANES)], axis=1)
    m_ref[...] = m0
    acc_ref[...] = jnp.dot(p.astype(BF16), va_ref[pl.ds(own, MOBA_BLOCK), :], preferred_element_type=F32)

    qa_ref[...] = qa

    def past_pair(t, carry):
        start = pl.multiple_of(t * pair, pair)
        kt = ka_ref[pl.ds(start, pair), :]
        vt = va_ref[pl.ds(start, pair), :]
        groups = [slice(r0, r0 + ROW_GROUP) for r0 in range(0, rows, ROW_GROUP)]

        def scores(rs):
            return lax.dot_general(qa_ref[rs, :], kt, (((1,), (1,)), ((), ())), preferred_element_type=F32)

        m_olds = [m_ref[rs, :] for rs in groups]
        s_next = scores(groups[0])
        updates = []
        for gi, rs in enumerate(groups):
            s = s_next
            if gi + 1 < len(groups):
                s_next = scores(groups[gi + 1])
            m_old = m_olds[gi]
            m_new = jnp.maximum(m_old, jnp.max(s, axis=1, keepdims=True))
            alpha = jnp.exp2(m_old - m_new)
            p = jnp.concatenate([jnp.exp2(s[:, c * LANES:(c + 1) * LANES] - m_new)
                                 for c in range(pair // LANES)], axis=1)
            pv = jnp.dot(p.astype(BF16), vt, preferred_element_type=F32)
            updates.append((rs, m_new, alpha, pv))
        for rs, m_new, alpha, pv in updates:
            for c in range(2):
                cs = slice(c * LANES, (c + 1) * LANES)
                acc_ref[rs, cs] = alpha * acc_ref[rs, cs] + pv[:, cs]
            m_ref[rs, :] = m_new
        return carry

    lax.fori_loop(0, (qi + 1) // 2, past_pair, 0)

    o = acc_ref[:, 0:HEAD_DIM] / acc_ref[:, HEAD_DIM:]
    for h in range(KV_GROUP):
        o_ref[:, h * HEAD_DIM:(h + 1) * HEAD_DIM] = o[h * MOBA_BLOCK:(h + 1) * MOBA_BLOCK].astype(o_ref.dtype)


def _attn_prompt(proj, batch, seq, k_col, v_col, cache_k3, page_table):
    nq = seq // MOBA_BLOCK
    assert seq % (2 * MOBA_BLOCK) == 0 and MOBA_TOPK <= nq <= LANES
    gw = KV_GROUP * HEAD_DIM
    rows = KV_GROUP * MOBA_BLOCK
    kblk, vblk = k_col // HEAD_DIM, v_col // HEAD_DIM
    grid = (batch, N_KV_HEADS, nq)
    stream_kwargs, km_spec, km_shape, stream_scratch = _key_stream_specs(
        cache_k3, page_table, batch * N_KV_HEADS * nq, lambda b, g, i: (b * N_KV_HEADS + g) * nq + i)
    grid_spec = pltpu.PrefetchScalarGridSpec(
        num_scalar_prefetch=1,
        grid=grid,
        in_specs=[
            pl.BlockSpec((MOBA_BLOCK, gw), lambda b, g, i, pt: (b * nq + i, g)),
            pl.BlockSpec((seq, HEAD_DIM), lambda b, g, i, pt: (b, kblk + g)),
            pl.BlockSpec((seq, HEAD_DIM), lambda b, g, i, pt: (b, vblk + g)),
            pl.BlockSpec(memory_space=pl.ANY),
        ],
        out_specs=[pl.BlockSpec((MOBA_BLOCK, gw), lambda b, g, i, pt: (b * nq + i, g)), km_spec],
        scratch_shapes=[
            pltpu.VMEM((seq, HEAD_DIM + LANES), BF16),
            pltpu.VMEM((seq, HEAD_DIM + LANES), BF16),
            pltpu.VMEM((LANES, 3 * HEAD_DIM), BF16),
            pltpu.VMEM((rows, HEAD_DIM + LANES), BF16),
            pltpu.VMEM((rows, LANES), F32),
            pltpu.VMEM((rows, HEAD_DIM + LANES), F32),
        ] + stream_scratch,
    )
    return pl.pallas_call(
        functools.partial(_attn_prompt_kernel, **stream_kwargs),
        grid_spec=grid_spec,
        out_shape=[jax.ShapeDtypeStruct((batch * seq, N_HEADS * HEAD_DIM), BF16), km_shape],
        compiler_params=_cparams(("arbitrary", "arbitrary", "arbitrary"), 48),
        name="attn_prompt",
    )(page_table, proj, proj, proj, cache_k3)


def _conv_seq_kernel(b_ref, c_ref, h_ref, cp_ref, hp_ref, st_ref, w_ref, cb_ref, ut_ref, *, seq):
    i = pl.program_id(0)
    tm = c_ref.shape[0]
    u = c_ref[...] * h_ref[...]
    prev = jnp.where((i * tm) % seq == 0, st_ref[0], cp_ref[...] * hp_ref[...])
    p1 = prev[SUBLANES - 1:SUBLANES, :]
    p2 = prev[SUBLANES - 2:SUBLANES - 1, :]
    row = lax.broadcasted_iota(jnp.int32, u.shape, 0)
    u1 = jnp.where(row == 0, p1, pltpu.roll(u, 1, axis=0))
    u2 = jnp.where(row == 0, p2, jnp.where(row == 1, p1, pltpu.roll(u, 2, axis=0)))
    w = w_ref[...]
    conv = w[0:1, :] * u2 + w[1:2, :] * u1 + w[2:3, :] * u
    cb_ref[...] = (b_ref[...] * conv).astype(BF16)
    ut_ref[0] = u[tm - SUBLANES:tm, :]


def _conv_seq(proj, state8, conv_w, seq, b_col, cw):
    m = proj.shape[0]
    tm = _row_tile(seq, 512)
    nt = m // tm
    cb = b_col // cw
    pstep = tm // SUBLANES
    return pl.pallas_call(
        functools.partial(_conv_seq_kernel, seq=seq),
        grid=(nt,),
        in_specs=[
            pl.BlockSpec((tm, cw), lambda i: (i, cb)),
            pl.BlockSpec((tm, cw), lambda i: (i, cb + 1)),
            pl.BlockSpec((tm, cw), lambda i: (i, cb + 2)),
            pl.BlockSpec((SUBLANES, cw), lambda i: (jnp.maximum(i * pstep - 1, 0), cb + 1)),
            pl.BlockSpec((SUBLANES, cw), lambda i: (jnp.maximum(i * pstep - 1, 0), cb + 2)),
            pl.BlockSpec((1, SUBLANES, cw), lambda i: ((i * tm) // seq, 0, 0)),
            pl.BlockSpec((CONV_K, cw), lambda i: (0, 0)),
        ],
        out_specs=[
            pl.BlockSpec((tm, cw), lambda i: (i, 0)),
            pl.BlockSpec((1, SUBLANES, cw), lambda i: (i, 0, 0)),
        ],
        out_shape=[
            jax.ShapeDtypeStruct((m, cw), BF16),
            jax.ShapeDtypeStruct((nt, SUBLANES, cw), F32),
        ],
        compiler_params=_cparams(("parallel",), 32),
        name="conv_seq",
    )(proj, proj, proj, proj, proj, state8, conv_w)


def _conv_step_kernel(b_ref, c_ref, h_ref, st_ref, w_ref, cb_ref, u_ref):
    cw = c_ref.shape[1]
    u = c_ref[...] * h_ref[...]
    w = w_ref[...]
    conv = w[0:1, :] * st_ref[:, 0:cw] + w[1:2, :] * st_ref[:, cw:2 * cw] + w[2:3, :] * u
    cb_ref[...] = (b_ref[...] * conv).astype(BF16)
    u_ref[...] = u


def _conv_step(proj, state, conv_w, b_col, cw):
    m = proj.shape[0]
    cb = b_col // cw
    return pl.pallas_call(
        _conv_step_kernel,
        grid=(1,),
        in_specs=[
            pl.BlockSpec((m, cw), lambda i: (0, cb)),
            pl.BlockSpec((m, cw), lambda i: (0, cb + 1)),
            pl.BlockSpec((m, cw), lambda i: (0, cb + 2)),
            pl.BlockSpec((m, (CONV_K - 1) * cw), lambda i: (0, 0)),
            pl.BlockSpec((CONV_K, cw), lambda i: (0, 0)),
        ],
        out_specs=[pl.BlockSpec((m, cw), lambda i: (0, 0)), pl.BlockSpec((m, cw), lambda i: (0, 0))],
        out_shape=[jax.ShapeDtypeStruct((m, cw), BF16), jax.ShapeDtypeStruct((m, cw), F32)],
        compiler_params=_cparams(("arbitrary",), 32),
        name="conv_step",
    )(proj, proj, proj, state, conv_w)


def _merge_kernel(a_ref, cb_ref, ga_ref, gc_ref, wa_ref, wc_ref, o_ref, *wb_refs):
    ya = jnp.dot(a_ref[...], _mxu_weight(wa_ref, wb_refs[0:1]), preferred_element_type=F32)
    yc = jnp.dot(cb_ref[...], _mxu_weight(wc_ref, wb_refs[1:2]), preferred_element_type=F32)
    o_ref[...] = (jax.nn.sigmoid(ga_ref[...]) * ya + jax.nn.sigmoid(gc_ref[...]) * yc).astype(o_ref.dtype)


def _merge(a, cb, proj, w_attn_br, w_conv_br, ga_col, gc_col):
    m, qw = a.shape
    cw = cb.shape[1]
    d = w_attn_br.shape[1]
    tm, tn = _row_tile(m, WIDE_ROW_TILE), COL_TILE
    ga, gc = ga_col // tn, gc_col // tn
    assert w_attn_br.dtype == w_conv_br.dtype
    wa_specs, wa_shapes = _weight_out(w_attn_br, (qw, tn), lambda i, j: (0, j))
    wc_specs, wc_shapes = _weight_out(w_conv_br, (cw, tn), lambda i, j: (0, j))
    assert not wa_specs or m == tm, "the bf16 weight copy is written once per weight tile"
    out = pl.pallas_call(
        _merge_kernel,
        grid=(m // tm, d // tn),
        in_specs=[
            pl.BlockSpec((tm, qw), lambda i, j: (i, 0), pipeline_mode=pl.Buffered(1)),
            pl.BlockSpec((tm, cw), lambda i, j: (i, 0), pipeline_mode=pl.Buffered(1)),
            pl.BlockSpec((tm, tn), lambda i, j: (i, ga + j)),
            pl.BlockSpec((tm, tn), lambda i, j: (i, gc + j)),
            pl.BlockSpec((qw, tn), lambda i, j: (0, j)),
            pl.BlockSpec((cw, tn), lambda i, j: (0, j)),
        ],
        out_specs=[pl.BlockSpec((tm, tn), lambda i, j: (i, j))] + wa_specs + wc_specs,
        out_shape=[jax.ShapeDtypeStruct((m, d), BF16)] + wa_shapes + wc_shapes,
        compiler_params=_cparams(("parallel", "parallel"), 56),
        name="merge",
    )(a, cb, proj, proj, w_attn_br, w_conv_br)
    return out[0], ((out[1], out[2]) if wa_specs else (w_attn_br, w_conv_br))


def _outproj_kernel(m_ref, w_ref, x_ref, o_ref, *wb_refs):
    o_ref[...] = x_ref[...] + jnp.dot(m_ref[...], _mxu_weight(w_ref, wb_refs), preferred_element_type=F32)


def _outproj(merged, w_o, x):
    m, d = merged.shape
    n = w_o.shape[1]
    tm, tn = _row_tile(m, WIDE_ROW_TILE), COL_TILE
    wb_specs, wb_shapes = _weight_out(w_o, (d, tn), lambda i, j: (0, j))
    assert not wb_specs or m == tm, "the bf16 weight copy is written once per weight tile"
    out = pl.pallas_call(
        _outproj_kernel,
        grid=(m // tm, n // tn),
        in_specs=[
            pl.BlockSpec((tm, d), lambda i, j: (i, 0), pipeline_mode=pl.Buffered(1)),
            pl.BlockSpec((d, tn), lambda i, j: (0, j)),
            pl.BlockSpec((tm, tn), lambda i, j: (i, j)),
        ],
        out_specs=[pl.BlockSpec((tm, tn), lambda i, j: (i, j))] + wb_specs,
        out_shape=[jax.ShapeDtypeStruct((m, n), F32)] + wb_shapes,
        compiler_params=_cparams(("parallel", "parallel"), 48),
        name="outproj",
    )(merged, w_o, x)
    return out[0], (out[1] if wb_specs else w_o)


def _ffn_up_kernel(x_ref, g_ref, wg_ref, wu_ref, o_ref, *rest):
    *wb_refs, hn_ref = rest

    @pl.when(pl.program_id(1) == 0)
    def _():
        hn_ref[...] = _rmsnorm_f32(x_ref[...], g_ref[...]).astype(BF16)

    hn = hn_ref[...]
    gate = jnp.dot(hn, _mxu_weight(wg_ref, wb_refs[0:1]), preferred_element_type=F32)
    up = jnp.dot(hn, _mxu_weight(wu_ref, wb_refs[1:2]), preferred_element_type=F32)
    o_ref[...] = (gate * jax.nn.sigmoid(gate) * up).astype(o_ref.dtype)


def _ffn_up(x, ln2, w_gate, w_up):
    m, d = x.shape
    f = w_gate.shape[1]
    tm, tn = _row_tile(m, WIDE_ROW_TILE), COL_TILE
    assert f % tn == 0 and w_gate.dtype == w_up.dtype
    wg_specs, wg_shapes = _weight_out(w_gate, (d, tn), lambda i, j: (0, j))
    wu_specs, wu_shapes = _weight_out(w_up, (d, tn), lambda i, j: (0, j))
    assert not wg_specs or m == tm, "the bf16 weight copy is written once per weight tile"
    out = pl.pallas_call(
        _ffn_up_kernel,
        grid=(m // tm, f // tn),
        in_specs=[
            pl.BlockSpec((tm, d), lambda i, j: (i, 0), pipeline_mode=pl.Buffered(1)),
            pl.BlockSpec((1, d), lambda i, j: (0, 0)),
            pl.BlockSpec((d, tn), lambda i, j: (0, j)),
            pl.BlockSpec((d, tn), lambda i, j: (0, j)),
        ],
        out_specs=[pl.BlockSpec((tm, tn), lambda i, j: (i, j))] + wg_specs + wu_specs,
        out_shape=[jax.ShapeDtypeStruct((m, f), BF16)] + wg_shapes + wu_shapes,
        scratch_shapes=[pltpu.VMEM((tm, d), BF16)],
        compiler_params=_cparams(("parallel", "arbitrary"), 56),
        name="ffn_up",
    )(x, ln2.reshape(1, d), w_gate, w_up)
    return out[0], ((out[1], out[2]) if wg_specs else (w_gate, w_up))


def _ffn_down_kernel(h_ref, w_ref, x_ref, g_ref, o_ref, *wb_refs):
    k = pl.program_id(1)

    @pl.when(k == 0)
    def _():
        o_ref[...] = x_ref[...]

    o_ref[...] += jnp.dot(h_ref[...], _mxu_weight(w_ref, wb_refs), preferred_element_type=F32)

    @pl.when(k == pl.num_programs(1) - 1)
    def _():
        o_ref[...] = _rmsnorm_f32(o_ref[...], g_ref[...])


def _ffn_down(hmid, w_down, x, ln_f):
    m, f = hmid.shape
    d = w_down.shape[1]
    tm, tk = _row_tile(m), COL_TILE
    wb_specs, wb_shapes = _weight_out(w_down, (tk, d), lambda i, k: (k, 0))
    assert not wb_specs or m == tm, "the bf16 weight copy is written once per weight tile"
    out = pl.pallas_call(
        _ffn_down_kernel,
        grid=(m // tm, f // tk),
        in_specs=[
            pl.BlockSpec((tm, tk), lambda i, k: (i, k)),
            pl.BlockSpec((tk, d), lambda i, k: (k, 0)),
            pl.BlockSpec((tm, d), lambda i, k: (i, 0)),
            pl.BlockSpec((1, d), lambda i, k: (0, 0)),
        ],
        out_specs=[pl.BlockSpec((tm, d), lambda i, k: (i, 0))] + wb_specs,
        out_shape=[jax.ShapeDtypeStruct((m, d), F32)] + wb_shapes,
        compiler_params=_cparams(("parallel", "arbitrary"), 48),
        name="ffn_down",
    )(hmid, w_down, x, ln_f.reshape(1, d))
    return out[0], (out[1] if wb_specs else w_down)


def _sample_gate_kernel(q_ref, km_ref, sel_ref):
    q = q_ref[0]
    nb = km_ref.shape[2]
    head_group = lax.broadcasted_iota(jnp.int32, (N_HEADS, nb), 0) // KV_GROUP
    gate = jnp.zeros((N_HEADS, nb), F32)
    for g in range(N_KV_HEADS):
        gg = lax.dot_general(q, km_ref[0, g], (((1,), (1,)), ((), ())),
                             precision=lax.Precision.HIGHEST, preferred_element_type=F32)
        gate = jnp.where(head_group == g, gg, gate)
    blk = lax.broadcasted_iota(jnp.int32, (N_HEADS, nb), 1).astype(F32)
    lane = lax.broadcasted_iota(jnp.int32, (N_HEADS, LANES), 1)
    out = jnp.zeros((N_HEADS, LANES), F32)
    for r in range(MOBA_TOPK):
        top = jnp.max(gate, axis=1, keepdims=True)
        first = jnp.min(jnp.where(gate == top, blk, float(nb)), axis=1, keepdims=True)
        out = jnp.where(lane == r, first, out)
        gate = jnp.where(blk == first, -jnp.inf, gate)
    sel_ref[0] = out.astype(jnp.int32)


def _sample_gate(q_s, kmean):
    db, _, nb, _ = kmean.shape
    assert MOBA_TOPK <= nb
    sel = pl.pallas_call(
        _sample_gate_kernel,
        grid=(db,),
        in_specs=[pl.BlockSpec((1, N_HEADS, HEAD_DIM), lambda b: (b, 0, 0)),
                  pl.BlockSpec((1, N_KV_HEADS, nb, HEAD_DIM), lambda b: (b, 0, 0, 0))],
        out_specs=pl.BlockSpec((1, N_HEADS, LANES), lambda b: (b, 0, 0)),
        out_shape=jax.ShapeDtypeStruct((db, N_HEADS, LANES), jnp.int32),
        compiler_params=_cparams(("parallel",), 32),
        name="sample_gate",
    )(q_s, kmean)
    return sel[:, :, :MOBA_TOPK]


def _sample_attn_kernel(pt_ref, sel_ref, q_ref, ks_ref, vs_ref, kc_ref, vc_ref, o_ref, kbuf, vbuf, sem, *,
                        pages_per_block):
    b = pl.program_id(0)
    slot = b % 2
    _, n_heads, n_sel, page, hd = kbuf.shape

    def head_copies(bb, h, sl):
        g = h // KV_GROUP
        out = []
        for t in range(n_sel):
            blk = sel_ref[bb, h * MOBA_TOPK + t // pages_per_block]
            pg = pt_ref[bb, blk * pages_per_block + t % pages_per_block]
            out.append(pltpu.make_async_copy(kc_ref.at[pg, :, g, :], kbuf.at[sl, h, t], sem.at[0, sl, h]))
            out.append(pltpu.make_async_copy(vc_ref.at[pg, :, g, :], vbuf.at[sl, h, t], sem.at[1, sl, h]))
        return out

    def start_all(bb, sl):
        def start_head(h, carry):
            for c in head_copies(bb, h, sl):
                c.start()
            return carry
        lax.fori_loop(0, n_heads, start_head, 0)

    @pl.when(b == 0)
    def _():
        start_all(b, slot)

    @pl.when(b + 1 < pl.num_programs(0))
    def _():
        start_all(b + 1, 1 - slot)

    ones = jnp.ones((2 * hd, LANES), BF16)

    def one_head(h):
        g = h // KV_GROUP
        q = q_ref[0, pl.ds(h, 1), :] * ATTN_SCALE
        prod = kbuf[slot, h].reshape(n_sel * page, hd) * q
        hi = prod.astype(BF16)
        lo = (prod - hi.astype(F32)).astype(BF16)
        s = jnp.dot(jnp.concatenate([hi, lo], axis=1), ones, preferred_element_type=F32)
        s_new = jnp.sum(ks_ref[0, pl.ds(g, 1), :] * q, axis=1, keepdims=True)
        m = jnp.maximum(jnp.max(s, axis=0, keepdims=True), s_new)
        p = jnp.exp(s - m)
        p_new = jnp.exp(s_new - m)
        l = jnp.sum(p, axis=0, keepdims=True) + p_new
        o = jnp.sum(p * vbuf[slot, h].reshape(n_sel * page, hd), axis=0, keepdims=True)
        o = o + p_new * vs_ref[0, pl.ds(g, 1), :]
        o_ref[0, pl.ds(h, 1), :] = (o / l).astype(o_ref.dtype)

    def head_group(g, carry):
        for j in range(KV_GROUP):
            for c in head_copies(b, g * KV_GROUP + j, slot):
                c.wait()
        for j in range(KV_GROUP):
            one_head(g * KV_GROUP + j)
        return carry

    lax.fori_loop(0, n_heads // KV_GROUP, head_group, 0)


def _sample_attn(q_s, k_s, v_s, cache_k4, cache_v4, page_table, sel):
    db, n_pages = page_table.shape
    page = cache_k4.shape[1]
    ppb = MOBA_BLOCK // page
    n_sel = MOBA_TOPK * ppb
    assert HEAD_DIM == LANES

    grid_spec = pltpu.PrefetchScalarGridSpec(
        num_scalar_prefetch=2,
        grid=(db,),
        in_specs=[
            pl.BlockSpec((1, N_HEADS, HEAD_DIM), lambda b, pt, sl: (b, 0, 0)),
            pl.BlockSpec((1, N_KV_HEADS, HEAD_DIM), lambda b, pt, sl: (b, 0, 0)),
            pl.BlockSpec((1, N_KV_HEADS, HEAD_DIM), lambda b, pt, sl: (b, 0, 0)),
            pl.BlockSpec(memory_space=pl.ANY),
            pl.BlockSpec(memory_space=pl.ANY),
        ],
        out_specs=pl.BlockSpec((1, N_HEADS, HEAD_DIM), lambda b, pt, sl: (b, 0, 0)),
        scratch_shapes=[
            pltpu.VMEM((2, N_HEADS, n_sel, page, HEAD_DIM), F32),
            pltpu.VMEM((2, N_HEADS, n_sel, page, HEAD_DIM), F32),
            pltpu.SemaphoreType.DMA((2, 2, N_HEADS)),
        ],
    )
    return pl.pallas_call(
        functools.partial(_sample_attn_kernel, pages_per_block=ppb),
        grid_spec=grid_spec,
        out_shape=jax.ShapeDtypeStruct((db, N_HEADS, HEAD_DIM), F32),
        compiler_params=_cparams(("arbitrary",), 40),
        name="sample_attn",
    )(page_table, sel, q_s, k_s, v_s, cache_k4, cache_v4)


def _rope_tables(pos):
    half = HEAD_DIM // 2
    inv = 1.0 / (ROPE_THETA ** (jnp.arange(half, dtype=F32) / half))
    ang = pos.astype(F32)[:, None] * inv[None, :]
    cos, sin = jnp.cos(ang), jnp.sin(ang)
    return jnp.concatenate([cos, cos], axis=-1), jnp.concatenate([-sin, sin], axis=-1)


def _trunk_tail(x, a, cb, proj, w_attn_br, w_conv_br, w_o, ln2, w_ff_gate, w_ff_up, w_ff_down, ln_f,
                ga_col, gc_col):
    merged, (w_attn_br, w_conv_br) = _merge(a, cb, proj, w_attn_br, w_conv_br, ga_col, gc_col)
    x1, w_o = _outproj(merged, w_o, x)
    hmid, (w_ff_gate, w_ff_up) = _ffn_up(x1, ln2, w_ff_gate, w_ff_up)
    y, w_ff_down = _ffn_down(hmid, w_ff_down, x1, ln_f)
    return y, (w_attn_br, w_conv_br, w_o, ln2, w_ff_gate, w_ff_up, w_ff_down, ln_f)


def kernel(x_prompt, x_sample, cache_k, cache_v, state_conv, page_table, ln1, w_in, conv_w, w_attn_br,
           w_conv_br, w_o, ln2, w_ff_gate, w_ff_up, w_ff_down, ln_f):
    batch, seq, d = x_prompt.shape
    db, dec_seq, _ = x_sample.shape
    depth, n_phys, page, kvh, hd = cache_k.shape
    assert depth == 1 and dec_seq == 1 and kvh == N_KV_HEADS and hd == HEAD_DIM
    qw, kvw = N_HEADS * HEAD_DIM, N_KV_HEADS * HEAD_DIM
    cw = conv_w.shape[-1]
    k_col, v_col, b_col = qw, qw + kvw, qw + 2 * kvw
    ga_col = b_col + 3 * cw
    gc_col = ga_col + d
    rope_cols = qw + kvw
    past = page_table.shape[1] * page
    trunk_f32 = (w_attn_br[0], w_conv_br[0], w_o[0], ln2[0], w_ff_gate[0], w_ff_up[0], w_ff_down[0], ln_f)
    cache_k3 = cache_k.reshape(n_phys, page * N_KV_HEADS, HEAD_DIM)
    cache_k4 = cache_k.reshape(n_phys, page, N_KV_HEADS, HEAD_DIM)
    cache_v4 = cache_v.reshape(n_phys, page, N_KV_HEADS, HEAD_DIM)

    xs = x_sample.reshape(db, d)
    cos_s, sin_s = _rope_tables(jnp.full((db,), past, jnp.int32))
    proj_s, w_in_b = _inproj(xs, ln1[0], w_in[0], cos_s, sin_s, rope_cols)
    q_s = proj_s[:, :qw]
    k_s = proj_s[:, k_col:k_col + kvw]
    v_s = proj_s[:, v_col:v_col + kvw]
    mp = batch * seq
    xp = x_prompt.reshape(mp, d)
    cos_p, sin_p = _rope_tables(jnp.arange(seq, dtype=jnp.int32))
    proj_p, _ = _inproj(xp, ln1[0], w_in_b, cos_p, sin_p, rope_cols)

    a_p, kmean_s = _attn_prompt(proj_p, batch, seq, k_col, v_col, cache_k3, page_table)
    zero_state = jnp.zeros((batch, SUBLANES, cw), F32)
    cb_p, utail_p = _conv_seq(proj_p, zero_state, conv_w[0], seq, b_col, cw)
    tiles_per_seq = utail_p.shape[0] // batch
    conv_p = utail_p.reshape(batch, tiles_per_seq, SUBLANES, cw)[:, -1, SUBLANES - (CONV_K - 1):, :]
    sel = _sample_gate(q_s.reshape(db, N_HEADS, HEAD_DIM), kmean_s)
    a_s = _sample_attn(q_s.reshape(db, N_HEADS, HEAD_DIM), k_s.reshape(db, N_KV_HEADS, HEAD_DIM),
                       v_s.reshape(db, N_KV_HEADS, HEAD_DIM), cache_k4, cache_v4, page_table,
                       sel.reshape(db, N_HEADS * MOBA_TOPK)).reshape(db, qw).astype(BF16)
    state = state_conv[0].astype(F32)
    cb_s, u_s = _conv_step(proj_s, state.reshape(db, (CONV_K - 1) * cw), conv_w[0], b_col, cw)
    conv_s = jnp.concatenate([state[:, 1:, :], u_s[:, None, :]], axis=1)

    y_s, trunk_bf16 = _trunk_tail(xs, a_s, cb_s, proj_s, *trunk_f32, ga_col, gc_col)
    y_p, _ = _trunk_tail(xp, a_p, cb_p, proj_p, *trunk_bf16, ga_col, gc_col)

    return (
        y_p.reshape(batch, seq, d),
        y_s.reshape(db, 1, d),
        proj_p[:, k_col:k_col + kvw].reshape(1, batch, seq, N_KV_HEADS, HEAD_DIM),
        proj_p[:, v_col:v_col + kvw].reshape(1, batch, seq, N_KV_HEADS, HEAD_DIM),
        conv_p[None],
        k_s.reshape(1, db, 1, N_KV_HEADS, HEAD_DIM),
        v_s.reshape(1, db, 1, N_KV_HEADS, HEAD_DIM),
        conv_s[None],
    )
```

```python
import functools

import jax
import jax.numpy as jnp
from jax import lax
from jax.experimental import pallas as pl
from jax.experimental.pallas import tpu as pltpu

F32 = jnp.float32
BF16 = jnp.bfloat16

N_HEADS = 16
HEAD_DIM = 128
N_KV_HEADS = 4
KV_GROUP = N_HEADS // N_KV_HEADS
MOBA_BLOCK = 256
MOBA_TOPK = 3
ROPE_THETA = 10000.0
CONV_K = 3
RMS_EPS = 1e-6
ATTN_SCALE = HEAD_DIM ** -0.5
LOG2E = 1.4426950408889634
MASK_BIAS = -1e30

LANES = 128
SUBLANES = 8
MIB = 1 << 20
ROW_TILE = 1024
WIDE_ROW_TILE = 2048
COL_TILE = 512
PAGES_PER_STEP = 32
ROW_GROUP = 256


def _cparams(semantics, vmem_mib):
    return pltpu.CompilerParams(dimension_semantics=semantics, vmem_limit_bytes=vmem_mib * MIB)


def _row_tile(m, cap=ROW_TILE):
    t = min(m, cap)
    assert m % t == 0 and t % SUBLANES == 0, (m, t)
    return t


def _rmsnorm_f32(x, g):
    return x * lax.rsqrt(jnp.mean(x * x, axis=-1, keepdims=True) + RMS_EPS) * g


def _mxu_weight(w_ref, wb_refs):
    if not wb_refs:
        return w_ref[...]
    w = w_ref[...].astype(BF16)
    wb_refs[0][...] = w
    return w


def _weight_out(w, block_shape, index_map):
    if w.dtype == BF16:
        return [], []
    assert w.dtype == F32
    return [pl.BlockSpec(block_shape, index_map)], [jax.ShapeDtypeStruct(w.shape, BF16)]


def _inproj_kernel(x_ref, g_ref, w_ref, cos_ref, sin_ref, o_ref, *rest, rope_tiles):
    *wb_refs, xn_ref = rest
    j = pl.program_id(1)

    @pl.when(j == 0)
    def _():
        xn_ref[...] = _rmsnorm_f32(x_ref[...], g_ref[...]).astype(BF16)

    acc = jnp.dot(xn_ref[...], _mxu_weight(w_ref, wb_refs), preferred_element_type=F32)

    @pl.when(j < rope_tiles)
    def _():
        cos = cos_ref[...]
        sin = sin_ref[...]
        for c in range(acc.shape[1] // HEAD_DIM):
            a = acc[:, c * HEAD_DIM:(c + 1) * HEAD_DIM]
            o_ref[:, c * HEAD_DIM:(c + 1) * HEAD_DIM] = a * cos + pltpu.roll(a, HEAD_DIM // 2, axis=1) * sin

    @pl.when(j >= rope_tiles)
    def _():
        o_ref[...] = acc


def _block_means(get_page, n_pages, page_rows):
    pages_per_block = MOBA_BLOCK * N_KV_HEADS // page_rows
    blocks = n_pages // pages_per_block
    row = lax.broadcasted_iota(jnp.int32, (blocks, HEAD_DIM), 0)
    means = [jnp.zeros((blocks, HEAD_DIM), F32) for _ in range(N_KV_HEADS)]
    for r in range(blocks):
        tot = jnp.zeros((SUBLANES, HEAD_DIM), F32)
        for t in range(pages_per_block):
            x = get_page(r * pages_per_block + t)
            tot = tot + jnp.sum(x.reshape(page_rows // SUBLANES, SUBLANES, HEAD_DIM), axis=0)
        head_sum = tot[0:N_KV_HEADS, :]
        for c in range(1, SUBLANES // N_KV_HEADS):
            head_sum = head_sum + tot[c * N_KV_HEADS:(c + 1) * N_KV_HEADS, :]
        head_mean = head_sum * (1.0 / MOBA_BLOCK)
        for g in range(N_KV_HEADS):
            means[g] = jnp.where(row == r, head_mean[g:g + 1, :], means[g])
    return means


class _KeyStream:
    def __init__(self, pt_ref, kc_ref, km_ref, pbuf, psem, step, chunks_per_seq, n_chunks, every_step):
        self.pt_ref, self.kc_ref, self.km_ref, self.pbuf, self.psem = pt_ref, kc_ref, km_ref, pbuf, psem
        self.step, self.chunks_per_seq, self.n_chunks = step, chunks_per_seq, n_chunks
        self.every_step = every_step
        self.slot = step % 2
        self.pages = pbuf.shape[1]

    def _copies(self, s, sl):
        b, c = s // self.chunks_per_seq, s % self.chunks_per_seq
        return [pltpu.make_async_copy(self.kc_ref.at[self.pt_ref[b, c * self.pages + t]],
                                      self.pbuf.at[sl, t], self.psem.at[sl])
                for t in range(self.pages)]

    def prefetch(self):
        @pl.when(self.step == 0)
        def _():
            for c in self._copies(self.step, self.slot):
                c.start()

        @pl.when(self.step + 1 < self.n_chunks)
        def _():
            for c in self._copies(self.step + 1, 1 - self.slot):
                c.start()

    def reduce(self):
        def body():
            for c in self._copies(self.step, self.slot):
                c.wait()
            means = _block_means(lambda t: self.pbuf[self.slot, t], self.pages, self.pbuf.shape[2])
            for g in range(N_KV_HEADS):
                self.km_ref[0, g] = means[g]

        if self.every_step:
            body()
        else:
            pl.when(self.step < self.n_chunks)(body)


def _key_stream_specs(cache_k3, page_table, n_steps, step_of):
    db, n_pages = page_table.shape
    _, page_rows, hd = cache_k3.shape
    page = page_rows // N_KV_HEADS
    pages = PAGES_PER_STEP
    assert MOBA_BLOCK % page == 0 and n_pages % pages == 0 and SUBLANES % N_KV_HEADS == 0
    blocks = pages * page // MOBA_BLOCK
    assert blocks % SUBLANES == 0 and hd == HEAD_DIM
    chunks_per_seq = n_pages // pages
    n_chunks = db * chunks_per_seq
    nb = n_pages * page // MOBA_BLOCK
    assert n_chunks <= n_steps, "not enough host grid steps to stream the key cache"

    def km_map(*ids_and_pt):
        s = jnp.minimum(step_of(*ids_and_pt[:-1]), n_chunks - 1)
        return (s // chunks_per_seq, 0, s % chunks_per_seq, 0)

    return (dict(chunks_per_seq=chunks_per_seq, n_chunks=n_chunks, every_step=n_chunks == n_steps),
            pl.BlockSpec((1, N_KV_HEADS, blocks, HEAD_DIM), km_map),
            jax.ShapeDtypeStruct((db, N_KV_HEADS, nb, HEAD_DIM), F32),
            [pltpu.VMEM((2, pages, page_rows, HEAD_DIM), F32), pltpu.SemaphoreType.DMA((2,))])


def _inproj(x, ln1, w_in, cos, sin, rope_cols):
    m, d = x.shape
    n = w_in.shape[1]
    tm, tn = _row_tile(m, WIDE_ROW_TILE), COL_TILE
    assert n % tn == 0 and rope_cols % tn == 0
    assert cos.shape[0] % tm == 0 and m % cos.shape[0] == 0
    table_tiles = cos.shape[0] // tm
    wb_specs, wb_shapes = _weight_out(w_in, (d, tn), lambda i, j: (0, j))
    assert not wb_specs or m == tm, "the bf16 weight copy is written once per weight tile"
    out = pl.pallas_call(
        functools.partial(_inproj_kernel, rope_tiles=rope_cols // tn),
        grid=(m // tm, n // tn),
        in_specs=[
            pl.BlockSpec((tm, d), lambda i, j: (i, 0), pipeline_mode=pl.Buffered(1)),
            pl.BlockSpec((1, d), lambda i, j: (0, 0)),
            pl.BlockSpec((d, tn), lambda i, j: (0, j)),
            pl.BlockSpec((tm, HEAD_DIM), lambda i, j: (i % table_tiles, 0)),
            pl.BlockSpec((tm, HEAD_DIM), lambda i, j: (i % table_tiles, 0)),
        ],
        out_specs=[pl.BlockSpec((tm, tn), lambda i, j: (i, j))] + wb_specs,
        out_shape=[jax.ShapeDtypeStruct((m, n), F32)] + wb_shapes,
        scratch_shapes=[pltpu.VMEM((tm, d), BF16)],
        compiler_params=_cparams(("parallel", "arbitrary"), 56),
        name="inproj",
    )(x, ln1.reshape(1, d), w_in, cos, sin)
    return out[0], (out[1] if wb_specs else w_in)


def _attn_prompt_kernel(pt_ref, q_ref, k_ref, v_ref, kc_ref, o_ref, km_ref, ko_ref, vo_ref, ka_ref, va_ref,
                        kmean_ref, qa_ref, m_ref, acc_ref, pbuf, psem, osem, *, chunks_per_seq, n_chunks,
                        every_step):
    qi = pl.program_id(2)
    seq = k_ref.shape[0]
    nb = seq // MOBA_BLOCK
    rows = KV_GROUP * MOBA_BLOCK
    pair = 2 * MOBA_BLOCK
    step = (pl.program_id(0) * pl.num_programs(1) + pl.program_id(1)) * pl.num_programs(2) + qi
    stream = _KeyStream(pt_ref, kc_ref, km_ref, pbuf, psem, step, chunks_per_seq, n_chunks, every_step)
    stream.prefetch()

    def kv_out_copies():
        b, g = pl.program_id(0), pl.program_id(1)
        return [pltpu.make_async_copy(k_ref, ko_ref.at[b, :, g, :], osem.at[0]),
                pltpu.make_async_copy(v_ref, vo_ref.at[b, :, g, :], osem.at[1])]

    @pl.when(qi == 0)
    def _():
        for c in kv_out_copies():
            c.start()

    @pl.when(qi == pl.num_programs(2) - 1)
    def _():
        for c in kv_out_copies():
            c.wait()

    @pl.when(qi == 0)
    def _():
        k = k_ref[...]
        row_blk = lax.broadcasted_iota(jnp.int32, (seq, LANES), 0) // MOBA_BLOCK
        lane = lax.broadcasted_iota(jnp.int32, (seq, LANES), 1)
        ka_ref[:, 0:HEAD_DIM] = k.astype(BF16)
        ka_ref[:, HEAD_DIM:] = jnp.where(lane == row_blk, 1.0, 0.0).astype(BF16)
        va_ref[:, 0:HEAD_DIM] = v_ref[...].astype(BF16)
        va_ref[:, HEAD_DIM:] = jnp.ones((seq, LANES), BF16)
        kmean = jnp.mean(k.reshape(nb, MOBA_BLOCK, HEAD_DIM), axis=1)
        km_hi = kmean.astype(BF16)
        km_lo = (kmean - km_hi.astype(F32)).astype(BF16)
        kmean_ref[...] = jnp.zeros_like(kmean_ref)
        kmean_ref[0:nb, :] = jnp.concatenate([km_hi, km_hi, km_lo], axis=1)

    stream.reduce()

    q = q_ref[...]
    q4 = jnp.concatenate([q[:, h * HEAD_DIM:(h + 1) * HEAD_DIM] for h in range(KV_GROUP)], axis=0)
    qs = (q4 * (ATTN_SCALE * LOG2E)).astype(BF16)

    q_hi = q4.astype(BF16)
    q_lo = (q4 - q_hi.astype(F32)).astype(BF16)
    gate = lax.dot_general(jnp.concatenate([q_hi, q_lo, q_hi], axis=1), kmean_ref[...],
                           (((1,), (1,)), ((), ())), preferred_element_type=F32)
    lane_i = lax.broadcasted_iota(jnp.int32, (rows, LANES), 1)
    lane = lane_i.astype(F32)
    past = lane_i < qi
    gate = jnp.where(past, gate, -jnp.inf)
    bias = jnp.full((rows, LANES), MASK_BIAS, F32)
    for _ in range(MOBA_TOPK):
        top = jnp.max(gate, axis=1, keepdims=True)
        pick = lane == jnp.min(jnp.where(gate == top, lane, float(LANES)), axis=1, keepdims=True)
        bias = jnp.where(pick & past, 0.0, bias)
        gate = jnp.where(pick, -jnp.inf, gate)
    qa = jnp.concatenate([qs, bias.astype(BF16)], axis=1)

    own = pl.multiple_of(qi * MOBA_BLOCK, MOBA_BLOCK)
    s = lax.dot_general(qs, ka_ref[pl.ds(own, MOBA_BLOCK), 0:HEAD_DIM], (((1,), (1,)), ((), ())),
                        preferred_element_type=F32)
    qrow = lax.broadcasted_iota(jnp.int32, (rows, MOBA_BLOCK), 0) & (MOBA_BLOCK - 1)
    kcol = lax.broadcasted_iota(jnp.int32, (rows, MOBA_BLOCK), 1)
    s = jnp.where(kcol <= qrow, s, MASK_BIAS)
    m0 = jnp.broadcast_to(jnp.max(s, axis=1, keepdims=True), (rows, LANES))
    p = jnp.concatenate([jnp.exp2(s[:, c * LANES:(c + 1) * LANES] - m0)
                         for c in range(MOBA_BLOCK // LANES)], axis=1)
    m_ref[...] = m0
    acc_ref[...] = jnp.dot(p.astype(BF16), va_ref[pl.ds(own, MOBA_BLOCK), :], preferred_element_type=F32)

    qa_ref[...] = qa

    def past_pair(t, carry):
        start = pl.multiple_of(t * pair, pair)
        kt = ka_ref[pl.ds(start, pair), :]
        vt = va_ref[pl.ds(start, pair), :]
        groups = [slice(r0, r0 + ROW_GROUP) for r0 in range(0, rows, ROW_GROUP)]

        def scores(rs):
            return lax.dot_general(qa_ref[rs, :], kt, (((1,), (1,)), ((), ())), preferred_element_type=F32)

        m_olds = [m_ref[rs, :] for rs in groups]
        s_next = scores(groups[0])
        updates = []
        for gi, rs in enumerate(groups):
            s = s_next
            if gi + 1 < len(groups):
                s_next = scores(groups[gi + 1])
            m_old = m_olds[gi]
            m_new = jnp.maximum(m_old, jnp.max(s, axis=1, keepdims=True))
            alpha = jnp.exp2(m_old - m_new)
            p = jnp.concatenate([jnp.exp2(s[:, c * LANES:(c + 1) * LANES] - m_new)
                                 for c in range(pair // LANES)], axis=1)
            pv = jnp.dot(p.astype(BF16), vt, preferred_element_type=F32)
            updates.append((rs, m_new, alpha, pv))
        for rs, m_new, alpha, pv in updates:
            for c in range(2):
                cs = slice(c * LANES, (c + 1) * LANES)
                acc_ref[rs, cs] = alpha * acc_ref[rs, cs] + pv[:, cs]
            m_ref[rs, :] = m_new
        return carry

    lax.fori_loop(0, (qi + 1) // 2, past_pair, 0)

    o = acc_ref[:, 0:HEAD_DIM] / acc_ref[:, HEAD_DIM:]
    for h in range(KV_GROUP):
        o_ref[:, h * HEAD_DIM:(h + 1) * HEAD_DIM] = o[h * MOBA_BLOCK:(h + 1) * MOBA_BLOCK].astype(o_ref.dtype)


def _attn_prompt(proj, batch, seq, k_col, v_col, cache_k3, page_table):
    nq = seq // MOBA_BLOCK
    assert seq % (2 * MOBA_BLOCK) == 0 and MOBA_TOPK <= nq <= LANES
    gw = KV_GROUP * HEAD_DIM
    rows = KV_GROUP * MOBA_BLOCK
    kblk, vblk = k_col // HEAD_DIM, v_col // HEAD_DIM
    grid = (batch, N_KV_HEADS, nq)
    stream_kwargs, km_spec, km_shape, stream_scratch = _key_stream_specs(
        cache_k3, page_table, batch * N_KV_HEADS * nq, lambda b, g, i: (b * N_KV_HEADS + g) * nq + i)
    grid_spec = pltpu.PrefetchScalarGridSpec(
        num_scalar_prefetch=1,
        grid=grid,
        in_specs=[
            pl.BlockSpec((MOBA_BLOCK, gw), lambda b, g, i, pt: (b * nq + i, g)),
            pl.BlockSpec((seq, HEAD_DIM), lambda b, g, i, pt: (b, kblk + g)),
            pl.BlockSpec((seq, HEAD_DIM), lambda b, g, i, pt: (b, vblk + g)),
            pl.BlockSpec(memory_space=pl.ANY),
        ],
        out_specs=[pl.BlockSpec((MOBA_BLOCK, gw), lambda b, g, i, pt: (b * nq + i, g)), km_spec,
                   pl.BlockSpec(memory_space=pl.ANY), pl.BlockSpec(memory_space=pl.ANY)],
        scratch_shapes=[
            pltpu.VMEM((seq, HEAD_DIM + LANES), BF16),
            pltpu.VMEM((seq, HEAD_DIM + LANES), BF16),
            pltpu.VMEM((LANES, 3 * HEAD_DIM), BF16),
            pltpu.VMEM((rows, HEAD_DIM + LANES), BF16),
            pltpu.VMEM((rows, LANES), F32),
            pltpu.VMEM((rows, HEAD_DIM + LANES), F32),
        ] + stream_scratch + [pltpu.SemaphoreType.DMA((2,))],
    )
    kv_shape = jax.ShapeDtypeStruct((batch, seq, N_KV_HEADS, HEAD_DIM), proj.dtype)
    return pl.pallas_call(
        functools.partial(_attn_prompt_kernel, **stream_kwargs),
        grid_spec=grid_spec,
        out_shape=[jax.ShapeDtypeStruct((batch * seq, N_HEADS * HEAD_DIM), BF16), km_shape, kv_shape, kv_shape],
        compiler_params=_cparams(("arbitrary", "arbitrary", "arbitrary"), 48),
        name="attn_prompt",
    )(page_table, proj, proj, proj, cache_k3)


def _conv_seq_kernel(b_ref, c_ref, h_ref, cp_ref, hp_ref, st_ref, w_ref, cb_ref, ut_ref, *, seq):
    i = pl.program_id(0)
    tm = c_ref.shape[0]
    u = c_ref[...] * h_ref[...]
    prev = jnp.where((i * tm) % seq == 0, st_ref[0], cp_ref[...] * hp_ref[...])
    p1 = prev[SUBLANES - 1:SUBLANES, :]
    p2 = prev[SUBLANES - 2:SUBLANES - 1, :]
    row = lax.broadcasted_iota(jnp.int32, u.shape, 0)
    u1 = jnp.where(row == 0, p1, pltpu.roll(u, 1, axis=0))
    u2 = jnp.where(row == 0, p2, jnp.where(row == 1, p1, pltpu.roll(u, 2, axis=0)))
    w = w_ref[...]
    conv = w[0:1, :] * u2 + w[1:2, :] * u1 + w[2:3, :] * u
    cb_ref[...] = (b_ref[...] * conv).astype(BF16)
    ut_ref[0] = u[tm - SUBLANES:tm, :]


def _conv_seq(proj, state8, conv_w, seq, b_col, cw):
    m = proj.shape[0]
    tm = _row_tile(seq, 512)
    nt = m // tm
    cb = b_col // cw
    pstep = tm // SUBLANES
    return pl.pallas_call(
        functools.partial(_conv_seq_kernel, seq=seq),
        grid=(nt,),
        in_specs=[
            pl.BlockSpec((tm, cw), lambda i: (i, cb)),
            pl.BlockSpec((tm, cw), lambda i: (i, cb + 1)),
            pl.BlockSpec((tm, cw), lambda i: (i, cb + 2)),
            pl.BlockSpec((SUBLANES, cw), lambda i: (jnp.maximum(i * pstep - 1, 0), cb + 1)),
            pl.BlockSpec((SUBLANES, cw), lambda i: (jnp.maximum(i * pstep - 1, 0), cb + 2)),
            pl.BlockSpec((1, SUBLANES, cw), lambda i: ((i * tm) // seq, 0, 0)),
            pl.BlockSpec((CONV_K, cw), lambda i: (0, 0)),
        ],
        out_specs=[
            pl.BlockSpec((tm, cw), lambda i: (i, 0)),
            pl.BlockSpec((1, SUBLANES, cw), lambda i: (i, 0, 0)),
        ],
        out_shape=[
            jax.ShapeDtypeStruct((m, cw), BF16),
            jax.ShapeDtypeStruct((nt, SUBLANES, cw), F32),
        ],
        compiler_params=_cparams(("parallel",), 32),
        name="conv_seq",
    )(proj, proj, proj, proj, proj, state8, conv_w)


def _conv_step_kernel(b_ref, c_ref, h_ref, st_ref, w_ref, cb_ref, u_ref):
    cw = c_ref.shape[1]
    u = c_ref[...] * h_ref[...]
    w = w_ref[...]
    conv = w[0:1, :] * st_ref[:, 0:cw] + w[1:2, :] * st_ref[:, cw:2 * cw] + w[2:3, :] * u
    cb_ref[...] = (b_ref[...] * conv).astype(BF16)
    u_ref[...] = u


def _conv_step(proj, state, conv_w, b_col, cw):
    m = proj.shape[0]
    cb = b_col // cw
    return pl.pallas_call(
        _conv_step_kernel,
        grid=(1,),
        in_specs=[
            pl.BlockSpec((m, cw), lambda i: (0, cb)),
            pl.BlockSpec((m, cw), lambda i: (0, cb + 1)),
            pl.BlockSpec((m, cw), lambda i: (0, cb + 2)),
            pl.BlockSpec((m, (CONV_K - 1) * cw), lambda i: (0, 0)),
            pl.BlockSpec((CONV_K, cw), lambda i: (0, 0)),
        ],
        out_specs=[pl.BlockSpec((m, cw), lambda i: (0, 0)), pl.BlockSpec((m, cw), lambda i: (0, 0))],
        out_shape=[jax.ShapeDtypeStruct((m, cw), BF16), jax.ShapeDtypeStruct((m, cw), F32)],
        compiler_params=_cparams(("arbitrary",), 32),
        name="conv_step",
    )(proj, proj, proj, state, conv_w)


def _merge_kernel(a_ref, cb_ref, ga_ref, gc_ref, wa_ref, wc_ref, o_ref, *wb_refs):
    ya = jnp.dot(a_ref[...], _mxu_weight(wa_ref, wb_refs[0:1]), preferred_element_type=F32)
    yc = jnp.dot(cb_ref[...], _mxu_weight(wc_ref, wb_refs[1:2]), preferred_element_type=F32)
    o_ref[...] = (jax.nn.sigmoid(ga_ref[...]) * ya + jax.nn.sigmoid(gc_ref[...]) * yc).astype(o_ref.dtype)


def _merge(a, cb, proj, w_attn_br, w_conv_br, ga_col, gc_col):
    m, qw = a.shape
    cw = cb.shape[1]
    d = w_attn_br.shape[1]
    tm, tn = _row_tile(m), COL_TILE
    ga, gc = ga_col // tn, gc_col // tn
    assert w_attn_br.dtype == w_conv_br.dtype
    wa_specs, wa_shapes = _weight_out(w_attn_br, (qw, tn), lambda i, j: (0, j))
    wc_specs, wc_shapes = _weight_out(w_conv_br, (cw, tn), lambda i, j: (0, j))
    assert not wa_specs or m == tm, "the bf16 weight copy is written once per weight tile"
    out = pl.pallas_call(
        _merge_kernel,
        grid=(m // tm, d // tn),
        in_specs=[
            pl.BlockSpec((tm, qw), lambda i, j: (i, 0)),
            pl.BlockSpec((tm, cw), lambda i, j: (i, 0)),
            pl.BlockSpec((tm, tn), lambda i, j: (i, ga + j)),
            pl.BlockSpec((tm, tn), lambda i, j: (i, gc + j)),
            pl.BlockSpec((qw, tn), lambda i, j: (0, j)),
            pl.BlockSpec((cw, tn), lambda i, j: (0, j)),
        ],
        out_specs=[pl.BlockSpec((tm, tn), lambda i, j: (i, j))] + wa_specs + wc_specs,
        out_shape=[jax.ShapeDtypeStruct((m, d), BF16)] + wa_shapes + wc_shapes,
        compiler_params=_cparams(("parallel", "parallel"), 56),
        name="merge",
    )(a, cb, proj, proj, w_attn_br, w_conv_br)
    return out[0], ((out[1], out[2]) if wa_specs else (w_attn_br, w_conv_br))


def _outproj_kernel(m_ref, w_ref, x_ref, o_ref, *wb_refs):
    o_ref[...] = x_ref[...] + jnp.dot(m_ref[...], _mxu_weight(w_ref, wb_refs), preferred_element_type=F32)


def _outproj(merged, w_o, x):
    m, d = merged.shape
    n = w_o.shape[1]
    tm, tn = _row_tile(m, WIDE_ROW_TILE), COL_TILE
    wb_specs, wb_shapes = _weight_out(w_o, (d, tn), lambda i, j: (0, j))
    assert not wb_specs or m == tm, "the bf16 weight copy is written once per weight tile"
    out = pl.pallas_call(
        _outproj_kernel,
        grid=(m // tm, n // tn),
        in_specs=[
            pl.BlockSpec((tm, d), lambda i, j: (i, 0), pipeline_mode=pl.Buffered(1)),
            pl.BlockSpec((d, tn), lambda i, j: (0, j)),
            pl.BlockSpec((tm, tn), lambda i, j: (i, j)),
        ],
        out_specs=[pl.BlockSpec((tm, tn), lambda i, j: (i, j))] + wb_specs,
        out_shape=[jax.ShapeDtypeStruct((m, n), F32)] + wb_shapes,
        compiler_params=_cparams(("parallel", "parallel"), 48),
        name="outproj",
    )(merged, w_o, x)
    return out[0], (out[1] if wb_specs else w_o)


def _ffn_up_kernel(x_ref, g_ref, wg_ref, wu_ref, o_ref, *rest):
    *wb_refs, hn_ref = rest

    @pl.when(pl.program_id(1) == 0)
    def _():
        hn_ref[...] = _rmsnorm_f32(x_ref[...], g_ref[...]).astype(BF16)

    hn = hn_ref[...]
    gate = jnp.dot(hn, _mxu_weight(wg_ref, wb_refs[0:1]), preferred_element_type=F32)
    up = jnp.dot(hn, _mxu_weight(wu_ref, wb_refs[1:2]), preferred_element_type=F32)
    o_ref[...] = (gate * jax.nn.sigmoid(gate) * up).astype(o_ref.dtype)


def _ffn_up(x, ln2, w_gate, w_up):
    m, d = x.shape
    f = w_gate.shape[1]
    tm, tn = _row_tile(m), COL_TILE
    assert f % tn == 0 and w_gate.dtype == w_up.dtype
    wg_specs, wg_shapes = _weight_out(w_gate, (d, tn), lambda i, j: (0, j))
    wu_specs, wu_shapes = _weight_out(w_up, (d, tn), lambda i, j: (0, j))
    assert not wg_specs or m == tm, "the bf16 weight copy is written once per weight tile"
    out = pl.pallas_call(
        _ffn_up_kernel,
        grid=(m // tm, f // tn),
        in_specs=[
            pl.BlockSpec((tm, d), lambda i, j: (i, 0)),
            pl.BlockSpec((1, d), lambda i, j: (0, 0)),
            pl.BlockSpec((d, tn), lambda i, j: (0, j)),
            pl.BlockSpec((d, tn), lambda i, j: (0, j)),
        ],
        out_specs=[pl.BlockSpec((tm, tn), lambda i, j: (i, j))] + wg_specs + wu_specs,
        out_shape=[jax.ShapeDtypeStruct((m, f), BF16)] + wg_shapes + wu_shapes,
        scratch_shapes=[pltpu.VMEM((tm, d), BF16)],
        compiler_params=_cparams(("parallel", "arbitrary"), 56),
        name="ffn_up",
    )(x, ln2.reshape(1, d), w_gate, w_up)
    return out[0], ((out[1], out[2]) if wg_specs else (w_gate, w_up))


def _ffn_down_kernel(h_ref, w_ref, x_ref, g_ref, o_ref, *wb_refs):
    k = pl.program_id(1)

    @pl.when(k == 0)
    def _():
        o_ref[...] = x_ref[...]

    o_ref[...] += jnp.dot(h_ref[...], _mxu_weight(w_ref, wb_refs), preferred_element_type=F32)

    @pl.when(k == pl.num_programs(1) - 1)
    def _():
        o_ref[...] = _rmsnorm_f32(o_ref[...], g_ref[...])


def _ffn_down(hmid, w_down, x, ln_f):
    m, f = hmid.shape
    d = w_down.shape[1]
    tm, tk = _row_tile(m), COL_TILE
    wb_specs, wb_shapes = _weight_out(w_down, (tk, d), lambda i, k: (k, 0))
    assert not wb_specs or m == tm, "the bf16 weight copy is written once per weight tile"
    out = pl.pallas_call(
        _ffn_down_kernel,
        grid=(m // tm, f // tk),
        in_specs=[
            pl.BlockSpec((tm, tk), lambda i, k: (i, k)),
            pl.BlockSpec((tk, d), lambda i, k: (k, 0)),
            pl.BlockSpec((tm, d), lambda i, k: (i, 0)),
            pl.BlockSpec((1, d), lambda i, k: (0, 0)),
        ],
        out_specs=[pl.BlockSpec((tm, d), lambda i, k: (i, 0))] + wb_specs,
        out_shape=[jax.ShapeDtypeStruct((m, d), F32)] + wb_shapes,
        compiler_params=_cparams(("parallel", "arbitrary"), 48),
        name="ffn_down",
    )(hmid, w_down, x, ln_f.reshape(1, d))
    return out[0], (out[1] if wb_specs else w_down)


def _sample_gate_kernel(q_ref, km_ref, sel_ref):
    q = q_ref[0]
    nb = km_ref.shape[2]
    head_group = lax.broadcasted_iota(jnp.int32, (N_HEADS, nb), 0) // KV_GROUP
    gate = jnp.zeros((N_HEADS, nb), F32)
    for g in range(N_KV_HEADS):
        gg = lax.dot_general(q, km_ref[0, g], (((1,), (1,)), ((), ())),
                             precision=lax.Precision.HIGHEST, preferred_element_type=F32)
        gate = jnp.where(head_group == g, gg, gate)
    blk = lax.broadcasted_iota(jnp.int32, (N_HEADS, nb), 1).astype(F32)
    lane = lax.broadcasted_iota(jnp.int32, (N_HEADS, LANES), 1)
    out = jnp.zeros((N_HEADS, LANES), F32)
    for r in range(MOBA_TOPK):
        top = jnp.max(gate, axis=1, keepdims=True)
        first = jnp.min(jnp.where(gate == top, blk, float(nb)), axis=1, keepdims=True)
        out = jnp.where(lane == r, first, out)
        gate = jnp.where(blk == first, -jnp.inf, gate)
    sel_ref[0] = out.astype(jnp.int32)


def _sample_gate(q_s, kmean):
    db, _, nb, _ = kmean.shape
    assert MOBA_TOPK <= nb
    sel = pl.pallas_call(
        _sample_gate_kernel,
        grid=(db,),
        in_specs=[pl.BlockSpec((1, N_HEADS, HEAD_DIM), lambda b: (b, 0, 0)),
                  pl.BlockSpec((1, N_KV_HEADS, nb, HEAD_DIM), lambda b: (b, 0, 0, 0))],
        out_specs=pl.BlockSpec((1, N_HEADS, LANES), lambda b: (b, 0, 0)),
        out_shape=jax.ShapeDtypeStruct((db, N_HEADS, LANES), jnp.int32),
        compiler_params=_cparams(("parallel",), 32),
        name="sample_gate",
    )(q_s, kmean)
    return sel[:, :, :MOBA_TOPK]


def _sample_attn_kernel(pt_ref, sel_ref, q_ref, ks_ref, vs_ref, kc_ref, vc_ref, o_ref, kbuf, vbuf, sem, *,
                        pages_per_block):
    b = pl.program_id(0)
    slot = b % 2
    _, n_heads, n_sel, page, hd = kbuf.shape

    def head_copies(bb, h, sl):
        g = h // KV_GROUP
        out = []
        for t in range(n_sel):
            blk = sel_ref[bb, h * MOBA_TOPK + t // pages_per_block]
            pg = pt_ref[bb, blk * pages_per_block + t % pages_per_block]
            out.append(pltpu.make_async_copy(kc_ref.at[pg, :, g, :], kbuf.at[sl, h, t], sem.at[0, sl, h]))
            out.append(pltpu.make_async_copy(vc_ref.at[pg, :, g, :], vbuf.at[sl, h, t], sem.at[1, sl, h]))
        return out

    def start_all(bb, sl):
        def start_head(h, carry):
            for c in head_copies(bb, h, sl):
                c.start()
            return carry
        lax.fori_loop(0, n_heads, start_head, 0)

    @pl.when(b == 0)
    def _():
        start_all(b, slot)

    @pl.when(b + 1 < pl.num_programs(0))
    def _():
        start_all(b + 1, 1 - slot)

    ones = jnp.ones((2 * hd, LANES), BF16)

    def one_head(h):
        g = h // KV_GROUP
        q = q_ref[0, pl.ds(h, 1), :] * ATTN_SCALE
        prod = kbuf[slot, h].reshape(n_sel * page, hd) * q
        hi = prod.astype(BF16)
        lo = (prod - hi.astype(F32)).astype(BF16)
        s = jnp.dot(jnp.concatenate([hi, lo], axis=1), ones, preferred_element_type=F32)
        s_new = jnp.sum(ks_ref[0, pl.ds(g, 1), :] * q, axis=1, keepdims=True)
        m = jnp.maximum(jnp.max(s, axis=0, keepdims=True), s_new)
        p = jnp.exp(s - m)
        p_new = jnp.exp(s_new - m)
        l = jnp.sum(p, axis=0, keepdims=True) + p_new
        o = jnp.sum(p * vbuf[slot, h].reshape(n_sel * page, hd), axis=0, keepdims=True)
        o = o + p_new * vs_ref[0, pl.ds(g, 1), :]
        o_ref[0, pl.ds(h, 1), :] = (o / l).astype(o_ref.dtype)

    def head_group(g, carry):
        for j in range(KV_GROUP):
            for c in head_copies(b, g * KV_GROUP + j, slot):
                c.wait()
        for j in range(KV_GROUP):
            one_head(g * KV_GROUP + j)
        return carry

    lax.fori_loop(0, n_heads // KV_GROUP, head_group, 0)


def _sample_attn(q_s, k_s, v_s, cache_k4, cache_v4, page_table, sel):
    db, n_pages = page_table.shape
    page = cache_k4.shape[1]
    ppb = MOBA_BLOCK // page
    n_sel = MOBA_TOPK * ppb
    assert HEAD_DIM == LANES

    grid_spec = pltpu.PrefetchScalarGridSpec(
        num_scalar_prefetch=2,
        grid=(db,),
        in_specs=[
            pl.BlockSpec((1, N_HEADS, HEAD_DIM), lambda b, pt, sl: (b, 0, 0)),
            pl.BlockSpec((1, N_KV_HEADS, HEAD_DIM), lambda b, pt, sl: (b, 0, 0)),
            pl.BlockSpec((1, N_KV_HEADS, HEAD_DIM), lambda b, pt, sl: (b, 0, 0)),
            pl.BlockSpec(memory_space=pl.ANY),
            pl.BlockSpec(memory_space=pl.ANY),
        ],
        out_specs=pl.BlockSpec((1, N_HEADS, HEAD_DIM), lambda b, pt, sl: (b, 0, 0)),
        scratch_shapes=[
            pltpu.VMEM((2, N_HEADS, n_sel, page, HEAD_DIM), F32),
            pltpu.VMEM((2, N_HEADS, n_sel, page, HEAD_DIM), F32),
            pltpu.SemaphoreType.DMA((2, 2, N_HEADS)),
        ],
    )
    return pl.pallas_call(
        functools.partial(_sample_attn_kernel, pages_per_block=ppb),
        grid_spec=grid_spec,
        out_shape=jax.ShapeDtypeStruct((db, N_HEADS, HEAD_DIM), F32),
        compiler_params=_cparams(("arbitrary",), 40),
        name="sample_attn",
    )(page_table, sel, q_s, k_s, v_s, cache_k4, cache_v4)


def _rope_tables(pos):
    half = HEAD_DIM // 2
    inv = 1.0 / (ROPE_THETA ** (jnp.arange(half, dtype=F32) / half))
    ang = pos.astype(F32)[:, None] * inv[None, :]
    cos, sin = jnp.cos(ang), jnp.sin(ang)
    return jnp.concatenate([cos, cos], axis=-1), jnp.concatenate([-sin, sin], axis=-1)


def _trunk_tail(x, a, cb, proj, w_attn_br, w_conv_br, w_o, ln2, w_ff_gate, w_ff_up, w_ff_down, ln_f,
                ga_col, gc_col):
    merged, (w_attn_br, w_conv_br) = _merge(a, cb, proj, w_attn_br, w_conv_br, ga_col, gc_col)
    x1, w_o = _outproj(merged, w_o, x)
    hmid, (w_ff_gate, w_ff_up) = _ffn_up(x1, ln2, w_ff_gate, w_ff_up)
    y, w_ff_down = _ffn_down(hmid, w_ff_down, x1, ln_f)
    return y, (w_attn_br, w_conv_br, w_o, ln2, w_ff_gate, w_ff_up, w_ff_down, ln_f)


def kernel(x_prompt, x_sample, cache_k, cache_v, state_conv, page_table, ln1, w_in, conv_w, w_attn_br,
           w_conv_br, w_o, ln2, w_ff_gate, w_ff_up, w_ff_down, ln_f):
    batch, seq, d = x_prompt.shape
    db, dec_seq, _ = x_sample.shape
    depth, n_phys, page, kvh, hd = cache_k.shape
    assert depth == 1 and dec_seq == 1 and kvh == N_KV_HEADS and hd == HEAD_DIM
    qw, kvw = N_HEADS * HEAD_DIM, N_KV_HEADS * HEAD_DIM
    cw = conv_w.shape[-1]
    k_col, v_col, b_col = qw, qw + kvw, qw + 2 * kvw
    ga_col = b_col + 3 * cw
    gc_col = ga_col + d
    rope_cols = qw + kvw
    past = page_table.shape[1] * page
    trunk_f32 = (w_attn_br[0], w_conv_br[0], w_o[0], ln2[0], w_ff_gate[0], w_ff_up[0], w_ff_down[0], ln_f)
    cache_k3 = cache_k.reshape(n_phys, page * N_KV_HEADS, HEAD_DIM)
    cache_k4 = cache_k.reshape(n_phys, page, N_KV_HEADS, HEAD_DIM)
    cache_v4 = cache_v.reshape(n_phys, page, N_KV_HEADS, HEAD_DIM)

    xs = x_sample.reshape(db, d)
    cos_s, sin_s = _rope_tables(jnp.full((db,), past, jnp.int32))
    proj_s, w_in_b = _inproj(xs, ln1[0], w_in[0], cos_s, sin_s, rope_cols)
    q_s = proj_s[:, :qw]
    k_s = proj_s[:, k_col:k_col + kvw]
    v_s = proj_s[:, v_col:v_col + kvw]
    mp = batch * seq
    xp = x_prompt.reshape(mp, d)
    cos_p, sin_p = _rope_tables(jnp.arange(seq, dtype=jnp.int32))
    proj_p, _ = _inproj(xp, ln1[0], w_in_b, cos_p, sin_p, rope_cols)

    a_p, kmean_s, k_p, v_p = _attn_prompt(proj_p, batch, seq, k_col, v_col, cache_k3, page_table)
    zero_state = jnp.zeros((batch, SUBLANES, cw), F32)
    cb_p, utail_p = _conv_seq(proj_p, zero_state, conv_w[0], seq, b_col, cw)
    tiles_per_seq = utail_p.shape[0] // batch
    conv_p = utail_p.reshape(batch, tiles_per_seq, SUBLANES, cw)[:, -1, SUBLANES - (CONV_K - 1):, :]
    sel = _sample_gate(q_s.reshape(db, N_HEADS, HEAD_DIM), kmean_s)
    a_s = _sample_attn(q_s.reshape(db, N_HEADS, HEAD_DIM), k_s.reshape(db, N_KV_HEADS, HEAD_DIM),
                       v_s.reshape(db, N_KV_HEADS, HEAD_DIM), cache_k4, cache_v4, page_table,
                       sel.reshape(db, N_HEADS * MOBA_TOPK)).reshape(db, qw).astype(BF16)
    state = state_conv[0].astype(F32)
    cb_s, u_s = _conv_step(proj_s, state.reshape(db, (CONV_K - 1) * cw), conv_w[0], b_col, cw)
    conv_s = jnp.concatenate([state[:, 1:, :], u_s[:, None, :]], axis=1)

    y_s, trunk_bf16 = _trunk_tail(xs, a_s, cb_s, proj_s, *trunk_f32, ga_col, gc_col)
    y_p, _ = _trunk_tail(xp, a_p, cb_p, proj_p, *trunk_bf16, ga_col, gc_col)

    return (
        y_p.reshape(batch, seq, d),
        y_s.reshape(db, 1, d),
        k_p[None],
        v_p[None],
        conv_p[None],
        k_s.reshape(1, db, 1, N_KV_HEADS, HEAD_DIM),
        v_s.reshape(1, db, 1, N_KV_HEADS, HEAD_DIM),
        conv_s[None],
    )
```

```python
import functools

import jax
import jax.numpy as jnp
from jax import lax
from jax.experimental import pallas as pl
from jax.experimental.pallas import tpu as pltpu

F32 = jnp.float32
BF16 = jnp.bfloat16

N_HEADS = 16
HEAD_DIM = 128
N_KV_HEADS = 4
KV_GROUP = N_HEADS // N_KV_HEADS
MOBA_BLOCK = 256
MOBA_TOPK = 3
ROPE_THETA = 10000.0
CONV_K = 3
RMS_EPS = 1e-6
ATTN_SCALE = HEAD_DIM ** -0.5
LOG2E = 1.4426950408889634
MASK_BIAS = -1e30

LANES = 128
SUBLANES = 8
MIB = 1 << 20
ROW_TILE = 1024
WIDE_ROW_TILE = 2048
COL_TILE = 512
PAGES_PER_STEP = 32
ROW_GROUP = 256


def _cparams(semantics, vmem_mib):
    return pltpu.CompilerParams(dimension_semantics=semantics, vmem_limit_bytes=vmem_mib * MIB)


def _row_tile(m, cap=ROW_TILE):
    t = min(m, cap)
    assert m % t == 0 and t % SUBLANES == 0, (m, t)
    return t


def _rmsnorm_f32(x, g):
    return x * lax.rsqrt(jnp.mean(x * x, axis=-1, keepdims=True) + RMS_EPS) * g


def _mxu_weight(w_ref, wb_refs):
    if not wb_refs:
        return w_ref[...]
    w = w_ref[...].astype(BF16)
    wb_refs[0][...] = w
    return w


def _weight_out(w, block_shape, index_map):
    if w.dtype == BF16:
        return [], []
    assert w.dtype == F32
    return [pl.BlockSpec(block_shape, index_map)], [jax.ShapeDtypeStruct(w.shape, BF16)]


def _inproj_kernel(x_ref, g_ref, w_ref, cos_ref, sin_ref, o_ref, *rest, rope_tiles):
    *wb_refs, xn_ref = rest
    j = pl.program_id(1)

    @pl.when(j == 0)
    def _():
        xn_ref[...] = _rmsnorm_f32(x_ref[...], g_ref[...]).astype(BF16)

    acc = jnp.dot(xn_ref[...], _mxu_weight(w_ref, wb_refs), preferred_element_type=F32)

    @pl.when(j < rope_tiles)
    def _():
        cos = cos_ref[...]
        sin = sin_ref[...]
        for c in range(acc.shape[1] // HEAD_DIM):
            a = acc[:, c * HEAD_DIM:(c + 1) * HEAD_DIM]
            o_ref[:, c * HEAD_DIM:(c + 1) * HEAD_DIM] = a * cos + pltpu.roll(a, HEAD_DIM // 2, axis=1) * sin

    @pl.when(j >= rope_tiles)
    def _():
        o_ref[...] = acc


def _block_means(get_page, n_pages, page_rows):
    pages_per_block = MOBA_BLOCK * N_KV_HEADS // page_rows
    blocks = n_pages // pages_per_block
    row = lax.broadcasted_iota(jnp.int32, (blocks, HEAD_DIM), 0)
    means = [jnp.zeros((blocks, HEAD_DIM), F32) for _ in range(N_KV_HEADS)]
    for r in range(blocks):
        tot = jnp.zeros((SUBLANES, HEAD_DIM), F32)
        for t in range(pages_per_block):
            x = get_page(r * pages_per_block + t)
            tot = tot + jnp.sum(x.reshape(page_rows // SUBLANES, SUBLANES, HEAD_DIM), axis=0)
        head_sum = tot[0:N_KV_HEADS, :]
        for c in range(1, SUBLANES // N_KV_HEADS):
            head_sum = head_sum + tot[c * N_KV_HEADS:(c + 1) * N_KV_HEADS, :]
        head_mean = head_sum * (1.0 / MOBA_BLOCK)
        for g in range(N_KV_HEADS):
            means[g] = jnp.where(row == r, head_mean[g:g + 1, :], means[g])
    return means


class _KeyStream:
    def __init__(self, pt_ref, kc_ref, km_ref, pbuf, psem, step, chunks_per_seq, n_chunks, every_step):
        self.pt_ref, self.kc_ref, self.km_ref, self.pbuf, self.psem = pt_ref, kc_ref, km_ref, pbuf, psem
        self.step, self.chunks_per_seq, self.n_chunks = step, chunks_per_seq, n_chunks
        self.every_step = every_step
        self.slot = step % 2
        self.pages = pbuf.shape[1]

    def _copies(self, s, sl):
        b, c = s // self.chunks_per_seq, s % self.chunks_per_seq
        return [pltpu.make_async_copy(self.kc_ref.at[self.pt_ref[b, c * self.pages + t]],
                                      self.pbuf.at[sl, t], self.psem.at[sl])
                for t in range(self.pages)]

    def prefetch(self):
        @pl.when(self.step == 0)
        def _():
            for c in self._copies(self.step, self.slot):
                c.start()

        @pl.when(self.step + 1 < self.n_chunks)
        def _():
            for c in self._copies(self.step + 1, 1 - self.slot):
                c.start()

    def reduce(self):
        def body():
            for c in self._copies(self.step, self.slot):
                c.wait()
            means = _block_means(lambda t: self.pbuf[self.slot, t], self.pages, self.pbuf.shape[2])
            for g in range(N_KV_HEADS):
                self.km_ref[0, g] = means[g]

        if self.every_step:
            body()
        else:
            pl.when(self.step < self.n_chunks)(body)


def _key_stream_specs(cache_k3, page_table, n_steps, step_of):
    db, n_pages = page_table.shape
    _, page_rows, hd = cache_k3.shape
    page = page_rows // N_KV_HEADS
    pages = PAGES_PER_STEP
    assert MOBA_BLOCK % page == 0 and n_pages % pages == 0 and SUBLANES % N_KV_HEADS == 0
    blocks = pages * page // MOBA_BLOCK
    assert blocks % SUBLANES == 0 and hd == HEAD_DIM
    chunks_per_seq = n_pages // pages
    n_chunks = db * chunks_per_seq
    nb = n_pages * page // MOBA_BLOCK
    assert n_chunks <= n_steps, "not enough host grid steps to stream the key cache"

    def km_map(*ids_and_pt):
        s = jnp.minimum(step_of(*ids_and_pt[:-1]), n_chunks - 1)
        return (s // chunks_per_seq, 0, s % chunks_per_seq, 0)

    return (dict(chunks_per_seq=chunks_per_seq, n_chunks=n_chunks, every_step=n_chunks == n_steps),
            pl.BlockSpec((1, N_KV_HEADS, blocks, HEAD_DIM), km_map),
            jax.ShapeDtypeStruct((db, N_KV_HEADS, nb, HEAD_DIM), F32),
            [pltpu.VMEM((2, pages, page_rows, HEAD_DIM), F32), pltpu.SemaphoreType.DMA((2,))])


def _inproj(x, ln1, w_in, cos, sin, rope_cols):
    m, d = x.shape
    n = w_in.shape[1]
    tm, tn = _row_tile(m, WIDE_ROW_TILE), COL_TILE
    assert n % tn == 0 and rope_cols % tn == 0
    assert cos.shape[0] % tm == 0 and m % cos.shape[0] == 0
    table_tiles = cos.shape[0] // tm
    wb_specs, wb_shapes = _weight_out(w_in, (d, tn), lambda i, j: (0, j))
    assert not wb_specs or m == tm, "the bf16 weight copy is written once per weight tile"
    out = pl.pallas_call(
        functools.partial(_inproj_kernel, rope_tiles=rope_cols // tn),
        grid=(m // tm, n // tn),
        in_specs=[
            pl.BlockSpec((tm, d), lambda i, j: (i, 0), pipeline_mode=pl.Buffered(1)),
            pl.BlockSpec((1, d), lambda i, j: (0, 0)),
            pl.BlockSpec((d, tn), lambda i, j: (0, j)),
            pl.BlockSpec((tm, HEAD_DIM), lambda i, j: (i % table_tiles, 0)),
            pl.BlockSpec((tm, HEAD_DIM), lambda i, j: (i % table_tiles, 0)),
        ],
        out_specs=[pl.BlockSpec((tm, tn), lambda i, j: (i, j))] + wb_specs,
        out_shape=[jax.ShapeDtypeStruct((m, n), F32)] + wb_shapes,
        scratch_shapes=[pltpu.VMEM((tm, d), BF16)],
        compiler_params=_cparams(("parallel", "arbitrary"), 56),
        name="inproj",
    )(x, ln1.reshape(1, d), w_in, cos, sin)
    return out[0], (out[1] if wb_specs else w_in)


def _attn_prompt_kernel(pt_ref, q_ref, k_ref, v_ref, kc_ref, o_ref, km_ref, ko_ref, vo_ref, ka_ref, va_ref,
                        kmean_ref, qa_ref, m_ref, acc_ref, pbuf, psem, osem, *, chunks_per_seq, n_chunks,
                        every_step):
    qi = pl.program_id(2)
    seq = k_ref.shape[0]
    nb = seq // MOBA_BLOCK
    rows = KV_GROUP * MOBA_BLOCK
    pair = 2 * MOBA_BLOCK
    step = (pl.program_id(0) * pl.num_programs(1) + pl.program_id(1)) * pl.num_programs(2) + qi
    stream = _KeyStream(pt_ref, kc_ref, km_ref, pbuf, psem, step, chunks_per_seq, n_chunks, every_step)
    stream.prefetch()

    def kv_out_copies():
        b, g = pl.program_id(0), pl.program_id(1)
        return [pltpu.make_async_copy(k_ref, ko_ref.at[b, :, g, :], osem.at[0]),
                pltpu.make_async_copy(v_ref, vo_ref.at[b, :, g, :], osem.at[1])]

    @pl.when(qi == 0)
    def _():
        for c in kv_out_copies():
            c.start()

    @pl.when(qi == pl.num_programs(2) - 1)
    def _():
        for c in kv_out_copies():
            c.wait()

    @pl.when(qi == 0)
    def _():
        k = k_ref[...]
        row_blk = lax.broadcasted_iota(jnp.int32, (seq, LANES), 0) // MOBA_BLOCK
        lane = lax.broadcasted_iota(jnp.int32, (seq, LANES), 1)
        ka_ref[:, 0:HEAD_DIM] = k.astype(BF16)
        ka_ref[:, HEAD_DIM:] = jnp.where(lane == row_blk, 1.0, 0.0).astype(BF16)
        va_ref[:, 0:HEAD_DIM] = v_ref[...].astype(BF16)
        va_ref[:, HEAD_DIM:] = jnp.ones((seq, LANES), BF16)
        kmean = jnp.mean(k.reshape(nb, MOBA_BLOCK, HEAD_DIM), axis=1)
        km_hi = kmean.astype(BF16)
        km_lo = (kmean - km_hi.astype(F32)).astype(BF16)
        kmean_ref[...] = jnp.zeros_like(kmean_ref)
        kmean_ref[0:nb, :] = jnp.concatenate([km_hi, km_hi, km_lo], axis=1)

    stream.reduce()

    q = q_ref[...]
    q4 = jnp.concatenate([q[:, h * HEAD_DIM:(h + 1) * HEAD_DIM] for h in range(KV_GROUP)], axis=0)
    qs = (q4 * (ATTN_SCALE * LOG2E)).astype(BF16)

    q_hi = q4.astype(BF16)
    q_lo = (q4 - q_hi.astype(F32)).astype(BF16)
    gate = lax.dot_general(jnp.concatenate([q_hi, q_lo, q_hi], axis=1), kmean_ref[...],
                           (((1,), (1,)), ((), ())), preferred_element_type=F32)
    lane_i = lax.broadcasted_iota(jnp.int32, (rows, LANES), 1)
    lane = lane_i.astype(F32)
    past = lane_i < qi
    gate = jnp.where(past, gate, -jnp.inf)
    bias = jnp.full((rows, LANES), MASK_BIAS, F32)
    for _ in range(MOBA_TOPK):
        top = jnp.max(gate, axis=1, keepdims=True)
        pick = lane == jnp.min(jnp.where(gate == top, lane, float(LANES)), axis=1, keepdims=True)
        bias = jnp.where(pick & past, 0.0, bias)
        gate = jnp.where(pick, -jnp.inf, gate)
    qa = jnp.concatenate([qs, bias.astype(BF16)], axis=1)

    own = pl.multiple_of(qi * MOBA_BLOCK, MOBA_BLOCK)
    s = lax.dot_general(qs, ka_ref[pl.ds(own, MOBA_BLOCK), 0:HEAD_DIM], (((1,), (1,)), ((), ())),
                        preferred_element_type=F32)
    qrow = lax.broadcasted_iota(jnp.int32, (rows, MOBA_BLOCK), 0) & (MOBA_BLOCK - 1)
    kcol = lax.broadcasted_iota(jnp.int32, (rows, MOBA_BLOCK), 1)
    s = jnp.where(kcol <= qrow, s, MASK_BIAS)
    m0 = jnp.broadcast_to(jnp.max(s, axis=1, keepdims=True), (rows, LANES))
    p = jnp.concatenate([jnp.exp2(s[:, c * LANES:(c + 1) * LANES] - m0)
                         for c in range(MOBA_BLOCK // LANES)], axis=1)
    m_ref[...] = m0
    acc_ref[...] = jnp.dot(p.astype(BF16), va_ref[pl.ds(own, MOBA_BLOCK), :], preferred_element_type=F32)

    qa_ref[...] = qa

    def past_pair(t, carry):
        start = pl.multiple_of(t * pair, pair)
        kt = ka_ref[pl.ds(start, pair), :]
        vt = va_ref[pl.ds(start, pair), :]
        groups = [slice(r0, r0 + ROW_GROUP) for r0 in range(0, rows, ROW_GROUP)]

        def scores(rs):
            return lax.dot_general(qa_ref[rs, :], kt, (((1,), (1,)), ((), ())), preferred_element_type=F32)

        m_olds = [m_ref[rs, :] for rs in groups]
        s_next = scores(groups[0])
        updates = []
        for gi, rs in enumerate(groups):
            s = s_next
            if gi + 1 < len(groups):
                s_next = scores(groups[gi + 1])
            m_old = m_olds[gi]
            m_new = jnp.maximum(m_old, jnp.max(s, axis=1, keepdims=True))
            alpha = jnp.exp2(m_old - m_new)
            p = jnp.concatenate([jnp.exp2(s[:, c * LANES:(c + 1) * LANES] - m_new)
                                 for c in range(pair // LANES)], axis=1)
            pv = jnp.dot(p.astype(BF16), vt, preferred_element_type=F32)
            updates.append((rs, m_new, alpha, pv))
        for rs, m_new, alpha, pv in updates:
            for c in range(2):
                cs = slice(c * LANES, (c + 1) * LANES)
                acc_ref[rs, cs] = alpha * acc_ref[rs, cs] + pv[:, cs]
            m_ref[rs, :] = m_new
        return carry

    lax.fori_loop(0, (qi + 1) // 2, past_pair, 0)

    o = acc_ref[:, 0:HEAD_DIM] / acc_ref[:, HEAD_DIM:]
    for h in range(KV_GROUP):
        o_ref[:, h * HEAD_DIM:(h + 1) * HEAD_DIM] = o[h * MOBA_BLOCK:(h + 1) * MOBA_BLOCK].astype(o_ref.dtype)


def _attn_prompt(proj, batch, seq, k_col, v_col, cache_k3, page_table):
    nq = seq // MOBA_BLOCK
    assert seq % (2 * MOBA_BLOCK) == 0 and MOBA_TOPK <= nq <= LANES
    gw = KV_GROUP * HEAD_DIM
    rows = KV_GROUP * MOBA_BLOCK
    kblk, vblk = k_col // HEAD_DIM, v_col // HEAD_DIM
    grid = (batch, N_KV_HEADS, nq)
    stream_kwargs, km_spec, km_shape, stream_scratch = _key_stream_specs(
        cache_k3, page_table, batch * N_KV_HEADS * nq, lambda b, g, i: (b * N_KV_HEADS + g) * nq + i)
    grid_spec = pltpu.PrefetchScalarGridSpec(
        num_scalar_prefetch=1,
        grid=grid,
        in_specs=[
            pl.BlockSpec((MOBA_BLOCK, gw), lambda b, g, i, pt: (b * nq + i, g)),
            pl.BlockSpec((seq, HEAD_DIM), lambda b, g, i, pt: (b, kblk + g)),
            pl.BlockSpec((seq, HEAD_DIM), lambda b, g, i, pt: (b, vblk + g)),
            pl.BlockSpec(memory_space=pl.ANY),
        ],
        out_specs=[pl.BlockSpec((MOBA_BLOCK, gw), lambda b, g, i, pt: (b * nq + i, g)), km_spec,
                   pl.BlockSpec(memory_space=pl.ANY), pl.BlockSpec(memory_space=pl.ANY)],
        scratch_shapes=[
            pltpu.VMEM((seq, HEAD_DIM + LANES), BF16),
            pltpu.VMEM((seq, HEAD_DIM + LANES), BF16),
            pltpu.VMEM((LANES, 3 * HEAD_DIM), BF16),
            pltpu.VMEM((rows, HEAD_DIM + LANES), BF16),
            pltpu.VMEM((rows, LANES), F32),
            pltpu.VMEM((rows, HEAD_DIM + LANES), F32),
        ] + stream_scratch + [pltpu.SemaphoreType.DMA((2,))],
    )
    kv_shape = jax.ShapeDtypeStruct((batch, seq, N_KV_HEADS, HEAD_DIM), proj.dtype)
    return pl.pallas_call(
        functools.partial(_attn_prompt_kernel, **stream_kwargs),
        grid_spec=grid_spec,
        out_shape=[jax.ShapeDtypeStruct((batch * seq, N_HEADS * HEAD_DIM), BF16), km_shape, kv_shape, kv_shape],
        compiler_params=_cparams(("arbitrary", "arbitrary", "arbitrary"), 48),
        name="attn_prompt",
    )(page_table, proj, proj, proj, cache_k3)


def _conv_seq_kernel(b_ref, c_ref, h_ref, cp_ref, hp_ref, st_ref, w_ref, cb_ref, ut_ref, *, seq):
    i = pl.program_id(0)
    tm = c_ref.shape[0]
    u = c_ref[...] * h_ref[...]
    prev = jnp.where((i * tm) % seq == 0, st_ref[0], cp_ref[...] * hp_ref[...])
    p1 = prev[SUBLANES - 1:SUBLANES, :]
    p2 = prev[SUBLANES - 2:SUBLANES - 1, :]
    row = lax.broadcasted_iota(jnp.int32, u.shape, 0)
    u1 = jnp.where(row == 0, p1, pltpu.roll(u, 1, axis=0))
    u2 = jnp.where(row == 0, p2, jnp.where(row == 1, p1, pltpu.roll(u, 2, axis=0)))
    w = w_ref[...]
    conv = w[0:1, :] * u2 + w[1:2, :] * u1 + w[2:3, :] * u
    cb_ref[...] = (b_ref[...] * conv).astype(BF16)
    ut_ref[0] = u[tm - SUBLANES:tm, :]


def _conv_seq(proj, state8, conv_w, seq, b_col, cw):
    m = proj.shape[0]
    tm = _row_tile(seq, 512)
    nt = m // tm
    cb = b_col // cw
    pstep = tm // SUBLANES
    return pl.pallas_call(
        functools.partial(_conv_seq_kernel, seq=seq),
        grid=(nt,),
        in_specs=[
            pl.BlockSpec((tm, cw), lambda i: (i, cb)),
            pl.BlockSpec((tm, cw), lambda i: (i, cb + 1)),
            pl.BlockSpec((tm, cw), lambda i: (i, cb + 2)),
            pl.BlockSpec((SUBLANES, cw), lambda i: (jnp.maximum(i * pstep - 1, 0), cb + 1)),
            pl.BlockSpec((SUBLANES, cw), lambda i: (jnp.maximum(i * pstep - 1, 0), cb + 2)),
            pl.BlockSpec((1, SUBLANES, cw), lambda i: ((i * tm) // seq, 0, 0)),
            pl.BlockSpec((CONV_K, cw), lambda i: (0, 0)),
        ],
        out_specs=[
            pl.BlockSpec((tm, cw), lambda i: (i, 0)),
            pl.BlockSpec((1, SUBLANES, cw), lambda i: (i, 0, 0)),
        ],
        out_shape=[
            jax.ShapeDtypeStruct((m, cw), BF16),
            jax.ShapeDtypeStruct((nt, SUBLANES, cw), F32),
        ],
        compiler_params=_cparams(("parallel",), 32),
        name="conv_seq",
    )(proj, proj, proj, proj, proj, state8, conv_w)


def _conv_step_kernel(b_ref, c_ref, h_ref, st_ref, w_ref, cb_ref, u_ref):
    cw = c_ref.shape[1]
    u = c_ref[...] * h_ref[...]
    w = w_ref[...]
    conv = w[0:1, :] * st_ref[:, 0:cw] + w[1:2, :] * st_ref[:, cw:2 * cw] + w[2:3, :] * u
    cb_ref[...] = (b_ref[...] * conv).astype(BF16)
    u_ref[...] = u


def _conv_step(proj, state, conv_w, b_col, cw):
    m = proj.shape[0]
    cb = b_col // cw
    return pl.pallas_call(
        _conv_step_kernel,
        grid=(1,),
        in_specs=[
            pl.BlockSpec((m, cw), lambda i: (0, cb)),
            pl.BlockSpec((m, cw), lambda i: (0, cb + 1)),
            pl.BlockSpec((m, cw), lambda i: (0, cb + 2)),
            pl.BlockSpec((m, (CONV_K - 1) * cw), lambda i: (0, 0)),
            pl.BlockSpec((CONV_K, cw), lambda i: (0, 0)),
        ],
        out_specs=[pl.BlockSpec((m, cw), lambda i: (0, 0)), pl.BlockSpec((m, cw), lambda i: (0, 0))],
        out_shape=[jax.ShapeDtypeStruct((m, cw), BF16), jax.ShapeDtypeStruct((m, cw), F32)],
        compiler_params=_cparams(("arbitrary",), 32),
        name="conv_step",
    )(proj, proj, proj, state, conv_w)


def _merge_outproj_kernel(a_ref, cb_ref, ga_ref, gc_ref, wa_ref, wc_ref, wo_ref, x_ref, o_ref, *rest, nj):
    *wb_refs, mg_ref = rest
    j = pl.program_id(1)

    @pl.when(j < nj)
    def _():
        ya = jnp.dot(a_ref[...], _mxu_weight(wa_ref, wb_refs[0:1]), preferred_element_type=F32)
        yc = jnp.dot(cb_ref[...], _mxu_weight(wc_ref, wb_refs[1:2]), preferred_element_type=F32)
        mg_ref[j] = (jax.nn.sigmoid(ga_ref[...]) * ya + jax.nn.sigmoid(gc_ref[...]) * yc).astype(BF16)

    @pl.when(j >= nj)
    def _():
        merged = jnp.concatenate([mg_ref[c] for c in range(nj)], axis=1)
        o_ref[...] = x_ref[...] + jnp.dot(merged, _mxu_weight(wo_ref, wb_refs[2:3]),
                                          preferred_element_type=F32)


def _merge_outproj(a, cb, proj, x, w_attn_br, w_conv_br, w_o, ga_col, gc_col):
    m, qw = a.shape
    cw = cb.shape[1]
    d = w_attn_br.shape[1]
    n = w_o.shape[1]
    tm, tn = _row_tile(m), COL_TILE
    nj = d // tn
    ga, gc = ga_col // tn, gc_col // tn
    assert n % tn == 0 and w_attn_br.dtype == w_conv_br.dtype == w_o.dtype

    def phase1(j):
        return jnp.minimum(j, nj - 1)

    def phase2(j):
        return jnp.maximum(j - nj, 0)

    wa_specs, wa_shapes = _weight_out(w_attn_br, (qw, tn), lambda i, j: (0, phase1(j)))
    wc_specs, wc_shapes = _weight_out(w_conv_br, (cw, tn), lambda i, j: (0, phase1(j)))
    wo_specs, wo_shapes = _weight_out(w_o, (d, tn), lambda i, j: (0, phase2(j)))
    assert not wa_specs or m == tm, "the bf16 weight copy is written once per weight tile"
    out = pl.pallas_call(
        functools.partial(_merge_outproj_kernel, nj=nj),
        grid=(m // tm, nj + n // tn),
        in_specs=[
            pl.BlockSpec((tm, qw), lambda i, j: (i, 0)),
            pl.BlockSpec((tm, cw), lambda i, j: (i, 0)),
            pl.BlockSpec((tm, tn), lambda i, j: (i, ga + phase1(j))),
            pl.BlockSpec((tm, tn), lambda i, j: (i, gc + phase1(j))),
            pl.BlockSpec((qw, tn), lambda i, j: (0, phase1(j))),
            pl.BlockSpec((cw, tn), lambda i, j: (0, phase1(j))),
            pl.BlockSpec((d, tn), lambda i, j: (0, phase2(j))),
            pl.BlockSpec((tm, tn), lambda i, j: (i, phase2(j))),
        ],
        out_specs=[pl.BlockSpec((tm, tn), lambda i, j: (i, phase2(j)))] + wa_specs + wc_specs + wo_specs,
        out_shape=[jax.ShapeDtypeStruct((m, n), F32)] + wa_shapes + wc_shapes + wo_shapes,
        scratch_shapes=[pltpu.VMEM((nj, tm, tn), BF16)],
        compiler_params=_cparams(("parallel", "arbitrary"), 56),
        name="merge_outproj",
    )(a, cb, proj, proj, w_attn_br, w_conv_br, w_o, x)
    return out[0], (tuple(out[1:4]) if wa_specs else (w_attn_br, w_conv_br, w_o))


def _ffn_up_kernel(x_ref, g_ref, wg_ref, wu_ref, o_ref, *rest):
    *wb_refs, hn_ref = rest

    @pl.when(pl.program_id(1) == 0)
    def _():
        hn_ref[...] = _rmsnorm_f32(x_ref[...], g_ref[...]).astype(BF16)

    hn = hn_ref[...]
    gate = jnp.dot(hn, _mxu_weight(wg_ref, wb_refs[0:1]), preferred_element_type=F32)
    up = jnp.dot(hn, _mxu_weight(wu_ref, wb_refs[1:2]), preferred_element_type=F32)
    o_ref[...] = (gate * jax.nn.sigmoid(gate) * up).astype(o_ref.dtype)


def _ffn_up(x, ln2, w_gate, w_up):
    m, d = x.shape
    f = w_gate.shape[1]
    tm, tn = _row_tile(m), COL_TILE
    assert f % tn == 0 and w_gate.dtype == w_up.dtype
    wg_specs, wg_shapes = _weight_out(w_gate, (d, tn), lambda i, j: (0, j))
    wu_specs, wu_shapes = _weight_out(w_up, (d, tn), lambda i, j: (0, j))
    assert not wg_specs or m == tm, "the bf16 weight copy is written once per weight tile"
    out = pl.pallas_call(
        _ffn_up_kernel,
        grid=(m // tm, f // tn),
        in_specs=[
            pl.BlockSpec((tm, d), lambda i, j: (i, 0)),
            pl.BlockSpec((1, d), lambda i, j: (0, 0)),
            pl.BlockSpec((d, tn), lambda i, j: (0, j)),
            pl.BlockSpec((d, tn), lambda i, j: (0, j)),
        ],
        out_specs=[pl.BlockSpec((tm, tn), lambda i, j: (i, j))] + wg_specs + wu_specs,
        out_shape=[jax.ShapeDtypeStruct((m, f), BF16)] + wg_shapes + wu_shapes,
        scratch_shapes=[pltpu.VMEM((tm, d), BF16)],
        compiler_params=_cparams(("parallel", "arbitrary"), 56),
        name="ffn_up",
    )(x, ln2.reshape(1, d), w_gate, w_up)
    return out[0], ((out[1], out[2]) if wg_specs else (w_gate, w_up))


def _ffn_down_kernel(h_ref, w_ref, x_ref, g_ref, o_ref, *wb_refs):
    k = pl.program_id(1)

    @pl.when(k == 0)
    def _():
        o_ref[...] = x_ref[...]

    o_ref[...] += jnp.dot(h_ref[...], _mxu_weight(w_ref, wb_refs), preferred_element_type=F32)

    @pl.when(k == pl.num_programs(1) - 1)
    def _():
        o_ref[...] = _rmsnorm_f32(o_ref[...], g_ref[...])


def _ffn_down(hmid, w_down, x, ln_f):
    m, f = hmid.shape
    d = w_down.shape[1]
    tm, tk = _row_tile(m), COL_TILE
    wb_specs, wb_shapes = _weight_out(w_down, (tk, d), lambda i, k: (k, 0))
    assert not wb_specs or m == tm, "the bf16 weight copy is written once per weight tile"
    out = pl.pallas_call(
        _ffn_down_kernel,
        grid=(m // tm, f // tk),
        in_specs=[
            pl.BlockSpec((tm, tk), lambda i, k: (i, k)),
            pl.BlockSpec((tk, d), lambda i, k: (k, 0)),
            pl.BlockSpec((tm, d), lambda i, k: (i, 0)),
            pl.BlockSpec((1, d), lambda i, k: (0, 0)),
        ],
        out_specs=[pl.BlockSpec((tm, d), lambda i, k: (i, 0))] + wb_specs,
        out_shape=[jax.ShapeDtypeStruct((m, d), F32)] + wb_shapes,
        compiler_params=_cparams(("parallel", "arbitrary"), 48),
        name="ffn_down",
    )(hmid, w_down, x, ln_f.reshape(1, d))
    return out[0], (out[1] if wb_specs else w_down)


def _sample_gate_kernel(q_ref, km_ref, sel_ref):
    db, _, nb, _ = km_ref.shape
    head_group = lax.broadcasted_iota(jnp.int32, (N_HEADS, nb), 0) // KV_GROUP
    blk = lax.broadcasted_iota(jnp.int32, (N_HEADS, nb), 1).astype(F32)
    lane = lax.broadcasted_iota(jnp.int32, (N_HEADS, LANES), 1)

    def one_sequence(b, carry):
        q = q_ref[b]
        gate = jnp.zeros((N_HEADS, nb), F32)
        for g in range(N_KV_HEADS):
            gg = lax.dot_general(q, km_ref[b, g], (((1,), (1,)), ((), ())),
                                 precision=lax.Precision.HIGHEST, preferred_element_type=F32)
            gate = jnp.where(head_group == g, gg, gate)
        out = jnp.zeros((N_HEADS, LANES), F32)
        for r in range(MOBA_TOPK):
            top = jnp.max(gate, axis=1, keepdims=True)
            first = jnp.min(jnp.where(gate == top, blk, float(nb)), axis=1, keepdims=True)
            out = jnp.where(lane == r, first, out)
            gate = jnp.where(blk == first, -jnp.inf, gate)
        sel_ref[b] = out.astype(jnp.int32)
        return carry

    lax.fori_loop(0, db, one_sequence, 0, unroll=2)


def _sample_gate(q_s, kmean):
    db, _, nb, _ = kmean.shape
    assert MOBA_TOPK <= nb
    sel = pl.pallas_call(
        _sample_gate_kernel,
        grid=(1,),
        in_specs=[pl.BlockSpec((db, N_HEADS, HEAD_DIM), lambda i: (0, 0, 0)),
                  pl.BlockSpec((db, N_KV_HEADS, nb, HEAD_DIM), lambda i: (0, 0, 0, 0))],
        out_specs=pl.BlockSpec((db, N_HEADS, LANES), lambda i: (0, 0, 0)),
        out_shape=jax.ShapeDtypeStruct((db, N_HEADS, LANES), jnp.int32),
        compiler_params=_cparams(("arbitrary",), 32),
        name="sample_gate",
    )(q_s, kmean)
    return sel[:, :, :MOBA_TOPK]


def _sample_attn_kernel(pt_ref, sel_ref, q_ref, ks_ref, vs_ref, kc_ref, vc_ref, o_ref, kbuf, vbuf, sem, *,
                        pages_per_block):
    b = pl.program_id(0)
    slot = b % 2
    _, n_heads, n_sel, page, hd = kbuf.shape

    def head_copies(bb, h, sl):
        g = h // KV_GROUP
        out = []
        for t in range(n_sel):
            blk = sel_ref[bb, h * MOBA_TOPK + t // pages_per_block]
            pg = pt_ref[bb, blk * pages_per_block + t % pages_per_block]
            out.append(pltpu.make_async_copy(kc_ref.at[pg, :, g, :], kbuf.at[sl, h, t], sem.at[0, sl, h]))
            out.append(pltpu.make_async_copy(vc_ref.at[pg, :, g, :], vbuf.at[sl, h, t], sem.at[1, sl, h]))
        return out

    def start_all(bb, sl):
        def start_head(h, carry):
            for c in head_copies(bb, h, sl):
                c.start()
            return carry
        lax.fori_loop(0, n_heads, start_head, 0)

    @pl.when(b == 0)
    def _():
        start_all(b, slot)

    @pl.when(b + 1 < pl.num_programs(0))
    def _():
        start_all(b + 1, 1 - slot)

    ones = jnp.ones((2 * hd, LANES), BF16)

    def one_head(h):
        g = h // KV_GROUP
        q = q_ref[0, pl.ds(h, 1), :] * ATTN_SCALE
        prod = kbuf[slot, h].reshape(n_sel * page, hd) * q
        hi = prod.astype(BF16)
        lo = (prod - hi.astype(F32)).astype(BF16)
        s = jnp.dot(jnp.concatenate([hi, lo], axis=1), ones, preferred_element_type=F32)
        s_new = jnp.sum(ks_ref[0, pl.ds(g, 1), :] * q, axis=1, keepdims=True)
        m = jnp.maximum(jnp.max(s, axis=0, keepdims=True), s_new)
        p = jnp.exp(s - m)
        p_new = jnp.exp(s_new - m)
        l = jnp.sum(p, axis=0, keepdims=True) + p_new
        o = jnp.sum(p * vbuf[slot, h].reshape(n_sel * page, hd), axis=0, keepdims=True)
        o = o + p_new * vs_ref[0, pl.ds(g, 1), :]
        o_ref[0, pl.ds(h, 1), :] = (o / l).astype(o_ref.dtype)

    def head_group(g, carry):
        for j in range(KV_GROUP):
            for c in head_copies(b, g * KV_GROUP + j, slot):
                c.wait()
        for j in range(KV_GROUP):
            one_head(g * KV_GROUP + j)
        return carry

    lax.fori_loop(0, n_heads // KV_GROUP, head_group, 0)


def _sample_attn(q_s, k_s, v_s, cache_k4, cache_v4, page_table, sel):
    db, n_pages = page_table.shape
    page = cache_k4.shape[1]
    ppb = MOBA_BLOCK // page
    n_sel = MOBA_TOPK * ppb
    assert HEAD_DIM == LANES

    grid_spec = pltpu.PrefetchScalarGridSpec(
        num_scalar_prefetch=2,
        grid=(db,),
        in_specs=[
            pl.BlockSpec((1, N_HEADS, HEAD_DIM), lambda b, pt, sl: (b, 0, 0)),
            pl.BlockSpec((1, N_KV_HEADS, HEAD_DIM), lambda b, pt, sl: (b, 0, 0)),
            pl.BlockSpec((1, N_KV_HEADS, HEAD_DIM), lambda b, pt, sl: (b, 0, 0)),
            pl.BlockSpec(memory_space=pl.ANY),
            pl.BlockSpec(memory_space=pl.ANY),
        ],
        out_specs=pl.BlockSpec((1, N_HEADS, HEAD_DIM), lambda b, pt, sl: (b, 0, 0)),
        scratch_shapes=[
            pltpu.VMEM((2, N_HEADS, n_sel, page, HEAD_DIM), F32),
            pltpu.VMEM((2, N_HEADS, n_sel, page, HEAD_DIM), F32),
            pltpu.SemaphoreType.DMA((2, 2, N_HEADS)),
        ],
    )
    return pl.pallas_call(
        functools.partial(_sample_attn_kernel, pages_per_block=ppb),
        grid_spec=grid_spec,
        out_shape=jax.ShapeDtypeStruct((db, N_HEADS, HEAD_DIM), F32),
        compiler_params=_cparams(("arbitrary",), 40),
        name="sample_attn",
    )(page_table, sel, q_s, k_s, v_s, cache_k4, cache_v4)


def _rope_tables(pos):
    half = HEAD_DIM // 2
    inv = 1.0 / (ROPE_THETA ** (jnp.arange(half, dtype=F32) / half))
    ang = pos.astype(F32)[:, None] * inv[None, :]
    cos, sin = jnp.cos(ang), jnp.sin(ang)
    return jnp.concatenate([cos, cos], axis=-1), jnp.concatenate([-sin, sin], axis=-1)


def _trunk_tail(x, a, cb, proj, w_attn_br, w_conv_br, w_o, ln2, w_ff_gate, w_ff_up, w_ff_down, ln_f,
                ga_col, gc_col):
    x1, (w_attn_br, w_conv_br, w_o) = _merge_outproj(a, cb, proj, x, w_attn_br, w_conv_br, w_o, ga_col, gc_col)
    hmid, (w_ff_gate, w_ff_up) = _ffn_up(x1, ln2, w_ff_gate, w_ff_up)
    y, w_ff_down = _ffn_down(hmid, w_ff_down, x1, ln_f)
    return y, (w_attn_br, w_conv_br, w_o, ln2, w_ff_gate, w_ff_up, w_ff_down, ln_f)


def kernel(x_prompt, x_sample, cache_k, cache_v, state_conv, page_table, ln1, w_in, conv_w, w_attn_br,
           w_conv_br, w_o, ln2, w_ff_gate, w_ff_up, w_ff_down, ln_f):
    batch, seq, d = x_prompt.shape
    db, dec_seq, _ = x_sample.shape
    depth, n_phys, page, kvh, hd = cache_k.shape
    assert depth == 1 and dec_seq == 1 and kvh == N_KV_HEADS and hd == HEAD_DIM
    qw, kvw = N_HEADS * HEAD_DIM, N_KV_HEADS * HEAD_DIM
    cw = conv_w.shape[-1]
    k_col, v_col, b_col = qw, qw + kvw, qw + 2 * kvw
    ga_col = b_col + 3 * cw
    gc_col = ga_col + d
    rope_cols = qw + kvw
    past = page_table.shape[1] * page
    trunk_f32 = (w_attn_br[0], w_conv_br[0], w_o[0], ln2[0], w_ff_gate[0], w_ff_up[0], w_ff_down[0], ln_f)
    cache_k3 = cache_k.reshape(n_phys, page * N_KV_HEADS, HEAD_DIM)
    cache_k4 = cache_k.reshape(n_phys, page, N_KV_HEADS, HEAD_DIM)
    cache_v4 = cache_v.reshape(n_phys, page, N_KV_HEADS, HEAD_DIM)

    xs = x_sample.reshape(db, d)
    cos_s, sin_s = _rope_tables(jnp.full((db,), past, jnp.int32))
    proj_s, w_in_b = _inproj(xs, ln1[0], w_in[0], cos_s, sin_s, rope_cols)
    q_s = proj_s[:, :qw]
    k_s = proj_s[:, k_col:k_col + kvw]
    v_s = proj_s[:, v_col:v_col + kvw]
    mp = batch * seq
    xp = x_prompt.reshape(mp, d)
    cos_p, sin_p = _rope_tables(jnp.arange(seq, dtype=jnp.int32))
    proj_p, _ = _inproj(xp, ln1[0], w_in_b, cos_p, sin_p, rope_cols)

    a_p, kmean_s, k_p, v_p = _attn_prompt(proj_p, batch, seq, k_col, v_col, cache_k3, page_table)
    zero_state = jnp.zeros((batch, SUBLANES, cw), F32)
    cb_p, utail_p = _conv_seq(proj_p, zero_state, conv_w[0], seq, b_col, cw)
    tiles_per_seq = utail_p.shape[0] // batch
    conv_p = utail_p.reshape(batch, tiles_per_seq, SUBLANES, cw)[:, -1, SUBLANES - (CONV_K - 1):, :]
    sel = _sample_gate(q_s.reshape(db, N_HEADS, HEAD_DIM), kmean_s)
    a_s = _sample_attn(q_s.reshape(db, N_HEADS, HEAD_DIM), k_s.reshape(db, N_KV_HEADS, HEAD_DIM),
                       v_s.reshape(db, N_KV_HEADS, HEAD_DIM), cache_k4, cache_v4, page_table,
                       sel.reshape(db, N_HEADS * MOBA_TOPK)).reshape(db, qw).astype(BF16)
    state = state_conv[0].astype(F32)
    cb_s, u_s = _conv_step(proj_s, state.reshape(db, (CONV_K - 1) * cw), conv_w[0], b_col, cw)
    conv_s = jnp.concatenate([state[:, 1:, :], u_s[:, None, :]], axis=1)

    y_s, trunk_bf16 = _trunk_tail(xs, a_s, cb_s, proj_s, *trunk_f32, ga_col, gc_col)
    y_p, _ = _trunk_tail(xp, a_p, cb_p, proj_p, *trunk_bf16, ga_col, gc_col)

    return (
        y_p.reshape(batch, seq, d),
        y_s.reshape(db, 1, d),
        k_p[None],
        v_p[None],
        conv_p[None],
        k_s.reshape(1, db, 1, N_KV_HEADS, HEAD_DIM),
        v_s.reshape(1, db, 1, N_KV_HEADS, HEAD_DIM),
        conv_s[None],
    )
```

```python
import functools

import jax
import jax.numpy as jnp
from jax import lax
from jax.experimental import pallas as pl
from jax.experimental.pallas import tpu as pltpu

F32 = jnp.float32
BF16 = jnp.bfloat16

N_HEADS = 16
HEAD_DIM = 128
N_KV_HEADS = 4
KV_GROUP = N_HEADS // N_KV_HEADS
MOBA_BLOCK = 256
MOBA_TOPK = 3
ROPE_THETA = 10000.0
CONV_K = 3
RMS_EPS = 1e-6
ATTN_SCALE = HEAD_DIM ** -0.5
LOG2E = 1.4426950408889634
MASK_BIAS = -1e30

LANES = 128
SUBLANES = 8
MIB = 1 << 20
ROW_TILE = 1024
WIDE_ROW_TILE = 2048
COL_TILE = 512
PAGES_PER_STEP = 32
ROW_GROUP = 256


VMEM_LIMIT_MIB = {
    "inproj": 56, "attn_prompt": 48, "conv_seq": 32, "conv_step": 32, "merge": 48, "outproj": 48,
    "ffn_up": 48, "ffn_down": 48, "sample_gate": 32, "sample_attn": 40,
}


def _cparams(name, semantics):
    return pltpu.CompilerParams(dimension_semantics=semantics, vmem_limit_bytes=VMEM_LIMIT_MIB[name] * MIB)


def _row_tile(m, cap=ROW_TILE):
    t = min(m, cap)
    assert m % t == 0 and t % SUBLANES == 0, (m, t)
    return t


def _rmsnorm_f32(x, g):
    return x * lax.rsqrt(jnp.mean(x * x, axis=-1, keepdims=True) + RMS_EPS) * g


def _mxu_weight(w_ref, wb_refs):
    if not wb_refs:
        return w_ref[...]
    w = w_ref[...].astype(BF16)
    wb_refs[0][...] = w
    return w


def _weight_out(w, block_shape, index_map):
    if w.dtype == BF16:
        return [], []
    assert w.dtype == F32
    return [pl.BlockSpec(block_shape, index_map)], [jax.ShapeDtypeStruct(w.shape, BF16)]


def _inproj_kernel(x_ref, g_ref, w_ref, cos_ref, sin_ref, o_ref, *rest, rope_tiles):
    *wb_refs, xn_ref = rest
    j = pl.program_id(1)

    @pl.when(j == 0)
    def _():
        xn_ref[...] = _rmsnorm_f32(x_ref[...], g_ref[...]).astype(BF16)

    acc = jnp.dot(xn_ref[...], _mxu_weight(w_ref, wb_refs), preferred_element_type=F32)

    @pl.when(j < rope_tiles)
    def _():
        cos = cos_ref[...]
        sin = sin_ref[...]
        for c in range(acc.shape[1] // HEAD_DIM):
            a = acc[:, c * HEAD_DIM:(c + 1) * HEAD_DIM]
            o_ref[:, c * HEAD_DIM:(c + 1) * HEAD_DIM] = a * cos + pltpu.roll(a, HEAD_DIM // 2, axis=1) * sin

    @pl.when(j >= rope_tiles)
    def _():
        o_ref[...] = acc


def _block_means(get_page, n_pages, page_rows):
    pages_per_block = MOBA_BLOCK * N_KV_HEADS // page_rows
    blocks = n_pages // pages_per_block
    row = lax.broadcasted_iota(jnp.int32, (blocks, HEAD_DIM), 0)
    means = [jnp.zeros((blocks, HEAD_DIM), F32) for _ in range(N_KV_HEADS)]
    for r in range(blocks):
        tot = jnp.zeros((SUBLANES, HEAD_DIM), F32)
        for t in range(pages_per_block):
            x = get_page(r * pages_per_block + t)
            tot = tot + jnp.sum(x.reshape(page_rows // SUBLANES, SUBLANES, HEAD_DIM), axis=0)
        head_sum = tot[0:N_KV_HEADS, :]
        for c in range(1, SUBLANES // N_KV_HEADS):
            head_sum = head_sum + tot[c * N_KV_HEADS:(c + 1) * N_KV_HEADS, :]
        head_mean = head_sum * (1.0 / MOBA_BLOCK)
        for g in range(N_KV_HEADS):
            means[g] = jnp.where(row == r, head_mean[g:g + 1, :], means[g])
    return means


class _KeyStream:
    def __init__(self, pt_ref, kc_ref, km_ref, pbuf, psem, step, chunks_per_seq, n_chunks, every_step):
        self.pt_ref, self.kc_ref, self.km_ref, self.pbuf, self.psem = pt_ref, kc_ref, km_ref, pbuf, psem
        self.step, self.chunks_per_seq, self.n_chunks = step, chunks_per_seq, n_chunks
        self.every_step = every_step
        self.slot = step % 2
        self.pages = pbuf.shape[1]

    def _copies(self, s, sl):
        b, c = s // self.chunks_per_seq, s % self.chunks_per_seq
        return [pltpu.make_async_copy(self.kc_ref.at[self.pt_ref[b, c * self.pages + t]],
                                      self.pbuf.at[sl, t], self.psem.at[sl])
                for t in range(self.pages)]

    def prefetch(self):
        @pl.when(self.step == 0)
        def _():
            for c in self._copies(self.step, self.slot):
                c.start()

        @pl.when(self.step + 1 < self.n_chunks)
        def _():
            for c in self._copies(self.step + 1, 1 - self.slot):
                c.start()

    def reduce(self):
        def body():
            for c in self._copies(self.step, self.slot):
                c.wait()
            means = _block_means(lambda t: self.pbuf[self.slot, t], self.pages, self.pbuf.shape[2])
            for g in range(N_KV_HEADS):
                self.km_ref[0, g] = means[g]

        if self.every_step:
            body()
        else:
            pl.when(self.step < self.n_chunks)(body)


def _key_stream_specs(cache_k3, page_table, n_steps, step_of):
    db, n_pages = page_table.shape
    _, page_rows, hd = cache_k3.shape
    page = page_rows // N_KV_HEADS
    pages = PAGES_PER_STEP
    assert MOBA_BLOCK % page == 0 and n_pages % pages == 0 and SUBLANES % N_KV_HEADS == 0
    blocks = pages * page // MOBA_BLOCK
    assert blocks % SUBLANES == 0 and hd == HEAD_DIM
    chunks_per_seq = n_pages // pages
    n_chunks = db * chunks_per_seq
    nb = n_pages * page // MOBA_BLOCK
    assert n_chunks <= n_steps, "not enough host grid steps to stream the key cache"

    def km_map(*ids_and_pt):
        s = jnp.minimum(step_of(*ids_and_pt[:-1]), n_chunks - 1)
        return (s // chunks_per_seq, 0, s % chunks_per_seq, 0)

    return (dict(chunks_per_seq=chunks_per_seq, n_chunks=n_chunks, every_step=n_chunks == n_steps),
            pl.BlockSpec((1, N_KV_HEADS, blocks, HEAD_DIM), km_map),
            jax.ShapeDtypeStruct((db, N_KV_HEADS, nb, HEAD_DIM), F32),
            [pltpu.VMEM((2, pages, page_rows, HEAD_DIM), F32), pltpu.SemaphoreType.DMA((2,))])


def _inproj(x, ln1, w_in, cos, sin, rope_cols):
    m, d = x.shape
    n = w_in.shape[1]
    tm, tn = _row_tile(m, WIDE_ROW_TILE), COL_TILE
    assert n % tn == 0 and rope_cols % tn == 0
    assert cos.shape[0] % tm == 0 and m % cos.shape[0] == 0
    table_tiles = cos.shape[0] // tm
    wb_specs, wb_shapes = _weight_out(w_in, (d, tn), lambda i, j: (0, j))
    assert not wb_specs or m == tm, "the bf16 weight copy is written once per weight tile"
    out = pl.pallas_call(
        functools.partial(_inproj_kernel, rope_tiles=rope_cols // tn),
        grid=(m // tm, n // tn),
        in_specs=[
            pl.BlockSpec((tm, d), lambda i, j: (i, 0), pipeline_mode=pl.Buffered(1)),
            pl.BlockSpec((1, d), lambda i, j: (0, 0)),
            pl.BlockSpec((d, tn), lambda i, j: (0, j)),
            pl.BlockSpec((tm, HEAD_DIM), lambda i, j: (i % table_tiles, 0)),
            pl.BlockSpec((tm, HEAD_DIM), lambda i, j: (i % table_tiles, 0)),
        ],
        out_specs=[pl.BlockSpec((tm, tn), lambda i, j: (i, j))] + wb_specs,
        out_shape=[jax.ShapeDtypeStruct((m, n), F32)] + wb_shapes,
        scratch_shapes=[pltpu.VMEM((tm, d), BF16)],
        compiler_params=_cparams("inproj", ("parallel", "arbitrary")),
        name="inproj",
    )(x, ln1.reshape(1, d), w_in, cos, sin)
    return out[0], (out[1] if wb_specs else w_in)


def _attn_prompt_kernel(pt_ref, q_ref, k_ref, v_ref, kc_ref, o_ref, km_ref, ko_ref, vo_ref, ka_ref, va_ref,
                        kmean_ref, qa_ref, m_ref, acc_ref, pbuf, psem, osem, *, chunks_per_seq, n_chunks,
                        every_step):
    qi = pl.program_id(2)
    seq = k_ref.shape[0]
    nb = seq // MOBA_BLOCK
    rows = KV_GROUP * MOBA_BLOCK
    pair = 2 * MOBA_BLOCK
    step = (pl.program_id(0) * pl.num_programs(1) + pl.program_id(1)) * pl.num_programs(2) + qi
    stream = _KeyStream(pt_ref, kc_ref, km_ref, pbuf, psem, step, chunks_per_seq, n_chunks, every_step)
    stream.prefetch()

    def kv_out_copies():
        b, g = pl.program_id(0), pl.program_id(1)
        return [pltpu.make_async_copy(k_ref, ko_ref.at[b, :, g, :], osem.at[0]),
                pltpu.make_async_copy(v_ref, vo_ref.at[b, :, g, :], osem.at[1])]

    @pl.when(qi == 0)
    def _():
        for c in kv_out_copies():
            c.start()

    @pl.when(qi == pl.num_programs(2) - 1)
    def _():
        for c in kv_out_copies():
            c.wait()

    @pl.when(qi == 0)
    def _():
        k = k_ref[...]
        row_blk = lax.broadcasted_iota(jnp.int32, (seq, LANES), 0) // MOBA_BLOCK
        lane = lax.broadcasted_iota(jnp.int32, (seq, LANES), 1)
        ka_ref[:, 0:HEAD_DIM] = k.astype(BF16)
        ka_ref[:, HEAD_DIM:] = jnp.where(lane == row_blk, 1.0, 0.0).astype(BF16)
        va_ref[:, 0:HEAD_DIM] = v_ref[...].astype(BF16)
        va_ref[:, HEAD_DIM:] = jnp.ones((seq, LANES), BF16)
        kmean = jnp.mean(k.reshape(nb, MOBA_BLOCK, HEAD_DIM), axis=1)
        km_hi = kmean.astype(BF16)
        km_lo = (kmean - km_hi.astype(F32)).astype(BF16)
        kmean_ref[...] = jnp.zeros_like(kmean_ref)
        kmean_ref[0:nb, :] = jnp.concatenate([km_hi, km_hi, km_lo], axis=1)

    stream.reduce()

    q = q_ref[...]
    q4 = jnp.concatenate([q[:, h * HEAD_DIM:(h + 1) * HEAD_DIM] for h in range(KV_GROUP)], axis=0)
    qs = (q4 * (ATTN_SCALE * LOG2E)).astype(BF16)

    q_hi = q4.astype(BF16)
    q_lo = (q4 - q_hi.astype(F32)).astype(BF16)
    gate = lax.dot_general(jnp.concatenate([q_hi, q_lo, q_hi], axis=1), kmean_ref[...],
                           (((1,), (1,)), ((), ())), preferred_element_type=F32)
    lane_i = lax.broadcasted_iota(jnp.int32, (rows, LANES), 1)
    lane = lane_i.astype(F32)
    past = lane_i < qi
    gate = jnp.where(past, gate, -jnp.inf)
    bias = jnp.full((rows, LANES), MASK_BIAS, F32)
    for _ in range(MOBA_TOPK):
        top = jnp.max(gate, axis=1, keepdims=True)
        pick = lane == jnp.min(jnp.where(gate == top, lane, float(LANES)), axis=1, keepdims=True)
        bias = jnp.where(pick & past, 0.0, bias)
        gate = jnp.where(pick, -jnp.inf, gate)
    qa = jnp.concatenate([qs, bias.astype(BF16)], axis=1)

    own = pl.multiple_of(qi * MOBA_BLOCK, MOBA_BLOCK)
    s = lax.dot_general(qs, ka_ref[pl.ds(own, MOBA_BLOCK), 0:HEAD_DIM], (((1,), (1,)), ((), ())),
                        preferred_element_type=F32)
    qrow = lax.broadcasted_iota(jnp.int32, (rows, MOBA_BLOCK), 0) & (MOBA_BLOCK - 1)
    kcol = lax.broadcasted_iota(jnp.int32, (rows, MOBA_BLOCK), 1)
    s = jnp.where(kcol <= qrow, s, MASK_BIAS)
    m0 = jnp.broadcast_to(jnp.max(s, axis=1, keepdims=True), (rows, LANES))
    p = jnp.concatenate([jnp.exp2(s[:, c * LANES:(c + 1) * LANES] - m0)
                         for c in range(MOBA_BLOCK // LANES)], axis=1)
    m_ref[...] = m0
    acc_ref[...] = jnp.dot(p.astype(BF16), va_ref[pl.ds(own, MOBA_BLOCK), :], preferred_element_type=F32)

    qa_ref[...] = qa

    def past_pair(t, carry):
        start = pl.multiple_of(t * pair, pair)
        kt = ka_ref[pl.ds(start, pair), :]
        vt = va_ref[pl.ds(start, pair), :]
        groups = [slice(r0, r0 + ROW_GROUP) for r0 in range(0, rows, ROW_GROUP)]

        def scores(rs):
            return lax.dot_general(qa_ref[rs, :], kt, (((1,), (1,)), ((), ())), preferred_element_type=F32)

        m_olds = [m_ref[rs, :] for rs in groups]
        s_next = scores(groups[0])
        updates = []
        for gi, rs in enumerate(groups):
            s = s_next
            if gi + 1 < len(groups):
                s_next = scores(groups[gi + 1])
            m_old = m_olds[gi]
            m_new = jnp.maximum(m_old, jnp.max(s, axis=1, keepdims=True))
            alpha = jnp.exp2(m_old - m_new)
            p = jnp.concatenate([jnp.exp2(s[:, c * LANES:(c + 1) * LANES] - m_new)
                                 for c in range(pair // LANES)], axis=1)
            pv = jnp.dot(p.astype(BF16), vt, preferred_element_type=F32)
            updates.append((rs, m_new, alpha, pv))
        for rs, m_new, alpha, pv in updates:
            for c in range(2):
                cs = slice(c * LANES, (c + 1) * LANES)
                acc_ref[rs, cs] = alpha * acc_ref[rs, cs] + pv[:, cs]
            m_ref[rs, :] = m_new
        return carry

    lax.fori_loop(0, (qi + 1) // 2, past_pair, 0)

    o = acc_ref[:, 0:HEAD_DIM] / acc_ref[:, HEAD_DIM:]
    for h in range(KV_GROUP):
        o_ref[:, h * HEAD_DIM:(h + 1) * HEAD_DIM] = o[h * MOBA_BLOCK:(h + 1) * MOBA_BLOCK].astype(o_ref.dtype)


def _attn_prompt(proj, batch, seq, k_col, v_col, cache_k3, page_table):
    nq = seq // MOBA_BLOCK
    assert seq % (2 * MOBA_BLOCK) == 0 and MOBA_TOPK <= nq <= LANES
    gw = KV_GROUP * HEAD_DIM
    rows = KV_GROUP * MOBA_BLOCK
    kblk, vblk = k_col // HEAD_DIM, v_col // HEAD_DIM
    grid = (batch, N_KV_HEADS, nq)
    stream_kwargs, km_spec, km_shape, stream_scratch = _key_stream_specs(
        cache_k3, page_table, batch * N_KV_HEADS * nq, lambda b, g, i: (b * N_KV_HEADS + g) * nq + i)
    grid_spec = pltpu.PrefetchScalarGridSpec(
        num_scalar_prefetch=1,
        grid=grid,
        in_specs=[
            pl.BlockSpec((MOBA_BLOCK, gw), lambda b, g, i, pt: (b * nq + i, g)),
            pl.BlockSpec((seq, HEAD_DIM), lambda b, g, i, pt: (b, kblk + g)),
            pl.BlockSpec((seq, HEAD_DIM), lambda b, g, i, pt: (b, vblk + g)),
            pl.BlockSpec(memory_space=pl.ANY),
        ],
        out_specs=[pl.BlockSpec((MOBA_BLOCK, gw), lambda b, g, i, pt: (b * nq + i, g)), km_spec,
                   pl.BlockSpec(memory_space=pl.ANY), pl.BlockSpec(memory_space=pl.ANY)],
        scratch_shapes=[
            pltpu.VMEM((seq, HEAD_DIM + LANES), BF16),
            pltpu.VMEM((seq, HEAD_DIM + LANES), BF16),
            pltpu.VMEM((LANES, 3 * HEAD_DIM), BF16),
            pltpu.VMEM((rows, HEAD_DIM + LANES), BF16),
            pltpu.VMEM((rows, LANES), F32),
            pltpu.VMEM((rows, HEAD_DIM + LANES), F32),
        ] + stream_scratch + [pltpu.SemaphoreType.DMA((2,))],
    )
    kv_shape = jax.ShapeDtypeStruct((batch, seq, N_KV_HEADS, HEAD_DIM), proj.dtype)
    return pl.pallas_call(
        functools.partial(_attn_prompt_kernel, **stream_kwargs),
        grid_spec=grid_spec,
        out_shape=[jax.ShapeDtypeStruct((batch * seq, N_HEADS * HEAD_DIM), BF16), km_shape, kv_shape, kv_shape],
        compiler_params=_cparams("attn_prompt", ("arbitrary", "arbitrary", "arbitrary")),
        name="attn_prompt",
    )(page_table, proj, proj, proj, cache_k3)


def _conv_seq_kernel(b_ref, c_ref, h_ref, cp_ref, hp_ref, st_ref, w_ref, cb_ref, ut_ref, *, seq):
    i = pl.program_id(0)
    tm = c_ref.shape[0]
    u = c_ref[...] * h_ref[...]
    prev = jnp.where((i * tm) % seq == 0, st_ref[0], cp_ref[...] * hp_ref[...])
    p1 = prev[SUBLANES - 1:SUBLANES, :]
    p2 = prev[SUBLANES - 2:SUBLANES - 1, :]
    row = lax.broadcasted_iota(jnp.int32, u.shape, 0)
    u1 = jnp.where(row == 0, p1, pltpu.roll(u, 1, axis=0))
    u2 = jnp.where(row == 0, p2, jnp.where(row == 1, p1, pltpu.roll(u, 2, axis=0)))
    w = w_ref[...]
    conv = w[0:1, :] * u2 + w[1:2, :] * u1 + w[2:3, :] * u
    cb_ref[...] = (b_ref[...] * conv).astype(BF16)
    ut_ref[0] = u[tm - SUBLANES:tm, :]


def _conv_seq(proj, state8, conv_w, seq, b_col, cw):
    m = proj.shape[0]
    tm = _row_tile(seq, 512)
    nt = m // tm
    cb = b_col // cw
    pstep = tm // SUBLANES
    return pl.pallas_call(
        functools.partial(_conv_seq_kernel, seq=seq),
        grid=(nt,),
        in_specs=[
            pl.BlockSpec((tm, cw), lambda i: (i, cb)),
            pl.BlockSpec((tm, cw), lambda i: (i, cb + 1)),
            pl.BlockSpec((tm, cw), lambda i: (i, cb + 2)),
            pl.BlockSpec((SUBLANES, cw), lambda i: (jnp.maximum(i * pstep - 1, 0), cb + 1)),
            pl.BlockSpec((SUBLANES, cw), lambda i: (jnp.maximum(i * pstep - 1, 0), cb + 2)),
            pl.BlockSpec((1, SUBLANES, cw), lambda i: ((i * tm) // seq, 0, 0)),
            pl.BlockSpec((CONV_K, cw), lambda i: (0, 0)),
        ],
        out_specs=[
            pl.BlockSpec((tm, cw), lambda i: (i, 0)),
            pl.BlockSpec((1, SUBLANES, cw), lambda i: (i, 0, 0)),
        ],
        out_shape=[
            jax.ShapeDtypeStruct((m, cw), BF16),
            jax.ShapeDtypeStruct((nt, SUBLANES, cw), F32),
        ],
        compiler_params=_cparams("conv_seq", ("parallel",)),
        name="conv_seq",
    )(proj, proj, proj, proj, proj, state8, conv_w)


def _conv_step_kernel(b_ref, c_ref, h_ref, st_ref, w_ref, cb_ref, u_ref):
    cw = c_ref.shape[1]
    u = c_ref[...] * h_ref[...]
    w = w_ref[...]
    conv = w[0:1, :] * st_ref[:, 0:cw] + w[1:2, :] * st_ref[:, cw:2 * cw] + w[2:3, :] * u
    cb_ref[...] = (b_ref[...] * conv).astype(BF16)
    u_ref[...] = u


def _conv_step(proj, state, conv_w, b_col, cw):
    m = proj.shape[0]
    cb = b_col // cw
    return pl.pallas_call(
        _conv_step_kernel,
        grid=(1,),
        in_specs=[
            pl.BlockSpec((m, cw), lambda i: (0, cb)),
            pl.BlockSpec((m, cw), lambda i: (0, cb + 1)),
            pl.BlockSpec((m, cw), lambda i: (0, cb + 2)),
            pl.BlockSpec((m, (CONV_K - 1) * cw), lambda i: (0, 0)),
            pl.BlockSpec((CONV_K, cw), lambda i: (0, 0)),
        ],
        out_specs=[pl.BlockSpec((m, cw), lambda i: (0, 0)), pl.BlockSpec((m, cw), lambda i: (0, 0))],
        out_shape=[jax.ShapeDtypeStruct((m, cw), BF16), jax.ShapeDtypeStruct((m, cw), F32)],
        compiler_params=_cparams("conv_step", ("arbitrary",)),
        name="conv_step",
    )(proj, proj, proj, state, conv_w)


def _merge_kernel(a_ref, cb_ref, ga_ref, gc_ref, wa_ref, wc_ref, o_ref, *wb_refs):
    ya = jnp.dot(a_ref[...], _mxu_weight(wa_ref, wb_refs[0:1]), preferred_element_type=F32)
    yc = jnp.dot(cb_ref[...], _mxu_weight(wc_ref, wb_refs[1:2]), preferred_element_type=F32)
    o_ref[...] = (jax.nn.sigmoid(ga_ref[...]) * ya + jax.nn.sigmoid(gc_ref[...]) * yc).astype(o_ref.dtype)


def _merge(a, cb, proj, w_attn_br, w_conv_br, ga_col, gc_col):
    m, qw = a.shape
    cw = cb.shape[1]
    d = w_attn_br.shape[1]
    tm, tn = _row_tile(m), COL_TILE
    ga, gc = ga_col // tn, gc_col // tn
    assert w_attn_br.dtype == w_conv_br.dtype
    wa_specs, wa_shapes = _weight_out(w_attn_br, (qw, tn), lambda i, j: (0, j))
    wc_specs, wc_shapes = _weight_out(w_conv_br, (cw, tn), lambda i, j: (0, j))
    assert not wa_specs or m == tm, "the bf16 weight copy is written once per weight tile"
    out = pl.pallas_call(
        _merge_kernel,
        grid=(m // tm, d // tn),
        in_specs=[
            pl.BlockSpec((tm, qw), lambda i, j: (i, 0)),
            pl.BlockSpec((tm, cw), lambda i, j: (i, 0)),
            pl.BlockSpec((tm, tn), lambda i, j: (i, ga + j)),
            pl.BlockSpec((tm, tn), lambda i, j: (i, gc + j)),
            pl.BlockSpec((qw, tn), lambda i, j: (0, j)),
            pl.BlockSpec((cw, tn), lambda i, j: (0, j)),
        ],
        out_specs=[pl.BlockSpec((tm, tn), lambda i, j: (i, j))] + wa_specs + wc_specs,
        out_shape=[jax.ShapeDtypeStruct((m, d), BF16)] + wa_shapes + wc_shapes,
        compiler_params=_cparams("merge", ("parallel", "parallel")),
        name="merge",
    )(a, cb, proj, proj, w_attn_br, w_conv_br)
    return out[0], ((out[1], out[2]) if wa_specs else (w_attn_br, w_conv_br))


def _outproj_kernel(m_ref, w_ref, x_ref, o_ref, *wb_refs):
    o_ref[...] = x_ref[...] + jnp.dot(m_ref[...], _mxu_weight(w_ref, wb_refs), preferred_element_type=F32)


def _outproj(merged, w_o, x):
    m, d = merged.shape
    n = w_o.shape[1]
    tm, tn = _row_tile(m, WIDE_ROW_TILE), COL_TILE
    wb_specs, wb_shapes = _weight_out(w_o, (d, tn), lambda i, j: (0, j))
    assert not wb_specs or m == tm, "the bf16 weight copy is written once per weight tile"
    out = pl.pallas_call(
        _outproj_kernel,
        grid=(m // tm, n // tn),
        in_specs=[
            pl.BlockSpec((tm, d), lambda i, j: (i, 0), pipeline_mode=pl.Buffered(1)),
            pl.BlockSpec((d, tn), lambda i, j: (0, j)),
            pl.BlockSpec((tm, tn), lambda i, j: (i, j)),
        ],
        out_specs=[pl.BlockSpec((tm, tn), lambda i, j: (i, j))] + wb_specs,
        out_shape=[jax.ShapeDtypeStruct((m, n), F32)] + wb_shapes,
        compiler_params=_cparams("outproj", ("parallel", "parallel")),
        name="outproj",
    )(merged, w_o, x)
    return out[0], (out[1] if wb_specs else w_o)


def _ffn_up_kernel(x_ref, g_ref, wg_ref, wu_ref, o_ref, *rest):
    *wb_refs, hn_ref = rest

    @pl.when(pl.program_id(1) == 0)
    def _():
        hn_ref[...] = _rmsnorm_f32(x_ref[...], g_ref[...]).astype(BF16)

    hn = hn_ref[...]
    gate = jnp.dot(hn, _mxu_weight(wg_ref, wb_refs[0:1]), preferred_element_type=F32)
    up = jnp.dot(hn, _mxu_weight(wu_ref, wb_refs[1:2]), preferred_element_type=F32)
    o_ref[...] = (gate * jax.nn.sigmoid(gate) * up).astype(o_ref.dtype)


def _ffn_up(x, ln2, w_gate, w_up):
    m, d = x.shape
    f = w_gate.shape[1]
    tm, tn = _row_tile(m), COL_TILE
    assert f % tn == 0 and w_gate.dtype == w_up.dtype
    wg_specs, wg_shapes = _weight_out(w_gate, (d, tn), lambda i, j: (0, j))
    wu_specs, wu_shapes = _weight_out(w_up, (d, tn), lambda i, j: (0, j))
    assert not wg_specs or m == tm, "the bf16 weight copy is written once per weight tile"
    out = pl.pallas_call(
        _ffn_up_kernel,
        grid=(m // tm, f // tn),
        in_specs=[
            pl.BlockSpec((tm, d), lambda i, j: (i, 0)),
            pl.BlockSpec((1, d), lambda i, j: (0, 0)),
            pl.BlockSpec((d, tn), lambda i, j: (0, j)),
            pl.BlockSpec((d, tn), lambda i, j: (0, j)),
        ],
        out_specs=[pl.BlockSpec((tm, tn), lambda i, j: (i, j))] + wg_specs + wu_specs,
        out_shape=[jax.ShapeDtypeStruct((m, f), BF16)] + wg_shapes + wu_shapes,
        scratch_shapes=[pltpu.VMEM((tm, d), BF16)],
        compiler_params=_cparams("ffn_up", ("parallel", "arbitrary")),
        name="ffn_up",
    )(x, ln2.reshape(1, d), w_gate, w_up)
    return out[0], ((out[1], out[2]) if wg_specs else (w_gate, w_up))


def _ffn_down_kernel(h_ref, w_ref, x_ref, g_ref, o_ref, *wb_refs):
    k = pl.program_id(1)

    @pl.when(k == 0)
    def _():
        o_ref[...] = x_ref[...]

    o_ref[...] += jnp.dot(h_ref[...], _mxu_weight(w_ref, wb_refs), preferred_element_type=F32)

    @pl.when(k == pl.num_programs(1) - 1)
    def _():
        o_ref[...] = _rmsnorm_f32(o_ref[...], g_ref[...])


def _ffn_down(hmid, w_down, x, ln_f):
    m, f = hmid.shape
    d = w_down.shape[1]
    tm, tk = _row_tile(m), COL_TILE
    wb_specs, wb_shapes = _weight_out(w_down, (tk, d), lambda i, k: (k, 0))
    assert not wb_specs or m == tm, "the bf16 weight copy is written once per weight tile"
    out = pl.pallas_call(
        _ffn_down_kernel,
        grid=(m // tm, f // tk),
        in_specs=[
            pl.BlockSpec((tm, tk), lambda i, k: (i, k)),
            pl.BlockSpec((tk, d), lambda i, k: (k, 0)),
            pl.BlockSpec((tm, d), lambda i, k: (i, 0)),
            pl.BlockSpec((1, d), lambda i, k: (0, 0)),
        ],
        out_specs=[pl.BlockSpec((tm, d), lambda i, k: (i, 0))] + wb_specs,
        out_shape=[jax.ShapeDtypeStruct((m, d), F32)] + wb_shapes,
        compiler_params=_cparams("ffn_down", ("parallel", "arbitrary")),
        name="ffn_down",
    )(hmid, w_down, x, ln_f.reshape(1, d))
    return out[0], (out[1] if wb_specs else w_down)


def _sample_gate_kernel(q_ref, km_ref, sel_ref):
    db, _, nb, _ = km_ref.shape
    head_group = lax.broadcasted_iota(jnp.int32, (N_HEADS, nb), 0) // KV_GROUP
    blk = lax.broadcasted_iota(jnp.int32, (N_HEADS, nb), 1).astype(F32)
    lane = lax.broadcasted_iota(jnp.int32, (N_HEADS, LANES), 1)

    def one_sequence(b, carry):
        q = q_ref[b]
        gate = jnp.zeros((N_HEADS, nb), F32)
        for g in range(N_KV_HEADS):
            gg = lax.dot_general(q, km_ref[b, g], (((1,), (1,)), ((), ())),
                                 precision=lax.Precision.HIGHEST, preferred_element_type=F32)
            gate = jnp.where(head_group == g, gg, gate)
        out = jnp.zeros((N_HEADS, LANES), F32)
        for r in range(MOBA_TOPK):
            top = jnp.max(gate, axis=1, keepdims=True)
            first = jnp.min(jnp.where(gate == top, blk, float(nb)), axis=1, keepdims=True)
            out = jnp.where(lane == r, first, out)
            gate = jnp.where(blk == first, -jnp.inf, gate)
        sel_ref[b] = out.astype(jnp.int32)
        return carry

    lax.fori_loop(0, db, one_sequence, 0, unroll=2)


def _sample_gate(q_s, kmean):
    db, _, nb, _ = kmean.shape
    assert MOBA_TOPK <= nb
    sel = pl.pallas_call(
        _sample_gate_kernel,
        grid=(1,),
        in_specs=[pl.BlockSpec((db, N_HEADS, HEAD_DIM), lambda i: (0, 0, 0)),
                  pl.BlockSpec((db, N_KV_HEADS, nb, HEAD_DIM), lambda i: (0, 0, 0, 0))],
        out_specs=pl.BlockSpec((db, N_HEADS, LANES), lambda i: (0, 0, 0)),
        out_shape=jax.ShapeDtypeStruct((db, N_HEADS, LANES), jnp.int32),
        compiler_params=_cparams("sample_gate", ("arbitrary",)),
        name="sample_gate",
    )(q_s, kmean)
    return sel[:, :, :MOBA_TOPK]


def _sample_attn_kernel(pt_ref, sel_ref, q_ref, ks_ref, vs_ref, kc_ref, vc_ref, o_ref, kbuf, vbuf, sem, *,
                        pages_per_block):
    b = pl.program_id(0)
    slot = b % 2
    _, n_heads, n_sel, page, hd = kbuf.shape

    def head_copies(bb, h, sl):
        g = h // KV_GROUP
        out = []
        for t in range(n_sel):
            blk = sel_ref[bb, h * MOBA_TOPK + t // pages_per_block]
            pg = pt_ref[bb, blk * pages_per_block + t % pages_per_block]
            out.append(pltpu.make_async_copy(kc_ref.at[pg, :, g, :], kbuf.at[sl, h, t], sem.at[0, sl, h]))
            out.append(pltpu.make_async_copy(vc_ref.at[pg, :, g, :], vbuf.at[sl, h, t], sem.at[1, sl, h]))
        return out

    def start_all(bb, sl):
        def start_head(h, carry):
            for c in head_copies(bb, h, sl):
                c.start()
            return carry
        lax.fori_loop(0, n_heads, start_head, 0)

    @pl.when(b == 0)
    def _():
        start_all(b, slot)

    @pl.when(b + 1 < pl.num_programs(0))
    def _():
        start_all(b + 1, 1 - slot)

    ones = jnp.ones((2 * hd, LANES), BF16)

    def one_head(h):
        g = h // KV_GROUP
        q = q_ref[0, pl.ds(h, 1), :] * ATTN_SCALE
        prod = kbuf[slot, h].reshape(n_sel * page, hd) * q
        hi = prod.astype(BF16)
        lo = (prod - hi.astype(F32)).astype(BF16)
        s = jnp.dot(jnp.concatenate([hi, lo], axis=1), ones, preferred_element_type=F32)
        s_new = jnp.sum(ks_ref[0, pl.ds(g, 1), :] * q, axis=1, keepdims=True)
        m = jnp.maximum(jnp.max(s, axis=0, keepdims=True), s_new)
        p = jnp.exp(s - m)
        p_new = jnp.exp(s_new - m)
        l = jnp.sum(p, axis=0, keepdims=True) + p_new
        o = jnp.sum(p * vbuf[slot, h].reshape(n_sel * page, hd), axis=0, keepdims=True)
        o = o + p_new * vs_ref[0, pl.ds(g, 1), :]
        o_ref[0, pl.ds(h, 1), :] = (o / l).astype(o_ref.dtype)

    def head_group(g, carry):
        for j in range(KV_GROUP):
            for c in head_copies(b, g * KV_GROUP + j, slot):
                c.wait()
        for j in range(KV_GROUP):
            one_head(g * KV_GROUP + j)
        return carry

    lax.fori_loop(0, n_heads // KV_GROUP, head_group, 0)


def _sample_attn(q_s, k_s, v_s, cache_k4, cache_v4, page_table, sel):
    db, n_pages = page_table.shape
    page = cache_k4.shape[1]
    ppb = MOBA_BLOCK // page
    n_sel = MOBA_TOPK * ppb
    assert HEAD_DIM == LANES

    grid_spec = pltpu.PrefetchScalarGridSpec(
        num_scalar_prefetch=2,
        grid=(db,),
        in_specs=[
            pl.BlockSpec((1, N_HEADS, HEAD_DIM), lambda b, pt, sl: (b, 0, 0)),
            pl.BlockSpec((1, N_KV_HEADS, HEAD_DIM), lambda b, pt, sl: (b, 0, 0)),
            pl.BlockSpec((1, N_KV_HEADS, HEAD_DIM), lambda b, pt, sl: (b, 0, 0)),
            pl.BlockSpec(memory_space=pl.ANY),
            pl.BlockSpec(memory_space=pl.ANY),
        ],
        out_specs=pl.BlockSpec((1, N_HEADS, HEAD_DIM), lambda b, pt, sl: (b, 0, 0)),
        scratch_shapes=[
            pltpu.VMEM((2, N_HEADS, n_sel, page, HEAD_DIM), F32),
            pltpu.VMEM((2, N_HEADS, n_sel, page, HEAD_DIM), F32),
            pltpu.SemaphoreType.DMA((2, 2, N_HEADS)),
        ],
    )
    return pl.pallas_call(
        functools.partial(_sample_attn_kernel, pages_per_block=ppb),
        grid_spec=grid_spec,
        out_shape=jax.ShapeDtypeStruct((db, N_HEADS, HEAD_DIM), F32),
        compiler_params=_cparams("sample_attn", ("arbitrary",)),
        name="sample_attn",
    )(page_table, sel, q_s, k_s, v_s, cache_k4, cache_v4)


def _rope_tables(pos):
    half = HEAD_DIM // 2
    inv = 1.0 / (ROPE_THETA ** (jnp.arange(half, dtype=F32) / half))
    ang = pos.astype(F32)[:, None] * inv[None, :]
    cos, sin = jnp.cos(ang), jnp.sin(ang)
    return jnp.concatenate([cos, cos], axis=-1), jnp.concatenate([-sin, sin], axis=-1)


def _trunk_tail(x, a, cb, proj, w_attn_br, w_conv_br, w_o, ln2, w_ff_gate, w_ff_up, w_ff_down, ln_f,
                ga_col, gc_col):
    merged, (w_attn_br, w_conv_br) = _merge(a, cb, proj, w_attn_br, w_conv_br, ga_col, gc_col)
    x1, w_o = _outproj(merged, w_o, x)
    hmid, (w_ff_gate, w_ff_up) = _ffn_up(x1, ln2, w_ff_gate, w_ff_up)
    y, w_ff_down = _ffn_down(hmid, w_ff_down, x1, ln_f)
    return y, (w_attn_br, w_conv_br, w_o, ln2, w_ff_gate, w_ff_up, w_ff_down, ln_f)


def kernel(x_prompt, x_sample, cache_k, cache_v, state_conv, page_table, ln1, w_in, conv_w, w_attn_br,
           w_conv_br, w_o, ln2, w_ff_gate, w_ff_up, w_ff_down, ln_f):
    batch, seq, d = x_prompt.shape
    db, dec_seq, _ = x_sample.shape
    depth, n_phys, page, kvh, hd = cache_k.shape
    assert depth == 1 and dec_seq == 1 and kvh == N_KV_HEADS and hd == HEAD_DIM
    qw, kvw = N_HEADS * HEAD_DIM, N_KV_HEADS * HEAD_DIM
    cw = conv_w.shape[-1]
    k_col, v_col, b_col = qw, qw + kvw, qw + 2 * kvw
    ga_col = b_col + 3 * cw
    gc_col = ga_col + d
    rope_cols = qw + kvw
    past = page_table.shape[1] * page
    trunk_f32 = (w_attn_br[0], w_conv_br[0], w_o[0], ln2[0], w_ff_gate[0], w_ff_up[0], w_ff_down[0], ln_f)
    cache_k3 = cache_k.reshape(n_phys, page * N_KV_HEADS, HEAD_DIM)
    cache_k4 = cache_k.reshape(n_phys, page, N_KV_HEADS, HEAD_DIM)
    cache_v4 = cache_v.reshape(n_phys, page, N_KV_HEADS, HEAD_DIM)

    xs = x_sample.reshape(db, d)
    cos_s, sin_s = _rope_tables(jnp.full((db,), past, jnp.int32))
    proj_s, w_in_b = _inproj(xs, ln1[0], w_in[0], cos_s, sin_s, rope_cols)
    q_s = proj_s[:, :qw]
    k_s = proj_s[:, k_col:k_col + kvw]
    v_s = proj_s[:, v_col:v_col + kvw]
    mp = batch * seq
    xp = x_prompt.reshape(mp, d)
    cos_p, sin_p = _rope_tables(jnp.arange(seq, dtype=jnp.int32))
    proj_p, _ = _inproj(xp, ln1[0], w_in_b, cos_p, sin_p, rope_cols)

    a_p, kmean_s, k_p, v_p = _attn_prompt(proj_p, batch, seq, k_col, v_col, cache_k3, page_table)
    zero_state = jnp.zeros((batch, SUBLANES, cw), F32)
    cb_p, utail_p = _conv_seq(proj_p, zero_state, conv_w[0], seq, b_col, cw)
    tiles_per_seq = utail_p.shape[0] // batch
    conv_p = utail_p.reshape(batch, tiles_per_seq, SUBLANES, cw)[:, -1, SUBLANES - (CONV_K - 1):, :]
    sel = _sample_gate(q_s.reshape(db, N_HEADS, HEAD_DIM), kmean_s)
    a_s = _sample_attn(q_s.reshape(db, N_HEADS, HEAD_DIM), k_s.reshape(db, N_KV_HEADS, HEAD_DIM),
                       v_s.reshape(db, N_KV_HEADS, HEAD_DIM), cache_k4, cache_v4, page_table,
                       sel.reshape(db, N_HEADS * MOBA_TOPK)).reshape(db, qw).astype(BF16)
    state = state_conv[0].astype(F32)
    cb_s, u_s = _conv_step(proj_s, state.reshape(db, (CONV_K - 1) * cw), conv_w[0], b_col, cw)
    conv_s = jnp.concatenate([state[:, 1:, :], u_s[:, None, :]], axis=1)

    y_s, trunk_bf16 = _trunk_tail(xs, a_s, cb_s, proj_s, *trunk_f32, ga_col, gc_col)
    y_p, _ = _trunk_tail(xp, a_p, cb_p, proj_p, *trunk_bf16, ga_col, gc_col)

    return (
        y_p.reshape(batch, seq, d),
        y_s.reshape(db, 1, d),
        k_p[None],
        v_p[None],
        conv_p[None],
        k_s.reshape(1, db, 1, N_KV_HEADS, HEAD_DIM),
        v_s.reshape(1, db, 1, N_KV_HEADS, HEAD_DIM),
        conv_s[None],
    )
```

```python
import functools

import jax
import jax.numpy as jnp
from jax import lax
from jax.experimental import pallas as pl
from jax.experimental.pallas import tpu as pltpu

F32 = jnp.float32
BF16 = jnp.bfloat16

N_HEADS = 16
HEAD_DIM = 128
N_KV_HEADS = 4
KV_GROUP = N_HEADS // N_KV_HEADS
MOBA_BLOCK = 256
MOBA_TOPK = 3
ROPE_THETA = 10000.0
CONV_K = 3
RMS_EPS = 1e-6
ATTN_SCALE = HEAD_DIM ** -0.5
LOG2E = 1.4426950408889634
MASK_BIAS = -1e30

LANES = 128
SUBLANES = 8
MIB = 1 << 20
ROW_TILE = 1024
WIDE_ROW_TILE = 2048
COL_TILE = 512
PAGES_PER_STEP = 32
ROW_GROUP = 512


VMEM_LIMIT_MIB = {
    "inproj": 56, "attn_prompt": 48, "conv_seq": 32, "conv_step": 32, "merge": 48, "outproj": 48,
    "ffn_up": 48, "ffn_down": 48, "sample_gate": 32, "sample_attn": 40,
}


def _cparams(name, semantics):
    return pltpu.CompilerParams(dimension_semantics=semantics, vmem_limit_bytes=VMEM_LIMIT_MIB[name] * MIB)


def _row_tile(m, cap=ROW_TILE):
    t = min(m, cap)
    assert m % t == 0 and t % SUBLANES == 0, (m, t)
    return t


def _rmsnorm_f32(x, g):
    return x * lax.rsqrt(jnp.mean(x * x, axis=-1, keepdims=True) + RMS_EPS) * g


def _mxu_weight(w_ref, wb_refs):
    if not wb_refs:
        return w_ref[...]
    w = w_ref[...].astype(BF16)
    wb_refs[0][...] = w
    return w


def _weight_out(w, block_shape, index_map):
    if w.dtype == BF16:
        return [], []
    assert w.dtype == F32
    return [pl.BlockSpec(block_shape, index_map)], [jax.ShapeDtypeStruct(w.shape, BF16)]


def _inproj_kernel(x_ref, g_ref, w_ref, cos_ref, sin_ref, o_ref, *rest, rope_tiles):
    *wb_refs, xn_ref = rest
    j = pl.program_id(1)

    @pl.when(j == 0)
    def _():
        xn_ref[...] = _rmsnorm_f32(x_ref[...], g_ref[...]).astype(BF16)

    acc = jnp.dot(xn_ref[...], _mxu_weight(w_ref, wb_refs), preferred_element_type=F32)

    @pl.when(j < rope_tiles)
    def _():
        cos = cos_ref[...]
        sin = sin_ref[...]
        for c in range(acc.shape[1] // HEAD_DIM):
            a = acc[:, c * HEAD_DIM:(c + 1) * HEAD_DIM]
            o_ref[:, c * HEAD_DIM:(c + 1) * HEAD_DIM] = a * cos + pltpu.roll(a, HEAD_DIM // 2, axis=1) * sin

    @pl.when(j >= rope_tiles)
    def _():
        o_ref[...] = acc


def _block_means(get_page, n_pages, page_rows):
    pages_per_block = MOBA_BLOCK * N_KV_HEADS // page_rows
    blocks = n_pages // pages_per_block
    row = lax.broadcasted_iota(jnp.int32, (blocks, HEAD_DIM), 0)
    means = [jnp.zeros((blocks, HEAD_DIM), F32) for _ in range(N_KV_HEADS)]
    for r in range(blocks):
        tot = jnp.zeros((SUBLANES, HEAD_DIM), F32)
        for t in range(pages_per_block):
            x = get_page(r * pages_per_block + t)
            tot = tot + jnp.sum(x.reshape(page_rows // SUBLANES, SUBLANES, HEAD_DIM), axis=0)
        head_sum = tot[0:N_KV_HEADS, :]
        for c in range(1, SUBLANES // N_KV_HEADS):
            head_sum = head_sum + tot[c * N_KV_HEADS:(c + 1) * N_KV_HEADS, :]
        head_mean = head_sum * (1.0 / MOBA_BLOCK)
        for g in range(N_KV_HEADS):
            means[g] = jnp.where(row == r, head_mean[g:g + 1, :], means[g])
    return means


class _KeyStream:
    def __init__(self, pt_ref, kc_ref, km_ref, pbuf, psem, step, chunks_per_seq, n_chunks, every_step):
        self.pt_ref, self.kc_ref, self.km_ref, self.pbuf, self.psem = pt_ref, kc_ref, km_ref, pbuf, psem
        self.step, self.chunks_per_seq, self.n_chunks = step, chunks_per_seq, n_chunks
        self.every_step = every_step
        self.slot = step % 2
        self.pages = pbuf.shape[1]

    def _copies(self, s, sl):
        b, c = s // self.chunks_per_seq, s % self.chunks_per_seq
        return [pltpu.make_async_copy(self.kc_ref.at[self.pt_ref[b, c * self.pages + t]],
                                      self.pbuf.at[sl, t], self.psem.at[sl])
                for t in range(self.pages)]

    def prefetch(self):
        @pl.when(self.step == 0)
        def _():
            for c in self._copies(self.step, self.slot):
                c.start()

        @pl.when(self.step + 1 < self.n_chunks)
        def _():
            for c in self._copies(self.step + 1, 1 - self.slot):
                c.start()

    def reduce(self):
        def body():
            for c in self._copies(self.step, self.slot):
                c.wait()
            means = _block_means(lambda t: self.pbuf[self.slot, t], self.pages, self.pbuf.shape[2])
            for g in range(N_KV_HEADS):
                self.km_ref[0, g] = means[g]

        if self.every_step:
            body()
        else:
            pl.when(self.step < self.n_chunks)(body)


def _key_stream_specs(cache_k3, page_table, n_steps, step_of):
    db, n_pages = page_table.shape
    _, page_rows, hd = cache_k3.shape
    page = page_rows // N_KV_HEADS
    pages = PAGES_PER_STEP
    assert MOBA_BLOCK % page == 0 and n_pages % pages == 0 and SUBLANES % N_KV_HEADS == 0
    blocks = pages * page // MOBA_BLOCK
    assert blocks % SUBLANES == 0 and hd == HEAD_DIM
    chunks_per_seq = n_pages // pages
    n_chunks = db * chunks_per_seq
    nb = n_pages * page // MOBA_BLOCK
    assert n_chunks <= n_steps, "not enough host grid steps to stream the key cache"

    def km_map(*ids_and_pt):
        s = jnp.minimum(step_of(*ids_and_pt[:-1]), n_chunks - 1)
        return (s // chunks_per_seq, 0, s % chunks_per_seq, 0)

    return (dict(chunks_per_seq=chunks_per_seq, n_chunks=n_chunks, every_step=n_chunks == n_steps),
            pl.BlockSpec((1, N_KV_HEADS, blocks, HEAD_DIM), km_map),
            jax.ShapeDtypeStruct((db, N_KV_HEADS, nb, HEAD_DIM), F32),
            [pltpu.VMEM((2, pages, page_rows, HEAD_DIM), F32), pltpu.SemaphoreType.DMA((2,))])


def _inproj(x, ln1, w_in, cos, sin, rope_cols):
    m, d = x.shape
    n = w_in.shape[1]
    tm, tn = _row_tile(m, WIDE_ROW_TILE), COL_TILE
    assert n % tn == 0 and rope_cols % tn == 0
    assert cos.shape[0] % tm == 0 and m % cos.shape[0] == 0
    table_tiles = cos.shape[0] // tm
    wb_specs, wb_shapes = _weight_out(w_in, (d, tn), lambda i, j: (0, j))
    assert not wb_specs or m == tm, "the bf16 weight copy is written once per weight tile"
    out = pl.pallas_call(
        functools.partial(_inproj_kernel, rope_tiles=rope_cols // tn),
        grid=(m // tm, n // tn),
        in_specs=[
            pl.BlockSpec((tm, d), lambda i, j: (i, 0), pipeline_mode=pl.Buffered(1)),
            pl.BlockSpec((1, d), lambda i, j: (0, 0)),
            pl.BlockSpec((d, tn), lambda i, j: (0, j)),
            pl.BlockSpec((tm, HEAD_DIM), lambda i, j: (i % table_tiles, 0)),
            pl.BlockSpec((tm, HEAD_DIM), lambda i, j: (i % table_tiles, 0)),
        ],
        out_specs=[pl.BlockSpec((tm, tn), lambda i, j: (i, j))] + wb_specs,
        out_shape=[jax.ShapeDtypeStruct((m, n), F32)] + wb_shapes,
        scratch_shapes=[pltpu.VMEM((tm, d), BF16)],
        compiler_params=_cparams("inproj", ("parallel", "arbitrary")),
        name="inproj",
    )(x, ln1.reshape(1, d), w_in, cos, sin)
    return out[0], (out[1] if wb_specs else w_in)


def _attn_prompt_kernel(pt_ref, q_ref, k_ref, v_ref, kc_ref, o_ref, km_ref, ko_ref, vo_ref, ka_ref, va_ref,
                        kmean_ref, qa_ref, m_ref, acc_ref, pbuf, psem, osem, *, chunks_per_seq, n_chunks,
                        every_step):
    qi = pl.program_id(2)
    seq = k_ref.shape[0]
    nb = seq // MOBA_BLOCK
    rows = KV_GROUP * MOBA_BLOCK
    pair = 2 * MOBA_BLOCK
    step = (pl.program_id(0) * pl.num_programs(1) + pl.program_id(1)) * pl.num_programs(2) + qi
    stream = _KeyStream(pt_ref, kc_ref, km_ref, pbuf, psem, step, chunks_per_seq, n_chunks, every_step)
    stream.prefetch()

    def kv_out_copies():
        b, g = pl.program_id(0), pl.program_id(1)
        return [pltpu.make_async_copy(k_ref, ko_ref.at[b, :, g, :], osem.at[0]),
                pltpu.make_async_copy(v_ref, vo_ref.at[b, :, g, :], osem.at[1])]

    @pl.when(qi == 0)
    def _():
        for c in kv_out_copies():
            c.start()

    @pl.when(qi == pl.num_programs(2) - 1)
    def _():
        for c in kv_out_copies():
            c.wait()

    @pl.when(qi == 0)
    def _():
        k = k_ref[...]
        row_blk = lax.broadcasted_iota(jnp.int32, (seq, LANES), 0) // MOBA_BLOCK
        lane = lax.broadcasted_iota(jnp.int32, (seq, LANES), 1)
        ka_ref[:, 0:HEAD_DIM] = k.astype(BF16)
        ka_ref[:, HEAD_DIM:] = jnp.where(lane == row_blk, 1.0, 0.0).astype(BF16)
        va_ref[:, 0:HEAD_DIM] = v_ref[...].astype(BF16)
        va_ref[:, HEAD_DIM:] = jnp.ones((seq, LANES), BF16)
        kmean = jnp.mean(k.reshape(nb, MOBA_BLOCK, HEAD_DIM), axis=1)
        km_hi = kmean.astype(BF16)
        km_lo = (kmean - km_hi.astype(F32)).astype(BF16)
        kmean_ref[...] = jnp.zeros_like(kmean_ref)
        kmean_ref[0:nb, :] = jnp.concatenate([km_hi, km_hi, km_lo], axis=1)

    stream.reduce()

    q = q_ref[...]
    q4 = jnp.concatenate([q[:, h * HEAD_DIM:(h + 1) * HEAD_DIM] for h in range(KV_GROUP)], axis=0)
    qs = (q4 * (ATTN_SCALE * LOG2E)).astype(BF16)

    q_hi = q4.astype(BF16)
    q_lo = (q4 - q_hi.astype(F32)).astype(BF16)
    gate = lax.dot_general(jnp.concatenate([q_hi, q_lo, q_hi], axis=1), kmean_ref[...],
                           (((1,), (1,)), ((), ())), preferred_element_type=F32)
    lane_i = lax.broadcasted_iota(jnp.int32, (rows, LANES), 1)
    lane = lane_i.astype(F32)
    past = lane_i < qi
    gate = jnp.where(past, gate, -jnp.inf)
    bias = jnp.full((rows, LANES), MASK_BIAS, F32)
    for _ in range(MOBA_TOPK):
        top = jnp.max(gate, axis=1, keepdims=True)
        pick = lane == jnp.min(jnp.where(gate == top, lane, float(LANES)), axis=1, keepdims=True)
        bias = jnp.where(pick & past, 0.0, bias)
        gate = jnp.where(pick, -jnp.inf, gate)
    qa = jnp.concatenate([qs, bias.astype(BF16)], axis=1)

    own = pl.multiple_of(qi * MOBA_BLOCK, MOBA_BLOCK)
    s = lax.dot_general(qs, ka_ref[pl.ds(own, MOBA_BLOCK), 0:HEAD_DIM], (((1,), (1,)), ((), ())),
                        preferred_element_type=F32)
    qrow = lax.broadcasted_iota(jnp.int32, (rows, MOBA_BLOCK), 0) & (MOBA_BLOCK - 1)
    kcol = lax.broadcasted_iota(jnp.int32, (rows, MOBA_BLOCK), 1)
    s = jnp.where(kcol <= qrow, s, MASK_BIAS)
    m0 = jnp.broadcast_to(jnp.max(s, axis=1, keepdims=True), (rows, LANES))
    p = jnp.concatenate([jnp.exp2(s[:, c * LANES:(c + 1) * LANES] - m0)
                         for c in range(MOBA_BLOCK // LANES)], axis=1)
    m_ref[...] = m0
    acc_ref[...] = jnp.dot(p.astype(BF16), va_ref[pl.ds(own, MOBA_BLOCK), :], preferred_element_type=F32)

    qa_ref[...] = qa

    def past_pair(t, carry):
        start = pl.multiple_of(t * pair, pair)
        groups = [slice(r0, r0 + ROW_GROUP) for r0 in range(0, rows, ROW_GROUP)]

        def scores(rs):
            return lax.dot_general(qa_ref[rs, :], ka_ref[pl.ds(start, pair), :], (((1,), (1,)), ((), ())),
                                   preferred_element_type=F32)

        m_olds = [m_ref[rs, :] for rs in groups]
        s_next = scores(groups[0])
        updates = []
        for gi, rs in enumerate(groups):
            s = s_next
            if gi + 1 < len(groups):
                s_next = scores(groups[gi + 1])
            m_old = m_olds[gi]
            m_new = jnp.maximum(m_old, jnp.max(s, axis=1, keepdims=True))
            alpha = jnp.exp2(m_old - m_new)
            p = jnp.concatenate([jnp.exp2(s[:, c * LANES:(c + 1) * LANES] - m_new)
                                 for c in range(pair // LANES)], axis=1)
            pv = jnp.dot(p.astype(BF16), va_ref[pl.ds(start, pair), :], preferred_element_type=F32)
            updates.append((rs, m_new, alpha, pv))
        for rs, m_new, alpha, pv in updates:
            for c in range(2):
                cs = slice(c * LANES, (c + 1) * LANES)
                acc_ref[rs, cs] = alpha * acc_ref[rs, cs] + pv[:, cs]
            m_ref[rs, :] = m_new
        return carry

    lax.fori_loop(0, (qi + 1) // 2, past_pair, 0)

    o = acc_ref[:, 0:HEAD_DIM] / acc_ref[:, HEAD_DIM:]
    for h in range(KV_GROUP):
        o_ref[:, h * HEAD_DIM:(h + 1) * HEAD_DIM] = o[h * MOBA_BLOCK:(h + 1) * MOBA_BLOCK].astype(o_ref.dtype)


def _attn_prompt(proj, batch, seq, k_col, v_col, cache_k3, page_table):
    nq = seq // MOBA_BLOCK
    assert seq % (2 * MOBA_BLOCK) == 0 and MOBA_TOPK <= nq <= LANES
    gw = KV_GROUP * HEAD_DIM
    rows = KV_GROUP * MOBA_BLOCK
    kblk, vblk = k_col // HEAD_DIM, v_col // HEAD_DIM
    grid = (batch, N_KV_HEADS, nq)
    stream_kwargs, km_spec, km_shape, stream_scratch = _key_stream_specs(
        cache_k3, page_table, batch * N_KV_HEADS * nq, lambda b, g, i: (b * N_KV_HEADS + g) * nq + i)
    grid_spec = pltpu.PrefetchScalarGridSpec(
        num_scalar_prefetch=1,
        grid=grid,
        in_specs=[
            pl.BlockSpec((MOBA_BLOCK, gw), lambda b, g, i, pt: (b * nq + i, g)),
            pl.BlockSpec((seq, HEAD_DIM), lambda b, g, i, pt: (b, kblk + g)),
            pl.BlockSpec((seq, HEAD_DIM), lambda b, g, i, pt: (b, vblk + g)),
            pl.BlockSpec(memory_space=pl.ANY),
        ],
        out_specs=[pl.BlockSpec((MOBA_BLOCK, gw), lambda b, g, i, pt: (b * nq + i, g)), km_spec,
                   pl.BlockSpec(memory_space=pl.ANY), pl.BlockSpec(memory_space=pl.ANY)],
        scratch_shapes=[
            pltpu.VMEM((seq, HEAD_DIM + LANES), BF16),
            pltpu.VMEM((seq, HEAD_DIM + LANES), BF16),
            pltpu.VMEM((LANES, 3 * HEAD_DIM), BF16),
            pltpu.VMEM((rows, HEAD_DIM + LANES), BF16),
            pltpu.VMEM((rows, LANES), F32),
            pltpu.VMEM((rows, HEAD_DIM + LANES), F32),
        ] + stream_scratch + [pltpu.SemaphoreType.DMA((2,))],
    )
    kv_shape = jax.ShapeDtypeStruct((batch, seq, N_KV_HEADS, HEAD_DIM), proj.dtype)
    return pl.pallas_call(
        functools.partial(_attn_prompt_kernel, **stream_kwargs),
        grid_spec=grid_spec,
        out_shape=[jax.ShapeDtypeStruct((batch * seq, N_HEADS * HEAD_DIM), BF16), km_shape, kv_shape, kv_shape],
        compiler_params=_cparams("attn_prompt", ("arbitrary", "arbitrary", "arbitrary")),
        name="attn_prompt",
    )(page_table, proj, proj, proj, cache_k3)


def _conv_seq_kernel(b_ref, c_ref, h_ref, cp_ref, hp_ref, st_ref, w_ref, cb_ref, ut_ref, *, seq):
    i = pl.program_id(0)
    tm = c_ref.shape[0]
    u = c_ref[...] * h_ref[...]
    prev = jnp.where((i * tm) % seq == 0, st_ref[0], cp_ref[...] * hp_ref[...])
    p1 = prev[SUBLANES - 1:SUBLANES, :]
    p2 = prev[SUBLANES - 2:SUBLANES - 1, :]
    row = lax.broadcasted_iota(jnp.int32, u.shape, 0)
    u1 = jnp.where(row == 0, p1, pltpu.roll(u, 1, axis=0))
    u2 = jnp.where(row == 0, p2, jnp.where(row == 1, p1, pltpu.roll(u, 2, axis=0)))
    w = w_ref[...]
    conv = w[0:1, :] * u2 + w[1:2, :] * u1 + w[2:3, :] * u
    cb_ref[...] = (b_ref[...] * conv).astype(BF16)
    ut_ref[0] = u[tm - SUBLANES:tm, :]


def _conv_seq(proj, state8, conv_w, seq, b_col, cw):
    m = proj.shape[0]
    tm = _row_tile(seq, 512)
    nt = m // tm
    cb = b_col // cw
    pstep = tm // SUBLANES
    return pl.pallas_call(
        functools.partial(_conv_seq_kernel, seq=seq),
        grid=(nt,),
        in_specs=[
            pl.BlockSpec((tm, cw), lambda i: (i, cb)),
            pl.BlockSpec((tm, cw), lambda i: (i, cb + 1)),
            pl.BlockSpec((tm, cw), lambda i: (i, cb + 2)),
            pl.BlockSpec((SUBLANES, cw), lambda i: (jnp.maximum(i * pstep - 1, 0), cb + 1)),
            pl.BlockSpec((SUBLANES, cw), lambda i: (jnp.maximum(i * pstep - 1, 0), cb + 2)),
            pl.BlockSpec((1, SUBLANES, cw), lambda i: ((i * tm) // seq, 0, 0)),
            pl.BlockSpec((CONV_K, cw), lambda i: (0, 0)),
        ],
        out_specs=[
            pl.BlockSpec((tm, cw), lambda i: (i, 0)),
            pl.BlockSpec((1, SUBLANES, cw), lambda i: (i, 0, 0)),
        ],
        out_shape=[
            jax.ShapeDtypeStruct((m, cw), BF16),
            jax.ShapeDtypeStruct((nt, SUBLANES, cw), F32),
        ],
        compiler_params=_cparams("conv_seq", ("parallel",)),
        name="conv_seq",
    )(proj, proj, proj, proj, proj, state8, conv_w)


def _conv_step_kernel(b_ref, c_ref, h_ref, st_ref, w_ref, cb_ref, u_ref):
    cw = c_ref.shape[1]
    u = c_ref[...] * h_ref[...]
    w = w_ref[...]
    conv = w[0:1, :] * st_ref[:, 0:cw] + w[1:2, :] * st_ref[:, cw:2 * cw] + w[2:3, :] * u
    cb_ref[...] = (b_ref[...] * conv).astype(BF16)
    u_ref[...] = u


def _conv_step(proj, state, conv_w, b_col, cw):
    m = proj.shape[0]
    cb = b_col // cw
    return pl.pallas_call(
        _conv_step_kernel,
        grid=(1,),
        in_specs=[
            pl.BlockSpec((m, cw), lambda i: (0, cb)),
            pl.BlockSpec((m, cw), lambda i: (0, cb + 1)),
            pl.BlockSpec((m, cw), lambda i: (0, cb + 2)),
            pl.BlockSpec((m, (CONV_K - 1) * cw), lambda i: (0, 0)),
            pl.BlockSpec((CONV_K, cw), lambda i: (0, 0)),
        ],
        out_specs=[pl.BlockSpec((m, cw), lambda i: (0, 0)), pl.BlockSpec((m, cw), lambda i: (0, 0))],
        out_shape=[jax.ShapeDtypeStruct((m, cw), BF16), jax.ShapeDtypeStruct((m, cw), F32)],
        compiler_params=_cparams("conv_step", ("arbitrary",)),
        name="conv_step",
    )(proj, proj, proj, state, conv_w)


def _merge_kernel(a_ref, cb_ref, ga_ref, gc_ref, wa_ref, wc_ref, o_ref, *wb_refs):
    ya = jnp.dot(a_ref[...], _mxu_weight(wa_ref, wb_refs[0:1]), preferred_element_type=F32)
    yc = jnp.dot(cb_ref[...], _mxu_weight(wc_ref, wb_refs[1:2]), preferred_element_type=F32)
    o_ref[...] = (jax.nn.sigmoid(ga_ref[...]) * ya + jax.nn.sigmoid(gc_ref[...]) * yc).astype(o_ref.dtype)


def _merge(a, cb, proj, w_attn_br, w_conv_br, ga_col, gc_col):
    m, qw = a.shape
    cw = cb.shape[1]
    d = w_attn_br.shape[1]
    tm, tn = _row_tile(m), COL_TILE
    ga, gc = ga_col // tn, gc_col // tn
    assert w_attn_br.dtype == w_conv_br.dtype
    wa_specs, wa_shapes = _weight_out(w_attn_br, (qw, tn), lambda i, j: (0, j))
    wc_specs, wc_shapes = _weight_out(w_conv_br, (cw, tn), lambda i, j: (0, j))
    assert not wa_specs or m == tm, "the bf16 weight copy is written once per weight tile"
    out = pl.pallas_call(
        _merge_kernel,
        grid=(m // tm, d // tn),
        in_specs=[
            pl.BlockSpec((tm, qw), lambda i, j: (i, 0)),
            pl.BlockSpec((tm, cw), lambda i, j: (i, 0)),
            pl.BlockSpec((tm, tn), lambda i, j: (i, ga + j)),
            pl.BlockSpec((tm, tn), lambda i, j: (i, gc + j)),
            pl.BlockSpec((qw, tn), lambda i, j: (0, j)),
            pl.BlockSpec((cw, tn), lambda i, j: (0, j)),
        ],
        out_specs=[pl.BlockSpec((tm, tn), lambda i, j: (i, j))] + wa_specs + wc_specs,
        out_shape=[jax.ShapeDtypeStruct((m, d), BF16)] + wa_shapes + wc_shapes,
        compiler_params=_cparams("merge", ("parallel", "parallel")),
        name="merge",
    )(a, cb, proj, proj, w_attn_br, w_conv_br)
    return out[0], ((out[1], out[2]) if wa_specs else (w_attn_br, w_conv_br))


def _outproj_kernel(m_ref, w_ref, x_ref, o_ref, *wb_refs):
    o_ref[...] = x_ref[...] + jnp.dot(m_ref[...], _mxu_weight(w_ref, wb_refs), preferred_element_type=F32)


def _outproj(merged, w_o, x):
    m, d = merged.shape
    n = w_o.shape[1]
    tm, tn = _row_tile(m, WIDE_ROW_TILE), COL_TILE
    wb_specs, wb_shapes = _weight_out(w_o, (d, tn), lambda i, j: (0, j))
    assert not wb_specs or m == tm, "the bf16 weight copy is written once per weight tile"
    out = pl.pallas_call(
        _outproj_kernel,
        grid=(m // tm, n // tn),
        in_specs=[
            pl.BlockSpec((tm, d), lambda i, j: (i, 0), pipeline_mode=pl.Buffered(1)),
            pl.BlockSpec((d, tn), lambda i, j: (0, j)),
            pl.BlockSpec((tm, tn), lambda i, j: (i, j)),
        ],
        out_specs=[pl.BlockSpec((tm, tn), lambda i, j: (i, j))] + wb_specs,
        out_shape=[jax.ShapeDtypeStruct((m, n), F32)] + wb_shapes,
        compiler_params=_cparams("outproj", ("parallel", "parallel")),
        name="outproj",
    )(merged, w_o, x)
    return out[0], (out[1] if wb_specs else w_o)


def _ffn_up_kernel(x_ref, g_ref, wg_ref, wu_ref, o_ref, *rest):
    *wb_refs, hn_ref = rest

    @pl.when(pl.program_id(1) == 0)
    def _():
        hn_ref[...] = _rmsnorm_f32(x_ref[...], g_ref[...]).astype(BF16)

    hn = hn_ref[...]
    gate = jnp.dot(hn, _mxu_weight(wg_ref, wb_refs[0:1]), preferred_element_type=F32)
    up = jnp.dot(hn, _mxu_weight(wu_ref, wb_refs[1:2]), preferred_element_type=F32)
    o_ref[...] = (gate * jax.nn.sigmoid(gate) * up).astype(o_ref.dtype)


def _ffn_up(x, ln2, w_gate, w_up):
    m, d = x.shape
    f = w_gate.shape[1]
    tm, tn = _row_tile(m), COL_TILE
    assert f % tn == 0 and w_gate.dtype == w_up.dtype
    wg_specs, wg_shapes = _weight_out(w_gate, (d, tn), lambda i, j: (0, j))
    wu_specs, wu_shapes = _weight_out(w_up, (d, tn), lambda i, j: (0, j))
    assert not wg_specs or m == tm, "the bf16 weight copy is written once per weight tile"
    out = pl.pallas_call(
        _ffn_up_kernel,
        grid=(m // tm, f // tn),
        in_specs=[
            pl.BlockSpec((tm, d), lambda i, j: (i, 0)),
            pl.BlockSpec((1, d), lambda i, j: (0, 0)),
            pl.BlockSpec((d, tn), lambda i, j: (0, j)),
            pl.BlockSpec((d, tn), lambda i, j: (0, j)),
        ],
        out_specs=[pl.BlockSpec((tm, tn), lambda i, j: (i, j))] + wg_specs + wu_specs,
        out_shape=[jax.ShapeDtypeStruct((m, f), BF16)] + wg_shapes + wu_shapes,
        scratch_shapes=[pltpu.VMEM((tm, d), BF16)],
        compiler_params=_cparams("ffn_up", ("parallel", "arbitrary")),
        name="ffn_up",
    )(x, ln2.reshape(1, d), w_gate, w_up)
    return out[0], ((out[1], out[2]) if wg_specs else (w_gate, w_up))


def _ffn_down_kernel(h_ref, w_ref, x_ref, g_ref, o_ref, *wb_refs):
    k = pl.program_id(1)

    @pl.when(k == 0)
    def _():
        o_ref[...] = x_ref[...]

    o_ref[...] += jnp.dot(h_ref[...], _mxu_weight(w_ref, wb_refs), preferred_element_type=F32)

    @pl.when(k == pl.num_programs(1) - 1)
    def _():
        o_ref[...] = _rmsnorm_f32(o_ref[...], g_ref[...])


def _ffn_down(hmid, w_down, x, ln_f):
    m, f = hmid.shape
    d = w_down.shape[1]
    tm, tk = _row_tile(m), COL_TILE
    wb_specs, wb_shapes = _weight_out(w_down, (tk, d), lambda i, k: (k, 0))
    assert not wb_specs or m == tm, "the bf16 weight copy is written once per weight tile"
    out = pl.pallas_call(
        _ffn_down_kernel,
        grid=(m // tm, f // tk),
        in_specs=[
            pl.BlockSpec((tm, tk), lambda i, k: (i, k)),
            pl.BlockSpec((tk, d), lambda i, k: (k, 0)),
            pl.BlockSpec((tm, d), lambda i, k: (i, 0)),
            pl.BlockSpec((1, d), lambda i, k: (0, 0)),
        ],
        out_specs=[pl.BlockSpec((tm, d), lambda i, k: (i, 0))] + wb_specs,
        out_shape=[jax.ShapeDtypeStruct((m, d), F32)] + wb_shapes,
        compiler_params=_cparams("ffn_down", ("parallel", "arbitrary")),
        name="ffn_down",
    )(hmid, w_down, x, ln_f.reshape(1, d))
    return out[0], (out[1] if wb_specs else w_down)


def _sample_gate_kernel(q_ref, km_ref, sel_ref):
    db, _, nb, _ = km_ref.shape
    head_group = lax.broadcasted_iota(jnp.int32, (N_HEADS, nb), 0) // KV_GROUP
    blk = lax.broadcasted_iota(jnp.int32, (N_HEADS, nb), 1).astype(F32)
    lane = lax.broadcasted_iota(jnp.int32, (N_HEADS, LANES), 1)

    def one_sequence(b, carry):
        q = q_ref[b]
        gate = jnp.zeros((N_HEADS, nb), F32)
        for g in range(N_KV_HEADS):
            gg = lax.dot_general(q, km_ref[b, g], (((1,), (1,)), ((), ())),
                                 precision=lax.Precision.HIGHEST, preferred_element_type=F32)
            gate = jnp.where(head_group == g, gg, gate)
        out = jnp.zeros((N_HEADS, LANES), F32)
        for r in range(MOBA_TOPK):
            top = jnp.max(gate, axis=1, keepdims=True)
            first = jnp.min(jnp.where(gate == top, blk, float(nb)), axis=1, keepdims=True)
            out = jnp.where(lane == r, first, out)
            gate = jnp.where(blk == first, -jnp.inf, gate)
        sel_ref[b] = out.astype(jnp.int32)
        return carry

    lax.fori_loop(0, db, one_sequence, 0, unroll=2)


def _sample_gate(q_s, kmean):
    db, _, nb, _ = kmean.shape
    assert MOBA_TOPK <= nb
    sel = pl.pallas_call(
        _sample_gate_kernel,
        grid=(1,),
        in_specs=[pl.BlockSpec((db, N_HEADS, HEAD_DIM), lambda i: (0, 0, 0)),
                  pl.BlockSpec((db, N_KV_HEADS, nb, HEAD_DIM), lambda i: (0, 0, 0, 0))],
        out_specs=pl.BlockSpec((db, N_HEADS, LANES), lambda i: (0, 0, 0)),
        out_shape=jax.ShapeDtypeStruct((db, N_HEADS, LANES), jnp.int32),
        compiler_params=_cparams("sample_gate", ("arbitrary",)),
        name="sample_gate",
    )(q_s, kmean)
    return sel[:, :, :MOBA_TOPK]


def _sample_attn_kernel(pt_ref, sel_ref, q_ref, ks_ref, vs_ref, kc_ref, vc_ref, o_ref, kbuf, vbuf, sem, *,
                        pages_per_block):
    b = pl.program_id(0)
    slot = b % 2
    _, n_heads, n_sel, page, hd = kbuf.shape

    def head_copies(bb, h, sl):
        g = h // KV_GROUP
        out = []
        for t in range(n_sel):
            blk = sel_ref[bb, h * MOBA_TOPK + t // pages_per_block]
            pg = pt_ref[bb, blk * pages_per_block + t % pages_per_block]
            out.append(pltpu.make_async_copy(kc_ref.at[pg, :, g, :], kbuf.at[sl, h, t], sem.at[0, sl, h]))
            out.append(pltpu.make_async_copy(vc_ref.at[pg, :, g, :], vbuf.at[sl, h, t], sem.at[1, sl, h]))
        return out

    def start_all(bb, sl):
        def start_head(h, carry):
            for c in head_copies(bb, h, sl):
                c.start()
            return carry
        lax.fori_loop(0, n_heads, start_head, 0)

    @pl.when(b == 0)
    def _():
        start_all(b, slot)

    @pl.when(b + 1 < pl.num_programs(0))
    def _():
        start_all(b + 1, 1 - slot)

    ones = jnp.ones((2 * hd, LANES), BF16)

    def one_head(h):
        g = h // KV_GROUP
        q = q_ref[0, pl.ds(h, 1), :] * ATTN_SCALE
        prod = kbuf[slot, h].reshape(n_sel * page, hd) * q
        hi = prod.astype(BF16)
        lo = (prod - hi.astype(F32)).astype(BF16)
        s = jnp.dot(jnp.concatenate([hi, lo], axis=1), ones, preferred_element_type=F32)
        s_new = jnp.sum(ks_ref[0, pl.ds(g, 1), :] * q, axis=1, keepdims=True)
        m = jnp.maximum(jnp.max(s, axis=0, keepdims=True), s_new)
        p = jnp.exp(s - m)
        p_new = jnp.exp(s_new - m)
        l = jnp.sum(p, axis=0, keepdims=True) + p_new
        o = jnp.sum(p * vbuf[slot, h].reshape(n_sel * page, hd), axis=0, keepdims=True)
        o = o + p_new * vs_ref[0, pl.ds(g, 1), :]
        o_ref[0, pl.ds(h, 1), :] = (o / l).astype(o_ref.dtype)

    def head_group(g, carry):
        for j in range(KV_GROUP):
            for c in head_copies(b, g * KV_GROUP + j, slot):
                c.wait()
        for j in range(KV_GROUP):
            one_head(g * KV_GROUP + j)
        return carry

    lax.fori_loop(0, n_heads // KV_GROUP, head_group, 0)


def _sample_attn(q_s, k_s, v_s, cache_k4, cache_v4, page_table, sel):
    db, n_pages = page_table.shape
    page = cache_k4.shape[1]
    ppb = MOBA_BLOCK // page
    n_sel = MOBA_TOPK * ppb
    assert HEAD_DIM == LANES

    grid_spec = pltpu.PrefetchScalarGridSpec(
        num_scalar_prefetch=2,
        grid=(db,),
        in_specs=[
            pl.BlockSpec((1, N_HEADS, HEAD_DIM), lambda b, pt, sl: (b, 0, 0)),
            pl.BlockSpec((1, N_KV_HEADS, HEAD_DIM), lambda b, pt, sl: (b, 0, 0)),
            pl.BlockSpec((1, N_KV_HEADS, HEAD_DIM), lambda b, pt, sl: (b, 0, 0)),
            pl.BlockSpec(memory_space=pl.ANY),
            pl.BlockSpec(memory_space=pl.ANY),
        ],
        out_specs=pl.BlockSpec((1, N_HEADS, HEAD_DIM), lambda b, pt, sl: (b, 0, 0)),
        scratch_shapes=[
            pltpu.VMEM((2, N_HEADS, n_sel, page, HEAD_DIM), F32),
            pltpu.VMEM((2, N_HEADS, n_sel, page, HEAD_DIM), F32),
            pltpu.SemaphoreType.DMA((2, 2, N_HEADS)),
        ],
    )
    return pl.pallas_call(
        functools.partial(_sample_attn_kernel, pages_per_block=ppb),
        grid_spec=grid_spec,
        out_shape=jax.ShapeDtypeStruct((db, N_HEADS, HEAD_DIM), F32),
        compiler_params=_cparams("sample_attn", ("arbitrary",)),
        name="sample_attn",
    )(page_table, sel, q_s, k_s, v_s, cache_k4, cache_v4)


def _rope_tables(pos):
    half = HEAD_DIM // 2
    inv = 1.0 / (ROPE_THETA ** (jnp.arange(half, dtype=F32) / half))
    ang = pos.astype(F32)[:, None] * inv[None, :]
    cos, sin = jnp.cos(ang), jnp.sin(ang)
    return jnp.concatenate([cos, cos], axis=-1), jnp.concatenate([-sin, sin], axis=-1)


def _trunk_tail(x, a, cb, proj, w_attn_br, w_conv_br, w_o, ln2, w_ff_gate, w_ff_up, w_ff_down, ln_f,
                ga_col, gc_col):
    merged, (w_attn_br, w_conv_br) = _merge(a, cb, proj, w_attn_br, w_conv_br, ga_col, gc_col)
    x1, w_o = _outproj(merged, w_o, x)
    hmid, (w_ff_gate, w_ff_up) = _ffn_up(x1, ln2, w_ff_gate, w_ff_up)
    y, w_ff_down = _ffn_down(hmid, w_ff_down, x1, ln_f)
    return y, (w_attn_br, w_conv_br, w_o, ln2, w_ff_gate, w_ff_up, w_ff_down, ln_f)


def kernel(x_prompt, x_sample, cache_k, cache_v, state_conv, page_table, ln1, w_in, conv_w, w_attn_br,
           w_conv_br, w_o, ln2, w_ff_gate, w_ff_up, w_ff_down, ln_f):
    batch, seq, d = x_prompt.shape
    db, dec_seq, _ = x_sample.shape
    depth, n_phys, page, kvh, hd = cache_k.shape
    assert depth == 1 and dec_seq == 1 and kvh == N_KV_HEADS and hd == HEAD_DIM
    qw, kvw = N_HEADS * HEAD_DIM, N_KV_HEADS * HEAD_DIM
    cw = conv_w.shape[-1]
    k_col, v_col, b_col = qw, qw + kvw, qw + 2 * kvw
    ga_col = b_col + 3 * cw
    gc_col = ga_col + d
    rope_cols = qw + kvw
    past = page_table.shape[1] * page
    trunk_f32 = (w_attn_br[0], w_conv_br[0], w_o[0], ln2[0], w_ff_gate[0], w_ff_up[0], w_ff_down[0], ln_f)
    cache_k3 = cache_k.reshape(n_phys, page * N_KV_HEADS, HEAD_DIM)
    cache_k4 = cache_k.reshape(n_phys, page, N_KV_HEADS, HEAD_DIM)
    cache_v4 = cache_v.reshape(n_phys, page, N_KV_HEADS, HEAD_DIM)

    xs = x_sample.reshape(db, d)
    cos_s, sin_s = _rope_tables(jnp.full((db,), past, jnp.int32))
    proj_s, w_in_b = _inproj(xs, ln1[0], w_in[0], cos_s, sin_s, rope_cols)
    q_s = proj_s[:, :qw]
    k_s = proj_s[:, k_col:k_col + kvw]
    v_s = proj_s[:, v_col:v_col + kvw]
    mp = batch * seq
    xp = x_prompt.reshape(mp, d)
    cos_p, sin_p = _rope_tables(jnp.arange(seq, dtype=jnp.int32))
    proj_p, _ = _inproj(xp, ln1[0], w_in_b, cos_p, sin_p, rope_cols)

    a_p, kmean_s, k_p, v_p = _attn_prompt(proj_p, batch, seq, k_col, v_col, cache_k3, page_table)
    zero_state = jnp.zeros((batch, SUBLANES, cw), F32)
    cb_p, utail_p = _conv_seq(proj_p, zero_state, conv_w[0], seq, b_col, cw)
    tiles_per_seq = utail_p.shape[0] // batch
    conv_p = utail_p.reshape(batch, tiles_per_seq, SUBLANES, cw)[:, -1, SUBLANES - (CONV_K - 1):, :]
    sel = _sample_gate(q_s.reshape(db, N_HEADS, HEAD_DIM), kmean_s)
    a_s = _sample_attn(q_s.reshape(db, N_HEADS, HEAD_DIM), k_s.reshape(db, N_KV_HEADS, HEAD_DIM),
                       v_s.reshape(db, N_KV_HEADS, HEAD_DIM), cache_k4, cache_v4, page_table,
                       sel.reshape(db, N_HEADS * MOBA_TOPK)).reshape(db, qw).astype(BF16)
    state = state_conv[0].astype(F32)
    cb_s, u_s = _conv_step(proj_s, state.reshape(db, (CONV_K - 1) * cw), conv_w[0], b_col, cw)
    conv_s = jnp.concatenate([state[:, 1:, :], u_s[:, None, :]], axis=1)

    y_s, trunk_bf16 = _trunk_tail(xs, a_s, cb_s, proj_s, *trunk_f32, ga_col, gc_col)
    y_p, _ = _trunk_tail(xp, a_p, cb_p, proj_p, *trunk_bf16, ga_col, gc_col)

    return (
        y_p.reshape(batch, seq, d),
        y_s.reshape(db, 1, d),
        k_p[None],
        v_p[None],
        conv_p[None],
        k_s.reshape(1, db, 1, N_KV_HEADS, HEAD_DIM),
        v_s.reshape(1, db, 1, N_KV_HEADS, HEAD_DIM),
        conv_s[None],
    )
```

```python
import functools

import jax
import jax.numpy as jnp
from jax import lax
from jax.experimental import pallas as pl
from jax.experimental.pallas import tpu as pltpu

F32 = jnp.float32
BF16 = jnp.bfloat16

N_HEADS = 16
HEAD_DIM = 128
N_KV_HEADS = 4
KV_GROUP = N_HEADS // N_KV_HEADS
MOBA_BLOCK = 256
MOBA_TOPK = 3
ROPE_THETA = 10000.0
CONV_K = 3
RMS_EPS = 1e-6
ATTN_SCALE = HEAD_DIM ** -0.5
LOG2E = 1.4426950408889634
MASK_BIAS = -1e30

LANES = 128
SUBLANES = 8
MIB = 1 << 20
ROW_TILE = 1024
WIDE_ROW_TILE = 2048
COL_TILE = 512
PAGES_PER_STEP = 32
ROW_GROUP = 512


VMEM_LIMIT_MIB = {
    "inproj": 56, "attn_prompt": 48, "conv_seq": 32, "conv_step": 32, "merge": 48, "outproj": 48,
    "ffn_up": 48, "ffn_down": 48, "sample_gate": 32, "sample_attn": 40,
}


def _cparams(name, semantics):
    return pltpu.CompilerParams(dimension_semantics=semantics, vmem_limit_bytes=VMEM_LIMIT_MIB[name] * MIB)


def _row_tile(m, cap=ROW_TILE):
    t = min(m, cap)
    assert m % t == 0 and t % SUBLANES == 0, (m, t)
    return t


def _rmsnorm_f32(x, g):
    return x * lax.rsqrt(jnp.mean(x * x, axis=-1, keepdims=True) + RMS_EPS) * g


def _mxu_weight(w_ref, wb_refs):
    if not wb_refs:
        return w_ref[...]
    w = w_ref[...].astype(BF16)
    wb_refs[0][...] = w
    return w


def _weight_out(w, block_shape, index_map):
    if w.dtype == BF16:
        return [], []
    assert w.dtype == F32
    return [pl.BlockSpec(block_shape, index_map)], [jax.ShapeDtypeStruct(w.shape, BF16)]


def _inproj_kernel(x_ref, g_ref, w_ref, cos_ref, sin_ref, o_ref, *rest, rope_tiles):
    *wb_refs, xn_ref = rest
    j = pl.program_id(1)

    @pl.when(j == 0)
    def _():
        xn_ref[...] = _rmsnorm_f32(x_ref[...], g_ref[...]).astype(BF16)

    acc = jnp.dot(xn_ref[...], _mxu_weight(w_ref, wb_refs), preferred_element_type=F32)

    @pl.when(j < rope_tiles)
    def _():
        cos = cos_ref[...]
        sin = sin_ref[...]
        for c in range(acc.shape[1] // HEAD_DIM):
            a = acc[:, c * HEAD_DIM:(c + 1) * HEAD_DIM]
            o_ref[:, c * HEAD_DIM:(c + 1) * HEAD_DIM] = a * cos + pltpu.roll(a, HEAD_DIM // 2, axis=1) * sin

    @pl.when(j >= rope_tiles)
    def _():
        o_ref[...] = acc


def _block_means(get_page, n_pages, page_rows):
    pages_per_block = MOBA_BLOCK * N_KV_HEADS // page_rows
    blocks = n_pages // pages_per_block
    row = lax.broadcasted_iota(jnp.int32, (blocks, HEAD_DIM), 0)
    means = [jnp.zeros((blocks, HEAD_DIM), F32) for _ in range(N_KV_HEADS)]
    for r in range(blocks):
        tot = jnp.zeros((SUBLANES, HEAD_DIM), F32)
        for t in range(pages_per_block):
            x = get_page(r * pages_per_block + t)
            tot = tot + jnp.sum(x.reshape(page_rows // SUBLANES, SUBLANES, HEAD_DIM), axis=0)
        head_sum = tot[0:N_KV_HEADS, :]
        for c in range(1, SUBLANES // N_KV_HEADS):
            head_sum = head_sum + tot[c * N_KV_HEADS:(c + 1) * N_KV_HEADS, :]
        head_mean = head_sum * (1.0 / MOBA_BLOCK)
        for g in range(N_KV_HEADS):
            means[g] = jnp.where(row == r, head_mean[g:g + 1, :], means[g])
    return means


class _KeyStream:
    def __init__(self, pt_ref, kc_ref, km_ref, pbuf, psem, step, chunks_per_seq, n_chunks, every_step):
        self.pt_ref, self.kc_ref, self.km_ref, self.pbuf, self.psem = pt_ref, kc_ref, km_ref, pbuf, psem
        self.step, self.chunks_per_seq, self.n_chunks = step, chunks_per_seq, n_chunks
        self.every_step = every_step
        self.slot = step % 2
        self.pages = pbuf.shape[1]

    def _copies(self, s, sl):
        b, c = s // self.chunks_per_seq, s % self.chunks_per_seq
        return [pltpu.make_async_copy(self.kc_ref.at[self.pt_ref[b, c * self.pages + t]],
                                      self.pbuf.at[sl, t], self.psem.at[sl])
                for t in range(self.pages)]

    def prefetch(self):
        @pl.when(self.step == 0)
        def _():
            for c in self._copies(self.step, self.slot):
                c.start()

        @pl.when(self.step + 1 < self.n_chunks)
        def _():
            for c in self._copies(self.step + 1, 1 - self.slot):
                c.start()

    def reduce(self):
        def body():
            for c in self._copies(self.step, self.slot):
                c.wait()
            means = _block_means(lambda t: self.pbuf[self.slot, t], self.pages, self.pbuf.shape[2])
            for g in range(N_KV_HEADS):
                self.km_ref[0, g] = means[g]

        if self.every_step:
            body()
        else:
            pl.when(self.step < self.n_chunks)(body)


def _key_stream_specs(cache_k3, page_table, n_steps, step_of):
    db, n_pages = page_table.shape
    _, page_rows, hd = cache_k3.shape
    page = page_rows // N_KV_HEADS
    pages = PAGES_PER_STEP
    assert MOBA_BLOCK % page == 0 and n_pages % pages == 0 and SUBLANES % N_KV_HEADS == 0
    blocks = pages * page // MOBA_BLOCK
    assert blocks % SUBLANES == 0 and hd == HEAD_DIM
    chunks_per_seq = n_pages // pages
    n_chunks = db * chunks_per_seq
    nb = n_pages * page // MOBA_BLOCK
    assert n_chunks <= n_steps, "not enough host grid steps to stream the key cache"

    def km_map(*ids_and_pt):
        s = jnp.minimum(step_of(*ids_and_pt[:-1]), n_chunks - 1)
        return (s // chunks_per_seq, 0, s % chunks_per_seq, 0)

    return (dict(chunks_per_seq=chunks_per_seq, n_chunks=n_chunks, every_step=n_chunks == n_steps),
            pl.BlockSpec((1, N_KV_HEADS, blocks, HEAD_DIM), km_map),
            jax.ShapeDtypeStruct((db, N_KV_HEADS, nb, HEAD_DIM), F32),
            [pltpu.VMEM((2, pages, page_rows, HEAD_DIM), F32), pltpu.SemaphoreType.DMA((2,))])


def _inproj(x, ln1, w_in, cos, sin, rope_cols):
    m, d = x.shape
    n = w_in.shape[1]
    tm, tn = _row_tile(m, WIDE_ROW_TILE), COL_TILE
    assert n % tn == 0 and rope_cols % tn == 0
    assert cos.shape[0] % tm == 0 and m % cos.shape[0] == 0
    table_tiles = cos.shape[0] // tm
    wb_specs, wb_shapes = _weight_out(w_in, (d, tn), lambda i, j: (0, j))
    assert not wb_specs or m == tm, "the bf16 weight copy is written once per weight tile"
    out = pl.pallas_call(
        functools.partial(_inproj_kernel, rope_tiles=rope_cols // tn),
        grid=(m // tm, n // tn),
        in_specs=[
            pl.BlockSpec((tm, d), lambda i, j: (i, 0), pipeline_mode=pl.Buffered(1)),
            pl.BlockSpec((1, d), lambda i, j: (0, 0)),
            pl.BlockSpec((d, tn), lambda i, j: (0, j)),
            pl.BlockSpec((tm, HEAD_DIM), lambda i, j: (i % table_tiles, 0)),
            pl.BlockSpec((tm, HEAD_DIM), lambda i, j: (i % table_tiles, 0)),
        ],
        out_specs=[pl.BlockSpec((tm, tn), lambda i, j: (i, j))] + wb_specs,
        out_shape=[jax.ShapeDtypeStruct((m, n), F32)] + wb_shapes,
        scratch_shapes=[pltpu.VMEM((tm, d), BF16)],
        compiler_params=_cparams("inproj", ("parallel", "arbitrary")),
        name="inproj",
    )(x, ln1.reshape(1, d), w_in, cos, sin)
    return out[0], (out[1] if wb_specs else w_in)


def _attn_prompt_kernel(pt_ref, q_ref, k_ref, v_ref, kc_ref, o_ref, km_ref, ko_ref, vo_ref, ka_ref, va_ref,
                        kmean_ref, qa_ref, m_ref, acc_ref, pbuf, psem, osem, *, chunks_per_seq, n_chunks,
                        every_step):
    qi = pl.program_id(2)
    seq = k_ref.shape[0]
    nb = seq // MOBA_BLOCK
    rows = KV_GROUP * MOBA_BLOCK
    pair = 2 * MOBA_BLOCK
    step = (pl.program_id(0) * pl.num_programs(1) + pl.program_id(1)) * pl.num_programs(2) + qi
    stream = _KeyStream(pt_ref, kc_ref, km_ref, pbuf, psem, step, chunks_per_seq, n_chunks, every_step)
    stream.prefetch()

    def kv_out_copies():
        b, g = pl.program_id(0), pl.program_id(1)
        return [pltpu.make_async_copy(k_ref, ko_ref.at[b, :, g, :], osem.at[0]),
                pltpu.make_async_copy(v_ref, vo_ref.at[b, :, g, :], osem.at[1])]

    @pl.when(qi == 0)
    def _():
        for c in kv_out_copies():
            c.start()

    @pl.when(qi == pl.num_programs(2) - 1)
    def _():
        for c in kv_out_copies():
            c.wait()

    @pl.when(qi == 0)
    def _():
        k = k_ref[...]
        row_blk = lax.broadcasted_iota(jnp.int32, (seq, LANES), 0) // MOBA_BLOCK
        lane = lax.broadcasted_iota(jnp.int32, (seq, LANES), 1)
        ka_ref[:, 0:HEAD_DIM] = k.astype(BF16)
        ka_ref[:, HEAD_DIM:] = jnp.where(lane == row_blk, 1.0, 0.0).astype(BF16)
        va_ref[:, 0:HEAD_DIM] = v_ref[...].astype(BF16)
        va_ref[:, HEAD_DIM:] = jnp.ones((seq, LANES), BF16)
        kmean = jnp.mean(k.reshape(nb, MOBA_BLOCK, HEAD_DIM), axis=1)
        km_hi = kmean.astype(BF16)
        km_lo = (kmean - km_hi.astype(F32)).astype(BF16)
        kmean_ref[...] = jnp.zeros_like(kmean_ref)
        kmean_ref[0:nb, :] = jnp.concatenate([km_hi, km_hi, km_lo], axis=1)

    stream.reduce()

    q = q_ref[...]
    q4 = jnp.concatenate([q[:, h * HEAD_DIM:(h + 1) * HEAD_DIM] for h in range(KV_GROUP)], axis=0)
    qs = (q4 * (ATTN_SCALE * LOG2E)).astype(BF16)

    q_hi = q4.astype(BF16)
    q_lo = (q4 - q_hi.astype(F32)).astype(BF16)
    gate = lax.dot_general(jnp.concatenate([q_hi, q_lo, q_hi], axis=1), kmean_ref[...],
                           (((1,), (1,)), ((), ())), preferred_element_type=F32)
    lane_i = lax.broadcasted_iota(jnp.int32, (rows, LANES), 1)
    lane = lane_i.astype(F32)
    past = lane_i < qi
    gate = jnp.where(past, gate, -jnp.inf)
    bias = jnp.full((rows, LANES), MASK_BIAS, F32)
    for _ in range(MOBA_TOPK):
        top = jnp.max(gate, axis=1, keepdims=True)
        pick = lane == jnp.min(jnp.where(gate == top, lane, float(LANES)), axis=1, keepdims=True)
        bias = jnp.where(pick & past, 0.0, bias)
        gate = jnp.where(pick, -jnp.inf, gate)
    qa = jnp.concatenate([qs, bias.astype(BF16)], axis=1)

    own = pl.multiple_of(qi * MOBA_BLOCK, MOBA_BLOCK)
    s = lax.dot_general(qs, ka_ref[pl.ds(own, MOBA_BLOCK), 0:HEAD_DIM], (((1,), (1,)), ((), ())),
                        preferred_element_type=F32)
    qrow = lax.broadcasted_iota(jnp.int32, (rows, MOBA_BLOCK), 0) & (MOBA_BLOCK - 1)
    kcol = lax.broadcasted_iota(jnp.int32, (rows, MOBA_BLOCK), 1)
    s = jnp.where(kcol <= qrow, s, MASK_BIAS)
    m0 = jnp.broadcast_to(jnp.max(s, axis=1, keepdims=True), (rows, LANES))
    p = jnp.concatenate([jnp.exp2(s[:, c * LANES:(c + 1) * LANES] - m0)
                         for c in range(MOBA_BLOCK // LANES)], axis=1)
    m_ref[...] = m0
    acc_ref[...] = jnp.dot(p.astype(BF16), va_ref[pl.ds(own, MOBA_BLOCK), :], preferred_element_type=F32)

    qa_ref[...] = qa

    def past_pair(t):
        start = pl.multiple_of(t * pair, pair)
        groups = [slice(r0, r0 + ROW_GROUP) for r0 in range(0, rows, ROW_GROUP)]

        def scores(rs):
            return lax.dot_general(qa_ref[rs, :], ka_ref[pl.ds(start, pair), :], (((1,), (1,)), ((), ())),
                                   preferred_element_type=F32)

        m_olds = [m_ref[rs, :] for rs in groups]
        s_next = scores(groups[0])
        updates = []
        for gi, rs in enumerate(groups):
            s = s_next
            if gi + 1 < len(groups):
                s_next = scores(groups[gi + 1])
            m_old = m_olds[gi]
            m_new = jnp.maximum(m_old, jnp.max(s, axis=1, keepdims=True))
            alpha = jnp.exp2(m_old - m_new)
            p = jnp.concatenate([jnp.exp2(s[:, c * LANES:(c + 1) * LANES] - m_new)
                                 for c in range(pair // LANES)], axis=1)
            pv = jnp.dot(p.astype(BF16), va_ref[pl.ds(start, pair), :], preferred_element_type=F32)
            updates.append((rs, m_new, alpha, pv))
        for rs, m_new, alpha, pv in updates:
            for c in range(2):
                cs = slice(c * LANES, (c + 1) * LANES)
                acc_ref[rs, cs] = alpha * acc_ref[rs, cs] + pv[:, cs]
            m_ref[rs, :] = m_new

    def two_pairs(t, carry):
        past_pair(2 * t)
        past_pair(2 * t + 1)
        return carry

    n_pairs = (qi + 1) // 2
    lax.fori_loop(0, n_pairs // 2, two_pairs, 0)

    @pl.when(n_pairs % 2 == 1)
    def _():
        past_pair(n_pairs - 1)

    o = acc_ref[:, 0:HEAD_DIM] / acc_ref[:, HEAD_DIM:]
    for h in range(KV_GROUP):
        o_ref[:, h * HEAD_DIM:(h + 1) * HEAD_DIM] = o[h * MOBA_BLOCK:(h + 1) * MOBA_BLOCK].astype(o_ref.dtype)


def _attn_prompt(proj, batch, seq, k_col, v_col, cache_k3, page_table):
    nq = seq // MOBA_BLOCK
    assert seq % (2 * MOBA_BLOCK) == 0 and MOBA_TOPK <= nq <= LANES
    gw = KV_GROUP * HEAD_DIM
    rows = KV_GROUP * MOBA_BLOCK
    kblk, vblk = k_col // HEAD_DIM, v_col // HEAD_DIM
    grid = (batch, N_KV_HEADS, nq)
    stream_kwargs, km_spec, km_shape, stream_scratch = _key_stream_specs(
        cache_k3, page_table, batch * N_KV_HEADS * nq, lambda b, g, i: (b * N_KV_HEADS + g) * nq + i)
    grid_spec = pltpu.PrefetchScalarGridSpec(
        num_scalar_prefetch=1,
        grid=grid,
        in_specs=[
            pl.BlockSpec((MOBA_BLOCK, gw), lambda b, g, i, pt: (b * nq + i, g)),
            pl.BlockSpec((seq, HEAD_DIM), lambda b, g, i, pt: (b, kblk + g)),
            pl.BlockSpec((seq, HEAD_DIM), lambda b, g, i, pt: (b, vblk + g)),
            pl.BlockSpec(memory_space=pl.ANY),
        ],
        out_specs=[pl.BlockSpec((MOBA_BLOCK, gw), lambda b, g, i, pt: (b * nq + i, g)), km_spec,
                   pl.BlockSpec(memory_space=pl.ANY), pl.BlockSpec(memory_space=pl.ANY)],
        scratch_shapes=[
            pltpu.VMEM((seq, HEAD_DIM + LANES), BF16),
            pltpu.VMEM((seq, HEAD_DIM + LANES), BF16),
            pltpu.VMEM((LANES, 3 * HEAD_DIM), BF16),
            pltpu.VMEM((rows, HEAD_DIM + LANES), BF16),
            pltpu.VMEM((rows, LANES), F32),
            pltpu.VMEM((rows, HEAD_DIM + LANES), F32),
        ] + stream_scratch + [pltpu.SemaphoreType.DMA((2,))],
    )
    kv_shape = jax.ShapeDtypeStruct((batch, seq, N_KV_HEADS, HEAD_DIM), proj.dtype)
    return pl.pallas_call(
        functools.partial(_attn_prompt_kernel, **stream_kwargs),
        grid_spec=grid_spec,
        out_shape=[jax.ShapeDtypeStruct((batch * seq, N_HEADS * HEAD_DIM), BF16), km_shape, kv_shape, kv_shape],
        compiler_params=_cparams("attn_prompt", ("arbitrary", "arbitrary", "arbitrary")),
        name="attn_prompt",
    )(page_table, proj, proj, proj, cache_k3)


def _conv_seq_kernel(b_ref, c_ref, h_ref, cp_ref, hp_ref, st_ref, w_ref, cb_ref, ut_ref, *, seq):
    i = pl.program_id(0)
    tm = c_ref.shape[0]
    u = c_ref[...] * h_ref[...]
    prev = jnp.where((i * tm) % seq == 0, st_ref[0], cp_ref[...] * hp_ref[...])
    p1 = prev[SUBLANES - 1:SUBLANES, :]
    p2 = prev[SUBLANES - 2:SUBLANES - 1, :]
    row = lax.broadcasted_iota(jnp.int32, u.shape, 0)
    u1 = jnp.where(row == 0, p1, pltpu.roll(u, 1, axis=0))
    u2 = jnp.where(row == 0, p2, jnp.where(row == 1, p1, pltpu.roll(u, 2, axis=0)))
    w = w_ref[...]
    conv = w[0:1, :] * u2 + w[1:2, :] * u1 + w[2:3, :] * u
    cb_ref[...] = (b_ref[...] * conv).astype(BF16)
    ut_ref[0] = u[tm - SUBLANES:tm, :]


def _conv_seq(proj, state8, conv_w, seq, b_col, cw):
    m = proj.shape[0]
    tm = _row_tile(seq, 512)
    nt = m // tm
    cb = b_col // cw
    pstep = tm // SUBLANES
    return pl.pallas_call(
        functools.partial(_conv_seq_kernel, seq=seq),
        grid=(nt,),
        in_specs=[
            pl.BlockSpec((tm, cw), lambda i: (i, cb)),
            pl.BlockSpec((tm, cw), lambda i: (i, cb + 1)),
            pl.BlockSpec((tm, cw), lambda i: (i, cb + 2)),
            pl.BlockSpec((SUBLANES, cw), lambda i: (jnp.maximum(i * pstep - 1, 0), cb + 1)),
            pl.BlockSpec((SUBLANES, cw), lambda i: (jnp.maximum(i * pstep - 1, 0), cb + 2)),
            pl.BlockSpec((1, SUBLANES, cw), lambda i: ((i * tm) // seq, 0, 0)),
            pl.BlockSpec((CONV_K, cw), lambda i: (0, 0)),
        ],
        out_specs=[
            pl.BlockSpec((tm, cw), lambda i: (i, 0)),
            pl.BlockSpec((1, SUBLANES, cw), lambda i: (i, 0, 0)),
        ],
        out_shape=[
            jax.ShapeDtypeStruct((m, cw), BF16),
            jax.ShapeDtypeStruct((nt, SUBLANES, cw), F32),
        ],
        compiler_params=_cparams("conv_seq", ("parallel",)),
        name="conv_seq",
    )(proj, proj, proj, proj, proj, state8, conv_w)


def _conv_step_kernel(b_ref, c_ref, h_ref, st_ref, w_ref, cb_ref, u_ref):
    cw = c_ref.shape[1]
    u = c_ref[...] * h_ref[...]
    w = w_ref[...]
    conv = w[0:1, :] * st_ref[:, 0:cw] + w[1:2, :] * st_ref[:, cw:2 * cw] + w[2:3, :] * u
    cb_ref[...] = (b_ref[...] * conv).astype(BF16)
    u_ref[...] = u


def _conv_step(proj, state, conv_w, b_col, cw):
    m = proj.shape[0]
    cb = b_col // cw
    return pl.pallas_call(
        _conv_step_kernel,
        grid=(1,),
        in_specs=[
            pl.BlockSpec((m, cw), lambda i: (0, cb)),
            pl.BlockSpec((m, cw), lambda i: (0, cb + 1)),
            pl.BlockSpec((m, cw), lambda i: (0, cb + 2)),
            pl.BlockSpec((m, (CONV_K - 1) * cw), lambda i: (0, 0)),
            pl.BlockSpec((CONV_K, cw), lambda i: (0, 0)),
        ],
        out_specs=[pl.BlockSpec((m, cw), lambda i: (0, 0)), pl.BlockSpec((m, cw), lambda i: (0, 0))],
        out_shape=[jax.ShapeDtypeStruct((m, cw), BF16), jax.ShapeDtypeStruct((m, cw), F32)],
        compiler_params=_cparams("conv_step", ("arbitrary",)),
        name="conv_step",
    )(proj, proj, proj, state, conv_w)


def _merge_kernel(a_ref, cb_ref, ga_ref, gc_ref, wa_ref, wc_ref, o_ref, *wb_refs):
    ya = jnp.dot(a_ref[...], _mxu_weight(wa_ref, wb_refs[0:1]), preferred_element_type=F32)
    yc = jnp.dot(cb_ref[...], _mxu_weight(wc_ref, wb_refs[1:2]), preferred_element_type=F32)
    o_ref[...] = (jax.nn.sigmoid(ga_ref[...]) * ya + jax.nn.sigmoid(gc_ref[...]) * yc).astype(o_ref.dtype)


def _merge(a, cb, proj, w_attn_br, w_conv_br, ga_col, gc_col):
    m, qw = a.shape
    cw = cb.shape[1]
    d = w_attn_br.shape[1]
    tm, tn = _row_tile(m), COL_TILE
    ga, gc = ga_col // tn, gc_col // tn
    assert w_attn_br.dtype == w_conv_br.dtype
    wa_specs, wa_shapes = _weight_out(w_attn_br, (qw, tn), lambda i, j: (0, j))
    wc_specs, wc_shapes = _weight_out(w_conv_br, (cw, tn), lambda i, j: (0, j))
    assert not wa_specs or m == tm, "the bf16 weight copy is written once per weight tile"
    out = pl.pallas_call(
        _merge_kernel,
        grid=(m // tm, d // tn),
        in_specs=[
            pl.BlockSpec((tm, qw), lambda i, j: (i, 0)),
            pl.BlockSpec((tm, cw), lambda i, j: (i, 0)),
            pl.BlockSpec((tm, tn), lambda i, j: (i, ga + j)),
            pl.BlockSpec((tm, tn), lambda i, j: (i, gc + j)),
            pl.BlockSpec((qw, tn), lambda i, j: (0, j)),
            pl.BlockSpec((cw, tn), lambda i, j: (0, j)),
        ],
        out_specs=[pl.BlockSpec((tm, tn), lambda i, j: (i, j))] + wa_specs + wc_specs,
        out_shape=[jax.ShapeDtypeStruct((m, d), BF16)] + wa_shapes + wc_shapes,
        compiler_params=_cparams("merge", ("parallel", "parallel")),
        name="merge",
    )(a, cb, proj, proj, w_attn_br, w_conv_br)
    return out[0], ((out[1], out[2]) if wa_specs else (w_attn_br, w_conv_br))


def _outproj_kernel(m_ref, w_ref, x_ref, o_ref, *wb_refs):
    o_ref[...] = x_ref[...] + jnp.dot(m_ref[...], _mxu_weight(w_ref, wb_refs), preferred_element_type=F32)


def _outproj(merged, w_o, x):
    m, d = merged.shape
    n = w_o.shape[1]
    tm, tn = _row_tile(m, WIDE_ROW_TILE), COL_TILE
    wb_specs, wb_shapes = _weight_out(w_o, (d, tn), lambda i, j: (0, j))
    assert not wb_specs or m == tm, "the bf16 weight copy is written once per weight tile"
    out = pl.pallas_call(
        _outproj_kernel,
        grid=(m // tm, n // tn),
        in_specs=[
            pl.BlockSpec((tm, d), lambda i, j: (i, 0), pipeline_mode=pl.Buffered(1)),
            pl.BlockSpec((d, tn), lambda i, j: (0, j)),
            pl.BlockSpec((tm, tn), lambda i, j: (i, j)),
        ],
        out_specs=[pl.BlockSpec((tm, tn), lambda i, j: (i, j))] + wb_specs,
        out_shape=[jax.ShapeDtypeStruct((m, n), F32)] + wb_shapes,
        compiler_params=_cparams("outproj", ("parallel", "parallel")),
        name="outproj",
    )(merged, w_o, x)
    return out[0], (out[1] if wb_specs else w_o)


def _ffn_up_kernel(x_ref, g_ref, wg_ref, wu_ref, o_ref, *rest):
    *wb_refs, hn_ref = rest

    @pl.when(pl.program_id(1) == 0)
    def _():
        hn_ref[...] = _rmsnorm_f32(x_ref[...], g_ref[...]).astype(BF16)

    hn = hn_ref[...]
    gate = jnp.dot(hn, _mxu_weight(wg_ref, wb_refs[0:1]), preferred_element_type=F32)
    up = jnp.dot(hn, _mxu_weight(wu_ref, wb_refs[1:2]), preferred_element_type=F32)
    o_ref[...] = (gate * jax.nn.sigmoid(gate) * up).astype(o_ref.dtype)


def _ffn_up(x, ln2, w_gate, w_up):
    m, d = x.shape
    f = w_gate.shape[1]
    tm, tn = _row_tile(m), COL_TILE
    assert f % tn == 0 and w_gate.dtype == w_up.dtype
    wg_specs, wg_shapes = _weight_out(w_gate, (d, tn), lambda i, j: (0, j))
    wu_specs, wu_shapes = _weight_out(w_up, (d, tn), lambda i, j: (0, j))
    assert not wg_specs or m == tm, "the bf16 weight copy is written once per weight tile"
    out = pl.pallas_call(
        _ffn_up_kernel,
        grid=(m // tm, f // tn),
        in_specs=[
            pl.BlockSpec((tm, d), lambda i, j: (i, 0)),
            pl.BlockSpec((1, d), lambda i, j: (0, 0)),
            pl.BlockSpec((d, tn), lambda i, j: (0, j)),
            pl.BlockSpec((d, tn), lambda i, j: (0, j)),
        ],
        out_specs=[pl.BlockSpec((tm, tn), lambda i, j: (i, j))] + wg_specs + wu_specs,
        out_shape=[jax.ShapeDtypeStruct((m, f), BF16)] + wg_shapes + wu_shapes,
        scratch_shapes=[pltpu.VMEM((tm, d), BF16)],
        compiler_params=_cparams("ffn_up", ("parallel", "arbitrary")),
        name="ffn_up",
    )(x, ln2.reshape(1, d), w_gate, w_up)
    return out[0], ((out[1], out[2]) if wg_specs else (w_gate, w_up))


def _ffn_down_kernel(h_ref, w_ref, x_ref, g_ref, o_ref, *wb_refs):
    k = pl.program_id(1)

    @pl.when(k == 0)
    def _():
        o_ref[...] = x_ref[...]

    o_ref[...] += jnp.dot(h_ref[...], _mxu_weight(w_ref, wb_refs), preferred_element_type=F32)

    @pl.when(k == pl.num_programs(1) - 1)
    def _():
        o_ref[...] = _rmsnorm_f32(o_ref[...], g_ref[...])


def _ffn_down(hmid, w_down, x, ln_f):
    m, f = hmid.shape
    d = w_down.shape[1]
    tm, tk = _row_tile(m), COL_TILE
    wb_specs, wb_shapes = _weight_out(w_down, (tk, d), lambda i, k: (k, 0))
    assert not wb_specs or m == tm, "the bf16 weight copy is written once per weight tile"
    out = pl.pallas_call(
        _ffn_down_kernel,
        grid=(m // tm, f // tk),
        in_specs=[
            pl.BlockSpec((tm, tk), lambda i, k: (i, k)),
            pl.BlockSpec((tk, d), lambda i, k: (k, 0)),
            pl.BlockSpec((tm, d), lambda i, k: (i, 0)),
            pl.BlockSpec((1, d), lambda i, k: (0, 0)),
        ],
        out_specs=[pl.BlockSpec((tm, d), lambda i, k: (i, 0))] + wb_specs,
        out_shape=[jax.ShapeDtypeStruct((m, d), F32)] + wb_shapes,
        compiler_params=_cparams("ffn_down", ("parallel", "arbitrary")),
        name="ffn_down",
    )(hmid, w_down, x, ln_f.reshape(1, d))
    return out[0], (out[1] if wb_specs else w_down)


def _sample_gate_kernel(q_ref, km_ref, sel_ref):
    db, _, nb, _ = km_ref.shape
    head_group = lax.broadcasted_iota(jnp.int32, (N_HEADS, nb), 0) // KV_GROUP
    blk = lax.broadcasted_iota(jnp.int32, (N_HEADS, nb), 1).astype(F32)
    lane = lax.broadcasted_iota(jnp.int32, (N_HEADS, LANES), 1)

    def one_sequence(b, carry):
        q = q_ref[b]
        gate = jnp.zeros((N_HEADS, nb), F32)
        for g in range(N_KV_HEADS):
            gg = lax.dot_general(q, km_ref[b, g], (((1,), (1,)), ((), ())),
                                 precision=lax.Precision.HIGHEST, preferred_element_type=F32)
            gate = jnp.where(head_group == g, gg, gate)
        out = jnp.zeros((N_HEADS, LANES), F32)
        for r in range(MOBA_TOPK):
            top = jnp.max(gate, axis=1, keepdims=True)
            first = jnp.min(jnp.where(gate == top, blk, float(nb)), axis=1, keepdims=True)
            out = jnp.where(lane == r, first, out)
            gate = jnp.where(blk == first, -jnp.inf, gate)
        sel_ref[b] = out.astype(jnp.int32)
        return carry

    lax.fori_loop(0, db, one_sequence, 0, unroll=2)


def _sample_gate(q_s, kmean):
    db, _, nb, _ = kmean.shape
    assert MOBA_TOPK <= nb
    sel = pl.pallas_call(
        _sample_gate_kernel,
        grid=(1,),
        in_specs=[pl.BlockSpec((db, N_HEADS, HEAD_DIM), lambda i: (0, 0, 0)),
                  pl.BlockSpec((db, N_KV_HEADS, nb, HEAD_DIM), lambda i: (0, 0, 0, 0))],
        out_specs=pl.BlockSpec((db, N_HEADS, LANES), lambda i: (0, 0, 0)),
        out_shape=jax.ShapeDtypeStruct((db, N_HEADS, LANES), jnp.int32),
        compiler_params=_cparams("sample_gate", ("arbitrary",)),
        name="sample_gate",
    )(q_s, kmean)
    return sel[:, :, :MOBA_TOPK]


def _sample_attn_kernel(pt_ref, sel_ref, q_ref, ks_ref, vs_ref, kc_ref, vc_ref, o_ref, kbuf, vbuf, sem, *,
                        pages_per_block):
    b = pl.program_id(0)
    slot = b % 2
    _, n_heads, n_sel, page, hd = kbuf.shape

    def head_copies(bb, h, sl):
        g = h // KV_GROUP
        out = []
        for t in range(n_sel):
            blk = sel_ref[bb, h * MOBA_TOPK + t // pages_per_block]
            pg = pt_ref[bb, blk * pages_per_block + t % pages_per_block]
            out.append(pltpu.make_async_copy(kc_ref.at[pg, :, g, :], kbuf.at[sl, h, t], sem.at[0, sl, h]))
            out.append(pltpu.make_async_copy(vc_ref.at[pg, :, g, :], vbuf.at[sl, h, t], sem.at[1, sl, h]))
        return out

    def start_all(bb, sl):
        def start_head(h, carry):
            for c in head_copies(bb, h, sl):
                c.start()
            return carry
        lax.fori_loop(0, n_heads, start_head, 0)

    @pl.when(b == 0)
    def _():
        start_all(b, slot)

    @pl.when(b + 1 < pl.num_programs(0))
    def _():
        start_all(b + 1, 1 - slot)

    ones = jnp.ones((2 * hd, LANES), BF16)

    def one_head(h):
        g = h // KV_GROUP
        q = q_ref[0, pl.ds(h, 1), :] * ATTN_SCALE
        prod = kbuf[slot, h].reshape(n_sel * page, hd) * q
        hi = prod.astype(BF16)
        lo = (prod - hi.astype(F32)).astype(BF16)
        s = jnp.dot(jnp.concatenate([hi, lo], axis=1), ones, preferred_element_type=F32)
        s_new = jnp.sum(ks_ref[0, pl.ds(g, 1), :] * q, axis=1, keepdims=True)
        m = jnp.maximum(jnp.max(s, axis=0, keepdims=True), s_new)
        p = jnp.exp(s - m)
        p_new = jnp.exp(s_new - m)
        l = jnp.sum(p, axis=0, keepdims=True) + p_new
        o = jnp.sum(p * vbuf[slot, h].reshape(n_sel * page, hd), axis=0, keepdims=True)
        o = o + p_new * vs_ref[0, pl.ds(g, 1), :]
        o_ref[0, pl.ds(h, 1), :] = (o / l).astype(o_ref.dtype)

    def head_group(g, carry):
        for j in range(KV_GROUP):
            for c in head_copies(b, g * KV_GROUP + j, slot):
                c.wait()
        for j in range(KV_GROUP):
            one_head(g * KV_GROUP + j)
        return carry

    lax.fori_loop(0, n_heads // KV_GROUP, head_group, 0)


def _sample_attn(q_s, k_s, v_s, cache_k4, cache_v4, page_table, sel):
    db, n_pages = page_table.shape
    page = cache_k4.shape[1]
    ppb = MOBA_BLOCK // page
    n_sel = MOBA_TOPK * ppb
    assert HEAD_DIM == LANES

    grid_spec = pltpu.PrefetchScalarGridSpec(
        num_scalar_prefetch=2,
        grid=(db,),
        in_specs=[
            pl.BlockSpec((1, N_HEADS, HEAD_DIM), lambda b, pt, sl: (b, 0, 0)),
            pl.BlockSpec((1, N_KV_HEADS, HEAD_DIM), lambda b, pt, sl: (b, 0, 0)),
            pl.BlockSpec((1, N_KV_HEADS, HEAD_DIM), lambda b, pt, sl: (b, 0, 0)),
            pl.BlockSpec(memory_space=pl.ANY),
            pl.BlockSpec(memory_space=pl.ANY),
        ],
        out_specs=pl.BlockSpec((1, N_HEADS, HEAD_DIM), lambda b, pt, sl: (b, 0, 0)),
        scratch_shapes=[
            pltpu.VMEM((2, N_HEADS, n_sel, page, HEAD_DIM), F32),
            pltpu.VMEM((2, N_HEADS, n_sel, page, HEAD_DIM), F32),
            pltpu.SemaphoreType.DMA((2, 2, N_HEADS)),
        ],
    )
    return pl.pallas_call(
        functools.partial(_sample_attn_kernel, pages_per_block=ppb),
        grid_spec=grid_spec,
        out_shape=jax.ShapeDtypeStruct((db, N_HEADS, HEAD_DIM), F32),
        compiler_params=_cparams("sample_attn", ("arbitrary",)),
        name="sample_attn",
    )(page_table, sel, q_s, k_s, v_s, cache_k4, cache_v4)


def _rope_tables(pos):
    half = HEAD_DIM // 2
    inv = 1.0 / (ROPE_THETA ** (jnp.arange(half, dtype=F32) / half))
    ang = pos.astype(F32)[:, None] * inv[None, :]
    cos, sin = jnp.cos(ang), jnp.sin(ang)
    return jnp.concatenate([cos, cos], axis=-1), jnp.concatenate([-sin, sin], axis=-1)


def _trunk_tail(x, a, cb, proj, w_attn_br, w_conv_br, w_o, ln2, w_ff_gate, w_ff_up, w_ff_down, ln_f,
                ga_col, gc_col):
    merged, (w_attn_br, w_conv_br) = _merge(a, cb, proj, w_attn_br, w_conv_br, ga_col, gc_col)
    x1, w_o = _outproj(merged, w_o, x)
    hmid, (w_ff_gate, w_ff_up) = _ffn_up(x1, ln2, w_ff_gate, w_ff_up)
    y, w_ff_down = _ffn_down(hmid, w_ff_down, x1, ln_f)
    return y, (w_attn_br, w_conv_br, w_o, ln2, w_ff_gate, w_ff_up, w_ff_down, ln_f)


def kernel(x_prompt, x_sample, cache_k, cache_v, state_conv, page_table, ln1, w_in, conv_w, w_attn_br,
           w_conv_br, w_o, ln2, w_ff_gate, w_ff_up, w_ff_down, ln_f):
    batch, seq, d = x_prompt.shape
    db, dec_seq, _ = x_sample.shape
    depth, n_phys, page, kvh, hd = cache_k.shape
    assert depth == 1 and dec_seq == 1 and kvh == N_KV_HEADS and hd == HEAD_DIM
    qw, kvw = N_HEADS * HEAD_DIM, N_KV_HEADS * HEAD_DIM
    cw = conv_w.shape[-1]
    k_col, v_col, b_col = qw, qw + kvw, qw + 2 * kvw
    ga_col = b_col + 3 * cw
    gc_col = ga_col + d
    rope_cols = qw + kvw
    past = page_table.shape[1] * page
    trunk_f32 = (w_attn_br[0], w_conv_br[0], w_o[0], ln2[0], w_ff_gate[0], w_ff_up[0], w_ff_down[0], ln_f)
    cache_k3 = cache_k.reshape(n_phys, page * N_KV_HEADS, HEAD_DIM)
    cache_k4 = cache_k.reshape(n_phys, page, N_KV_HEADS, HEAD_DIM)
    cache_v4 = cache_v.reshape(n_phys, page, N_KV_HEADS, HEAD_DIM)

    xs = x_sample.reshape(db, d)
    cos_s, sin_s = _rope_tables(jnp.full((db,), past, jnp.int32))
    proj_s, w_in_b = _inproj(xs, ln1[0], w_in[0], cos_s, sin_s, rope_cols)
    q_s = proj_s[:, :qw]
    k_s = proj_s[:, k_col:k_col + kvw]
    v_s = proj_s[:, v_col:v_col + kvw]
    mp = batch * seq
    xp = x_prompt.reshape(mp, d)
    cos_p, sin_p = _rope_tables(jnp.arange(seq, dtype=jnp.int32))
    proj_p, _ = _inproj(xp, ln1[0], w_in_b, cos_p, sin_p, rope_cols)

    a_p, kmean_s, k_p, v_p = _attn_prompt(proj_p, batch, seq, k_col, v_col, cache_k3, page_table)
    zero_state = jnp.zeros((batch, SUBLANES, cw), F32)
    cb_p, utail_p = _conv_seq(proj_p, zero_state, conv_w[0], seq, b_col, cw)
    tiles_per_seq = utail_p.shape[0] // batch
    conv_p = utail_p.reshape(batch, tiles_per_seq, SUBLANES, cw)[:, -1, SUBLANES - (CONV_K - 1):, :]
    sel = _sample_gate(q_s.reshape(db, N_HEADS, HEAD_DIM), kmean_s)
    a_s = _sample_attn(q_s.reshape(db, N_HEADS, HEAD_DIM), k_s.reshape(db, N_KV_HEADS, HEAD_DIM),
                       v_s.reshape(db, N_KV_HEADS, HEAD_DIM), cache_k4, cache_v4, page_table,
                       sel.reshape(db, N_HEADS * MOBA_TOPK)).reshape(db, qw).astype(BF16)
    state = state_conv[0].astype(F32)
    cb_s, u_s = _conv_step(proj_s, state.reshape(db, (CONV_K - 1) * cw), conv_w[0], b_col, cw)
    conv_s = jnp.concatenate([state[:, 1:, :], u_s[:, None, :]], axis=1)

    y_s, trunk_bf16 = _trunk_tail(xs, a_s, cb_s, proj_s, *trunk_f32, ga_col, gc_col)
    y_p, _ = _trunk_tail(xp, a_p, cb_p, proj_p, *trunk_bf16, ga_col, gc_col)

    return (
        y_p.reshape(batch, seq, d),
        y_s.reshape(db, 1, d),
        k_p[None],
        v_p[None],
        conv_p[None],
        k_s.reshape(1, db, 1, N_KV_HEADS, HEAD_DIM),
        v_s.reshape(1, db, 1, N_KV_HEADS, HEAD_DIM),
        conv_s[None],
    )
```

```python
import functools

import jax
import jax.numpy as jnp
from jax import lax
from jax.experimental import pallas as pl
from jax.experimental.pallas import tpu as pltpu

F32 = jnp.float32
BF16 = jnp.bfloat16

N_HEADS = 16
HEAD_DIM = 128
N_KV_HEADS = 4
KV_GROUP = N_HEADS // N_KV_HEADS
MOBA_BLOCK = 256
MOBA_TOPK = 3
ROPE_THETA = 10000.0
CONV_K = 3
RMS_EPS = 1e-6
ATTN_SCALE = HEAD_DIM ** -0.5
LOG2E = 1.4426950408889634
MASK_BIAS = -1e30

LANES = 128
SUBLANES = 8
MIB = 1 << 20
ROW_TILE = 1024
WIDE_ROW_TILE = 2048
COL_TILE = 512
PAGES_PER_STEP = 32
ROW_GROUP = 512


VMEM_LIMIT_MIB = {
    "inproj": 56, "attn_prompt": 48, "conv_seq": 32, "conv_step": 32, "merge": 48, "outproj": 48,
    "ffn_up": 48, "ffn_down": 48, "sample_gate": 32, "sample_attn": 40,
}


def _cparams(name, semantics):
    return pltpu.CompilerParams(dimension_semantics=semantics, vmem_limit_bytes=VMEM_LIMIT_MIB[name] * MIB)


def _row_tile(m, cap=ROW_TILE):
    t = min(m, cap)
    assert m % t == 0 and t % SUBLANES == 0, (m, t)
    return t


def _rmsnorm_f32(x, g):
    return x * lax.rsqrt(jnp.mean(x * x, axis=-1, keepdims=True) + RMS_EPS) * g


def _mxu_weight(w_ref, wb_refs):
    if not wb_refs:
        return w_ref[...]
    w = w_ref[...].astype(BF16)
    wb_refs[0][...] = w
    return w


def _weight_out(w, block_shape, index_map):
    if w.dtype == BF16:
        return [], []
    assert w.dtype == F32
    return [pl.BlockSpec(block_shape, index_map)], [jax.ShapeDtypeStruct(w.shape, BF16)]


def _inproj_kernel(x_ref, g_ref, w_ref, cos_ref, sin_ref, o_ref, *rest, rope_tiles):
    *wb_refs, xn_ref = rest
    j = pl.program_id(1)

    @pl.when(j == 0)
    def _():
        xn_ref[...] = _rmsnorm_f32(x_ref[...], g_ref[...]).astype(BF16)

    acc = jnp.dot(xn_ref[...], _mxu_weight(w_ref, wb_refs), preferred_element_type=F32)

    @pl.when(j < rope_tiles)
    def _():
        cos = cos_ref[...]
        sin = sin_ref[...]
        for c in range(acc.shape[1] // HEAD_DIM):
            a = acc[:, c * HEAD_DIM:(c + 1) * HEAD_DIM]
            o_ref[:, c * HEAD_DIM:(c + 1) * HEAD_DIM] = a * cos + pltpu.roll(a, HEAD_DIM // 2, axis=1) * sin

    @pl.when(j >= rope_tiles)
    def _():
        o_ref[...] = acc


def _block_means(get_page, n_pages, page_rows):
    pages_per_block = MOBA_BLOCK * N_KV_HEADS // page_rows
    blocks = n_pages // pages_per_block
    row = lax.broadcasted_iota(jnp.int32, (blocks, HEAD_DIM), 0)
    means = [jnp.zeros((blocks, HEAD_DIM), F32) for _ in range(N_KV_HEADS)]
    for r in range(blocks):
        tot = jnp.zeros((SUBLANES, HEAD_DIM), F32)
        for t in range(pages_per_block):
            x = get_page(r * pages_per_block + t)
            tot = tot + jnp.sum(x.reshape(page_rows // SUBLANES, SUBLANES, HEAD_DIM), axis=0)
        head_sum = tot[0:N_KV_HEADS, :]
        for c in range(1, SUBLANES // N_KV_HEADS):
            head_sum = head_sum + tot[c * N_KV_HEADS:(c + 1) * N_KV_HEADS, :]
        head_mean = head_sum * (1.0 / MOBA_BLOCK)
        for g in range(N_KV_HEADS):
            means[g] = jnp.where(row == r, head_mean[g:g + 1, :], means[g])
    return means


class _KeyStream:
    def __init__(self, pt_ref, kc_ref, km_ref, pbuf, psem, step, chunks_per_seq, n_chunks, every_step):
        self.pt_ref, self.kc_ref, self.km_ref, self.pbuf, self.psem = pt_ref, kc_ref, km_ref, pbuf, psem
        self.step, self.chunks_per_seq, self.n_chunks = step, chunks_per_seq, n_chunks
        self.every_step = every_step
        self.slot = step % 2
        self.pages = pbuf.shape[1]

    def _copies(self, s, sl):
        b, c = s // self.chunks_per_seq, s % self.chunks_per_seq
        return [pltpu.make_async_copy(self.kc_ref.at[self.pt_ref[b, c * self.pages + t]],
                                      self.pbuf.at[sl, t], self.psem.at[sl])
                for t in range(self.pages)]

    def prefetch(self):
        @pl.when(self.step == 0)
        def _():
            for c in self._copies(self.step, self.slot):
                c.start()

        @pl.when(self.step + 1 < self.n_chunks)
        def _():
            for c in self._copies(self.step + 1, 1 - self.slot):
                c.start()

    def reduce(self):
        def body():
            for c in self._copies(self.step, self.slot):
                c.wait()
            means = _block_means(lambda t: self.pbuf[self.slot, t], self.pages, self.pbuf.shape[2])
            for g in range(N_KV_HEADS):
                self.km_ref[0, g] = means[g]

        if self.every_step:
            body()
        else:
            pl.when(self.step < self.n_chunks)(body)


def _key_stream_specs(cache_k3, page_table, n_steps, step_of):
    db, n_pages = page_table.shape
    _, page_rows, hd = cache_k3.shape
    page = page_rows // N_KV_HEADS
    pages = PAGES_PER_STEP
    assert MOBA_BLOCK % page == 0 and n_pages % pages == 0 and SUBLANES % N_KV_HEADS == 0
    blocks = pages * page // MOBA_BLOCK
    assert blocks % SUBLANES == 0 and hd == HEAD_DIM
    chunks_per_seq = n_pages // pages
    n_chunks = db * chunks_per_seq
    nb = n_pages * page // MOBA_BLOCK
    assert n_chunks <= n_steps, "not enough host grid steps to stream the key cache"

    def km_map(*ids_and_pt):
        s = jnp.minimum(step_of(*ids_and_pt[:-1]), n_chunks - 1)
        return (s // chunks_per_seq, 0, s % chunks_per_seq, 0)

    return (dict(chunks_per_seq=chunks_per_seq, n_chunks=n_chunks, every_step=n_chunks == n_steps),
            pl.BlockSpec((1, N_KV_HEADS, blocks, HEAD_DIM), km_map),
            jax.ShapeDtypeStruct((db, N_KV_HEADS, nb, HEAD_DIM), F32),
            [pltpu.VMEM((2, pages, page_rows, HEAD_DIM), F32), pltpu.SemaphoreType.DMA((2,))])


def _inproj(x, ln1, w_in, cos, sin, rope_cols):
    m, d = x.shape
    n = w_in.shape[1]
    tm, tn = _row_tile(m, WIDE_ROW_TILE), COL_TILE
    assert n % tn == 0 and rope_cols % tn == 0
    assert cos.shape[0] % tm == 0 and m % cos.shape[0] == 0
    table_tiles = cos.shape[0] // tm
    wb_specs, wb_shapes = _weight_out(w_in, (d, tn), lambda i, j: (0, j))
    assert not wb_specs or m == tm, "the bf16 weight copy is written once per weight tile"
    out = pl.pallas_call(
        functools.partial(_inproj_kernel, rope_tiles=rope_cols // tn),
        grid=(m // tm, n // tn),
        in_specs=[
            pl.BlockSpec((tm, d), lambda i, j: (i, 0), pipeline_mode=pl.Buffered(1)),
            pl.BlockSpec((1, d), lambda i, j: (0, 0)),
            pl.BlockSpec((d, tn), lambda i, j: (0, j)),
            pl.BlockSpec((tm, HEAD_DIM), lambda i, j: (i % table_tiles, 0)),
            pl.BlockSpec((tm, HEAD_DIM), lambda i, j: (i % table_tiles, 0)),
        ],
        out_specs=[pl.BlockSpec((tm, tn), lambda i, j: (i, j))] + wb_specs,
        out_shape=[jax.ShapeDtypeStruct((m, n), F32)] + wb_shapes,
        scratch_shapes=[pltpu.VMEM((tm, d), BF16)],
        compiler_params=_cparams("inproj", ("parallel", "arbitrary")),
        name="inproj",
    )(x, ln1.reshape(1, d), w_in, cos, sin)
    return out[0], (out[1] if wb_specs else w_in)


def _attn_prompt_kernel(pt_ref, q_ref, k_ref, v_ref, kc_ref, o_ref, km_ref, ko_ref, vo_ref, ka_ref, va_ref,
                        kmean_ref, qa_ref, m_ref, acc_ref, pbuf, psem, osem, *, chunks_per_seq, n_chunks,
                        every_step):
    qi = pl.program_id(2)
    seq = k_ref.shape[0]
    nb = seq // MOBA_BLOCK
    rows = KV_GROUP * MOBA_BLOCK
    pair = 2 * MOBA_BLOCK
    step = (pl.program_id(0) * pl.num_programs(1) + pl.program_id(1)) * pl.num_programs(2) + qi
    stream = _KeyStream(pt_ref, kc_ref, km_ref, pbuf, psem, step, chunks_per_seq, n_chunks, every_step)
    stream.prefetch()

    def kv_out_copies():
        b, g = pl.program_id(0), pl.program_id(1)
        return [pltpu.make_async_copy(k_ref, ko_ref.at[b, :, g, :], osem.at[0]),
                pltpu.make_async_copy(v_ref, vo_ref.at[b, :, g, :], osem.at[1])]

    @pl.when(qi == 0)
    def _():
        for c in kv_out_copies():
            c.start()

    @pl.when(qi == pl.num_programs(2) - 1)
    def _():
        for c in kv_out_copies():
            c.wait()

    @pl.when(qi == 0)
    def _():
        k = k_ref[...]
        row_blk = lax.broadcasted_iota(jnp.int32, (seq, LANES), 0) // MOBA_BLOCK
        lane = lax.broadcasted_iota(jnp.int32, (seq, LANES), 1)
        ka_ref[:, 0:HEAD_DIM] = k.astype(BF16)
        ka_ref[:, HEAD_DIM:] = jnp.where(lane == row_blk, 1.0, 0.0).astype(BF16)
        va_ref[:, 0:HEAD_DIM] = v_ref[...].astype(BF16)
        va_ref[:, HEAD_DIM:] = jnp.ones((seq, LANES), BF16)
        kmean = jnp.mean(k.reshape(nb, MOBA_BLOCK, HEAD_DIM), axis=1)
        km_hi = kmean.astype(BF16)
        km_lo = (kmean - km_hi.astype(F32)).astype(BF16)
        kmean_ref[...] = jnp.zeros_like(kmean_ref)
        kmean_ref[0:nb, :] = jnp.concatenate([km_hi, km_hi, km_lo], axis=1)

    stream.reduce()

    q = q_ref[...]
    q4 = jnp.concatenate([q[:, h * HEAD_DIM:(h + 1) * HEAD_DIM] for h in range(KV_GROUP)], axis=0)
    qs = (q4 * (ATTN_SCALE * LOG2E)).astype(BF16)

    q_hi = q4.astype(BF16)
    q_lo = (q4 - q_hi.astype(F32)).astype(BF16)
    gate = lax.dot_general(jnp.concatenate([q_hi, q_lo, q_hi], axis=1), kmean_ref[...],
                           (((1,), (1,)), ((), ())), preferred_element_type=F32)
    lane_i = lax.broadcasted_iota(jnp.int32, (rows, LANES), 1)
    lane = lane_i.astype(F32)
    past = lane_i < qi
    gate = jnp.where(past, gate, -jnp.inf)
    bias = jnp.full((rows, LANES), MASK_BIAS, F32)
    for _ in range(MOBA_TOPK):
        top = jnp.max(gate, axis=1, keepdims=True)
        pick = lane == jnp.min(jnp.where(gate == top, lane, float(LANES)), axis=1, keepdims=True)
        bias = jnp.where(pick & past, 0.0, bias)
        gate = jnp.where(pick, -jnp.inf, gate)
    qa = jnp.concatenate([qs, bias.astype(BF16)], axis=1)

    own = pl.multiple_of(qi * MOBA_BLOCK, MOBA_BLOCK)
    s = lax.dot_general(qs, ka_ref[pl.ds(own, MOBA_BLOCK), 0:HEAD_DIM], (((1,), (1,)), ((), ())),
                        preferred_element_type=F32)
    qrow = lax.broadcasted_iota(jnp.int32, (rows, MOBA_BLOCK), 0) & (MOBA_BLOCK - 1)
    kcol = lax.broadcasted_iota(jnp.int32, (rows, MOBA_BLOCK), 1)
    s = jnp.where(kcol <= qrow, s, MASK_BIAS)
    m0 = jnp.broadcast_to(jnp.max(s, axis=1, keepdims=True), (rows, LANES))
    p = jnp.concatenate([jnp.exp2(s[:, c * LANES:(c + 1) * LANES] - m0)
                         for c in range(MOBA_BLOCK // LANES)], axis=1)
    m_ref[...] = m0
    acc_ref[...] = jnp.dot(p.astype(BF16), va_ref[pl.ds(own, MOBA_BLOCK), :], preferred_element_type=F32)

    qa_ref[...] = qa

    def past_pair(t):
        start = pl.multiple_of(t * pair, pair)
        groups = [slice(r0, r0 + ROW_GROUP) for r0 in range(0, rows, ROW_GROUP)]

        def scores(rs):
            return lax.dot_general(qa_ref[rs, :], ka_ref[pl.ds(start, pair), :], (((1,), (1,)), ((), ())),
                                   preferred_element_type=F32)

        m_olds = [m_ref[rs, :] for rs in groups]
        s_next = scores(groups[0])
        updates = []
        for gi, rs in enumerate(groups):
            s = s_next
            if gi + 1 < len(groups):
                s_next = scores(groups[gi + 1])
            m_old = m_olds[gi]
            m_new = jnp.maximum(m_old, jnp.max(s, axis=1, keepdims=True))
            alpha = jnp.exp2(m_old - m_new)
            p = jnp.concatenate([jnp.exp2(s[:, c * LANES:(c + 1) * LANES] - m_new)
                                 for c in range(pair // LANES)], axis=1)
            pv = jnp.dot(p.astype(BF16), va_ref[pl.ds(start, pair), :], preferred_element_type=F32)
            updates.append((rs, m_new, alpha, pv))
        for rs, m_new, alpha, pv in updates:
            for c in range(2):
                cs = slice(c * LANES, (c + 1) * LANES)
                acc_ref[rs, cs] = alpha * acc_ref[rs, cs] + pv[:, cs]
            m_ref[rs, :] = m_new

    def two_pairs(t, carry):
        past_pair(2 * t)
        past_pair(2 * t + 1)
        return carry

    n_pairs = (qi + 1) // 2
    lax.fori_loop(0, n_pairs // 2, two_pairs, 0)

    @pl.when(n_pairs % 2 == 1)
    def _():
        past_pair(n_pairs - 1)

    o = acc_ref[:, 0:HEAD_DIM] / acc_ref[:, HEAD_DIM:]
    for h in range(KV_GROUP):
        o_ref[:, h * HEAD_DIM:(h + 1) * HEAD_DIM] = o[h * MOBA_BLOCK:(h + 1) * MOBA_BLOCK].astype(o_ref.dtype)


def _attn_prompt(proj, batch, seq, k_col, v_col, cache_k3, page_table):
    nq = seq // MOBA_BLOCK
    assert seq % (2 * MOBA_BLOCK) == 0 and MOBA_TOPK <= nq <= LANES
    gw = KV_GROUP * HEAD_DIM
    rows = KV_GROUP * MOBA_BLOCK
    kblk, vblk = k_col // HEAD_DIM, v_col // HEAD_DIM
    grid = (batch, N_KV_HEADS, nq)
    stream_kwargs, km_spec, km_shape, stream_scratch = _key_stream_specs(
        cache_k3, page_table, batch * N_KV_HEADS * nq, lambda b, g, i: (b * N_KV_HEADS + g) * nq + i)
    grid_spec = pltpu.PrefetchScalarGridSpec(
        num_scalar_prefetch=1,
        grid=grid,
        in_specs=[
            pl.BlockSpec((MOBA_BLOCK, gw), lambda b, g, i, pt: (b * nq + i, g)),
            pl.BlockSpec((seq, HEAD_DIM), lambda b, g, i, pt: (b, kblk + g)),
            pl.BlockSpec((seq, HEAD_DIM), lambda b, g, i, pt: (b, vblk + g)),
            pl.BlockSpec(memory_space=pl.ANY),
        ],
        out_specs=[pl.BlockSpec((MOBA_BLOCK, gw), lambda b, g, i, pt: (b * nq + i, g)), km_spec,
                   pl.BlockSpec(memory_space=pl.ANY), pl.BlockSpec(memory_space=pl.ANY)],
        scratch_shapes=[
            pltpu.VMEM((seq, HEAD_DIM + LANES), BF16),
            pltpu.VMEM((seq, HEAD_DIM + LANES), BF16),
            pltpu.VMEM((LANES, 3 * HEAD_DIM), BF16),
            pltpu.VMEM((rows, HEAD_DIM + LANES), BF16),
            pltpu.VMEM((rows, LANES), F32),
            pltpu.VMEM((rows, HEAD_DIM + LANES), F32),
        ] + stream_scratch + [pltpu.SemaphoreType.DMA((2,))],
    )
    kv_shape = jax.ShapeDtypeStruct((batch, seq, N_KV_HEADS, HEAD_DIM), proj.dtype)
    return pl.pallas_call(
        functools.partial(_attn_prompt_kernel, **stream_kwargs),
        grid_spec=grid_spec,
        out_shape=[jax.ShapeDtypeStruct((batch * seq, N_HEADS * HEAD_DIM), BF16), km_shape, kv_shape, kv_shape],
        compiler_params=_cparams("attn_prompt", ("arbitrary", "arbitrary", "arbitrary")),
        name="attn_prompt",
    )(page_table, proj, proj, proj, cache_k3)


def _conv_seq_kernel(b_ref, c_ref, h_ref, cp_ref, hp_ref, st_ref, w_ref, cb_ref, ut_ref, *, seq):
    i = pl.program_id(0)
    tm = c_ref.shape[0]
    u = c_ref[...] * h_ref[...]
    prev = jnp.where((i * tm) % seq == 0, st_ref[0], cp_ref[...] * hp_ref[...])
    p1 = prev[SUBLANES - 1:SUBLANES, :]
    p2 = prev[SUBLANES - 2:SUBLANES - 1, :]
    row = lax.broadcasted_iota(jnp.int32, u.shape, 0)
    u1 = jnp.where(row == 0, p1, pltpu.roll(u, 1, axis=0))
    u2 = jnp.where(row == 0, p2, jnp.where(row == 1, p1, pltpu.roll(u, 2, axis=0)))
    w = w_ref[...]
    conv = w[0:1, :] * u2 + w[1:2, :] * u1 + w[2:3, :] * u
    cb_ref[...] = (b_ref[...] * conv).astype(BF16)
    ut_ref[0] = u[tm - SUBLANES:tm, :]


def _conv_seq(proj, state8, conv_w, seq, b_col, cw):
    m = proj.shape[0]
    tm = _row_tile(seq, 512)
    nt = m // tm
    cb = b_col // cw
    pstep = tm // SUBLANES
    return pl.pallas_call(
        functools.partial(_conv_seq_kernel, seq=seq),
        grid=(nt,),
        in_specs=[
            pl.BlockSpec((tm, cw), lambda i: (i, cb)),
            pl.BlockSpec((tm, cw), lambda i: (i, cb + 1)),
            pl.BlockSpec((tm, cw), lambda i: (i, cb + 2)),
            pl.BlockSpec((SUBLANES, cw), lambda i: (jnp.maximum(i * pstep - 1, 0), cb + 1)),
            pl.BlockSpec((SUBLANES, cw), lambda i: (jnp.maximum(i * pstep - 1, 0), cb + 2)),
            pl.BlockSpec((1, SUBLANES, cw), lambda i: ((i * tm) // seq, 0, 0)),
            pl.BlockSpec((CONV_K, cw), lambda i: (0, 0)),
        ],
        out_specs=[
            pl.BlockSpec((tm, cw), lambda i: (i, 0)),
            pl.BlockSpec((1, SUBLANES, cw), lambda i: (i, 0, 0)),
        ],
        out_shape=[
            jax.ShapeDtypeStruct((m, cw), BF16),
            jax.ShapeDtypeStruct((nt, SUBLANES, cw), F32),
        ],
        compiler_params=_cparams("conv_seq", ("parallel",)),
        name="conv_seq",
    )(proj, proj, proj, proj, proj, state8, conv_w)


def _conv_step_kernel(b_ref, c_ref, h_ref, st_ref, w_ref, cb_ref, u_ref):
    cw = c_ref.shape[1]
    u = c_ref[...] * h_ref[...]
    w = w_ref[...]
    conv = w[0:1, :] * st_ref[:, 0:cw] + w[1:2, :] * st_ref[:, cw:2 * cw] + w[2:3, :] * u
    cb_ref[...] = (b_ref[...] * conv).astype(BF16)
    u_ref[...] = u


def _conv_step(proj, state, conv_w, b_col, cw):
    m = proj.shape[0]
    cb = b_col // cw
    return pl.pallas_call(
        _conv_step_kernel,
        grid=(1,),
        in_specs=[
            pl.BlockSpec((m, cw), lambda i: (0, cb)),
            pl.BlockSpec((m, cw), lambda i: (0, cb + 1)),
            pl.BlockSpec((m, cw), lambda i: (0, cb + 2)),
            pl.BlockSpec((m, (CONV_K - 1) * cw), lambda i: (0, 0)),
            pl.BlockSpec((CONV_K, cw), lambda i: (0, 0)),
        ],
        out_specs=[pl.BlockSpec((m, cw), lambda i: (0, 0)), pl.BlockSpec((m, cw), lambda i: (0, 0))],
        out_shape=[jax.ShapeDtypeStruct((m, cw), BF16), jax.ShapeDtypeStruct((m, cw), F32)],
        compiler_params=_cparams("conv_step", ("arbitrary",)),
        name="conv_step",
    )(proj, proj, proj, state, conv_w)


def _merge_kernel(a_ref, cb_ref, ga_ref, gc_ref, wa_ref, wc_ref, o_ref, *wb_refs):
    ya = jnp.dot(a_ref[...], _mxu_weight(wa_ref, wb_refs[0:1]), preferred_element_type=F32)
    yc = jnp.dot(cb_ref[...], _mxu_weight(wc_ref, wb_refs[1:2]), preferred_element_type=F32)
    o_ref[...] = (jax.nn.sigmoid(ga_ref[...]) * ya + jax.nn.sigmoid(gc_ref[...]) * yc).astype(o_ref.dtype)


def _merge(a, cb, proj, w_attn_br, w_conv_br, ga_col, gc_col):
    m, qw = a.shape
    cw = cb.shape[1]
    d = w_attn_br.shape[1]
    tm, tn = _row_tile(m), COL_TILE
    ga, gc = ga_col // tn, gc_col // tn
    assert w_attn_br.dtype == w_conv_br.dtype
    wa_specs, wa_shapes = _weight_out(w_attn_br, (qw, tn), lambda i, j: (0, j))
    wc_specs, wc_shapes = _weight_out(w_conv_br, (cw, tn), lambda i, j: (0, j))
    assert not wa_specs or m == tm, "the bf16 weight copy is written once per weight tile"
    out = pl.pallas_call(
        _merge_kernel,
        grid=(m // tm, d // tn),
        in_specs=[
            pl.BlockSpec((tm, qw), lambda i, j: (i, 0)),
            pl.BlockSpec((tm, cw), lambda i, j: (i, 0)),
            pl.BlockSpec((tm, tn), lambda i, j: (i, ga + j)),
            pl.BlockSpec((tm, tn), lambda i, j: (i, gc + j)),
            pl.BlockSpec((qw, tn), lambda i, j: (0, j)),
            pl.BlockSpec((cw, tn), lambda i, j: (0, j)),
        ],
        out_specs=[pl.BlockSpec((tm, tn), lambda i, j: (i, j))] + wa_specs + wc_specs,
        out_shape=[jax.ShapeDtypeStruct((m, d), BF16)] + wa_shapes + wc_shapes,
        compiler_params=_cparams("merge", ("parallel", "parallel")),
        name="merge",
    )(a, cb, proj, proj, w_attn_br, w_conv_br)
    return out[0], ((out[1], out[2]) if wa_specs else (w_attn_br, w_conv_br))


def _outproj_kernel(m_ref, w_ref, x_ref, o_ref, *wb_refs):
    o_ref[...] = x_ref[...] + jnp.dot(m_ref[...], _mxu_weight(w_ref, wb_refs), preferred_element_type=F32)


def _outproj(merged, w_o, x):
    m, d = merged.shape
    n = w_o.shape[1]
    tm, tn = _row_tile(m, WIDE_ROW_TILE), COL_TILE
    wb_specs, wb_shapes = _weight_out(w_o, (d, tn), lambda i, j: (0, j))
    assert not wb_specs or m == tm, "the bf16 weight copy is written once per weight tile"
    out = pl.pallas_call(
        _outproj_kernel,
        grid=(m // tm, n // tn),
        in_specs=[
            pl.BlockSpec((tm, d), lambda i, j: (i, 0), pipeline_mode=pl.Buffered(1)),
            pl.BlockSpec((d, tn), lambda i, j: (0, j)),
            pl.BlockSpec((tm, tn), lambda i, j: (i, j)),
        ],
        out_specs=[pl.BlockSpec((tm, tn), lambda i, j: (i, j))] + wb_specs,
        out_shape=[jax.ShapeDtypeStruct((m, n), F32)] + wb_shapes,
        compiler_params=_cparams("outproj", ("parallel", "parallel")),
        name="outproj",
    )(merged, w_o, x)
    return out[0], (out[1] if wb_specs else w_o)


def _ffn_up_kernel(x_ref, g_ref, wg_ref, wu_ref, o_ref, *rest):
    *wb_refs, hn_ref = rest

    @pl.when(pl.program_id(1) == 0)
    def _():
        hn_ref[...] = _rmsnorm_f32(x_ref[...], g_ref[...]).astype(BF16)

    hn = hn_ref[...]
    gate = jnp.dot(hn, _mxu_weight(wg_ref, wb_refs[0:1]), preferred_element_type=F32)
    up = jnp.dot(hn, _mxu_weight(wu_ref, wb_refs[1:2]), preferred_element_type=F32)
    o_ref[...] = (gate * jax.nn.sigmoid(gate) * up).astype(o_ref.dtype)


def _ffn_up(x, ln2, w_gate, w_up):
    m, d = x.shape
    f = w_gate.shape[1]
    tm, tn = _row_tile(m), COL_TILE
    assert f % tn == 0 and w_gate.dtype == w_up.dtype
    wg_specs, wg_shapes = _weight_out(w_gate, (d, tn), lambda i, j: (0, j))
    wu_specs, wu_shapes = _weight_out(w_up, (d, tn), lambda i, j: (0, j))
    assert not wg_specs or m == tm, "the bf16 weight copy is written once per weight tile"
    out = pl.pallas_call(
        _ffn_up_kernel,
        grid=(m // tm, f // tn),
        in_specs=[
            pl.BlockSpec((tm, d), lambda i, j: (i, 0)),
            pl.BlockSpec((1, d), lambda i, j: (0, 0)),
            pl.BlockSpec((d, tn), lambda i, j: (0, j)),
            pl.BlockSpec((d, tn), lambda i, j: (0, j)),
        ],
        out_specs=[pl.BlockSpec((tm, tn), lambda i, j: (i, j))] + wg_specs + wu_specs,
        out_shape=[jax.ShapeDtypeStruct((m, f), BF16)] + wg_shapes + wu_shapes,
        scratch_shapes=[pltpu.VMEM((tm, d), BF16)],
        compiler_params=_cparams("ffn_up", ("parallel", "arbitrary")),
        name="ffn_up",
    )(x, ln2.reshape(1, d), w_gate, w_up)
    return out[0], ((out[1], out[2]) if wg_specs else (w_gate, w_up))


def _ffn_down_kernel(h_ref, w_ref, x_ref, g_ref, o_ref, *wb_refs):
    k = pl.program_id(1)

    @pl.when(k == 0)
    def _():
        o_ref[...] = x_ref[...]

    o_ref[...] += jnp.dot(h_ref[...], _mxu_weight(w_ref, wb_refs), preferred_element_type=F32)

    @pl.when(k == pl.num_programs(1) - 1)
    def _():
        o_ref[...] = _rmsnorm_f32(o_ref[...], g_ref[...])


def _ffn_down(hmid, w_down, x, ln_f):
    m, f = hmid.shape
    d = w_down.shape[1]
    tm, tk = _row_tile(m), COL_TILE
    wb_specs, wb_shapes = _weight_out(w_down, (tk, d), lambda i, k: (k, 0))
    assert not wb_specs or m == tm, "the bf16 weight copy is written once per weight tile"
    out = pl.pallas_call(
        _ffn_down_kernel,
        grid=(m // tm, f // tk),
        in_specs=[
            pl.BlockSpec((tm, tk), lambda i, k: (i, k)),
            pl.BlockSpec((tk, d), lambda i, k: (k, 0)),
            pl.BlockSpec((tm, d), lambda i, k: (i, 0)),
            pl.BlockSpec((1, d), lambda i, k: (0, 0)),
        ],
        out_specs=[pl.BlockSpec((tm, d), lambda i, k: (i, 0))] + wb_specs,
        out_shape=[jax.ShapeDtypeStruct((m, d), F32)] + wb_shapes,
        compiler_params=_cparams("ffn_down", ("parallel", "arbitrary")),
        name="ffn_down",
    )(hmid, w_down, x, ln_f.reshape(1, d))
    return out[0], (out[1] if wb_specs else w_down)


def _sample_gate_kernel(q_ref, km_ref, sel_ref):
    db, _, nb, _ = km_ref.shape
    head_group = lax.broadcasted_iota(jnp.int32, (N_HEADS, nb), 0) // KV_GROUP
    blk = lax.broadcasted_iota(jnp.int32, (N_HEADS, nb), 1).astype(F32)
    lane = lax.broadcasted_iota(jnp.int32, (N_HEADS, LANES), 1)

    def one_sequence(b, carry):
        q = q_ref[b]
        gate = jnp.zeros((N_HEADS, nb), F32)
        for g in range(N_KV_HEADS):
            gg = lax.dot_general(q, km_ref[b, g], (((1,), (1,)), ((), ())),
                                 precision=lax.Precision.HIGHEST, preferred_element_type=F32)
            gate = jnp.where(head_group == g, gg, gate)
        out = jnp.zeros((N_HEADS, LANES), F32)
        for r in range(MOBA_TOPK):
            top = jnp.max(gate, axis=1, keepdims=True)
            first = jnp.min(jnp.where(gate == top, blk, float(nb)), axis=1, keepdims=True)
            out = jnp.where(lane == r, first, out)
            gate = jnp.where(blk == first, -jnp.inf, gate)
        sel_ref[b] = out.astype(jnp.int32)
        return carry

    lax.fori_loop(0, db, one_sequence, 0, unroll=2)


def _sample_gate(q_s, kmean):
    db, _, nb, _ = kmean.shape
    assert MOBA_TOPK <= nb
    sel = pl.pallas_call(
        _sample_gate_kernel,
        grid=(1,),
        in_specs=[pl.BlockSpec((db, N_HEADS, HEAD_DIM), lambda i: (0, 0, 0)),
                  pl.BlockSpec((db, N_KV_HEADS, nb, HEAD_DIM), lambda i: (0, 0, 0, 0))],
        out_specs=pl.BlockSpec((db, N_HEADS, LANES), lambda i: (0, 0, 0)),
        out_shape=jax.ShapeDtypeStruct((db, N_HEADS, LANES), jnp.int32),
        compiler_params=_cparams("sample_gate", ("arbitrary",)),
        name="sample_gate",
    )(q_s, kmean)
    return sel[:, :, :MOBA_TOPK]


def _sample_attn_kernel(pt_ref, sel_ref, q_ref, ks_ref, vs_ref, kc_ref, vc_ref, o_ref, kbuf, vbuf, sem, *,
                        pages_per_block):
    b = pl.program_id(0)
    slot = b % 2
    _, n_heads, n_sel, page, hd = kbuf.shape

    def head_copies(bb, h, sl):
        g = h // KV_GROUP
        out = []
        for t in range(n_sel):
            blk = sel_ref[bb, h * MOBA_TOPK + t // pages_per_block]
            pg = pt_ref[bb, blk * pages_per_block + t % pages_per_block]
            out.append(pltpu.make_async_copy(kc_ref.at[pg, :, g, :], kbuf.at[sl, h, t], sem.at[0, sl, h]))
            out.append(pltpu.make_async_copy(vc_ref.at[pg, :, g, :], vbuf.at[sl, h, t], sem.at[1, sl, h]))
        return out

    def start_all(bb, sl):
        def start_head(h, carry):
            for c in head_copies(bb, h, sl):
                c.start()
            return carry
        lax.fori_loop(0, n_heads, start_head, 0)

    @pl.when(b == 0)
    def _():
        start_all(b, slot)

    @pl.when(b + 1 < pl.num_programs(0))
    def _():
        start_all(b + 1, 1 - slot)

    ones = jnp.ones((hd, LANES), BF16)

    def one_head(h):
        g = h // KV_GROUP
        q = q_ref[0, pl.ds(h, 1), :] * ATTN_SCALE
        prod = kbuf[slot, h].reshape(n_sel * page, hd) * q
        s = jnp.dot(prod.astype(BF16), ones, preferred_element_type=F32)
        s_new = jnp.sum(ks_ref[0, pl.ds(g, 1), :] * q, axis=1, keepdims=True)
        m = jnp.maximum(jnp.max(s, axis=0, keepdims=True), s_new)
        p = jnp.exp(s - m)
        p_new = jnp.exp(s_new - m)
        l = jnp.sum(p, axis=0, keepdims=True) + p_new
        o = jnp.sum(p * vbuf[slot, h].reshape(n_sel * page, hd), axis=0, keepdims=True)
        o = o + p_new * vs_ref[0, pl.ds(g, 1), :]
        o_ref[0, pl.ds(h, 1), :] = (o / l).astype(o_ref.dtype)

    def head_group(g, carry):
        for j in range(KV_GROUP):
            for c in head_copies(b, g * KV_GROUP + j, slot):
                c.wait()
        for j in range(KV_GROUP):
            one_head(g * KV_GROUP + j)
        return carry

    lax.fori_loop(0, n_heads // KV_GROUP, head_group, 0)


def _sample_attn(q_s, k_s, v_s, cache_k4, cache_v4, page_table, sel):
    db, n_pages = page_table.shape
    page = cache_k4.shape[1]
    ppb = MOBA_BLOCK // page
    n_sel = MOBA_TOPK * ppb
    assert HEAD_DIM == LANES

    grid_spec = pltpu.PrefetchScalarGridSpec(
        num_scalar_prefetch=2,
        grid=(db,),
        in_specs=[
            pl.BlockSpec((1, N_HEADS, HEAD_DIM), lambda b, pt, sl: (b, 0, 0)),
            pl.BlockSpec((1, N_KV_HEADS, HEAD_DIM), lambda b, pt, sl: (b, 0, 0)),
            pl.BlockSpec((1, N_KV_HEADS, HEAD_DIM), lambda b, pt, sl: (b, 0, 0)),
            pl.BlockSpec(memory_space=pl.ANY),
            pl.BlockSpec(memory_space=pl.ANY),
        ],
        out_specs=pl.BlockSpec((1, N_HEADS, HEAD_DIM), lambda b, pt, sl: (b, 0, 0)),
        scratch_shapes=[
            pltpu.VMEM((2, N_HEADS, n_sel, page, HEAD_DIM), F32),
            pltpu.VMEM((2, N_HEADS, n_sel, page, HEAD_DIM), F32),
            pltpu.SemaphoreType.DMA((2, 2, N_HEADS)),
        ],
    )
    return pl.pallas_call(
        functools.partial(_sample_attn_kernel, pages_per_block=ppb),
        grid_spec=grid_spec,
        out_shape=jax.ShapeDtypeStruct((db, N_HEADS, HEAD_DIM), F32),
        compiler_params=_cparams("sample_attn", ("arbitrary",)),
        name="sample_attn",
    )(page_table, sel, q_s, k_s, v_s, cache_k4, cache_v4)


def _rope_tables(pos):
    half = HEAD_DIM // 2
    inv = 1.0 / (ROPE_THETA ** (jnp.arange(half, dtype=F32) / half))
    ang = pos.astype(F32)[:, None] * inv[None, :]
    cos, sin = jnp.cos(ang), jnp.sin(ang)
    return jnp.concatenate([cos, cos], axis=-1), jnp.concatenate([-sin, sin], axis=-1)


def _trunk_tail(x, a, cb, proj, w_attn_br, w_conv_br, w_o, ln2, w_ff_gate, w_ff_up, w_ff_down, ln_f,
                ga_col, gc_col):
    merged, (w_attn_br, w_conv_br) = _merge(a, cb, proj, w_attn_br, w_conv_br, ga_col, gc_col)
    x1, w_o = _outproj(merged, w_o, x)
    hmid, (w_ff_gate, w_ff_up) = _ffn_up(x1, ln2, w_ff_gate, w_ff_up)
    y, w_ff_down = _ffn_down(hmid, w_ff_down, x1, ln_f)
    return y, (w_attn_br, w_conv_br, w_o, ln2, w_ff_gate, w_ff_up, w_ff_down, ln_f)


def kernel(x_prompt, x_sample, cache_k, cache_v, state_conv, page_table, ln1, w_in, conv_w, w_attn_br,
           w_conv_br, w_o, ln2, w_ff_gate, w_ff_up, w_ff_down, ln_f):
    batch, seq, d = x_prompt.shape
    db, dec_seq, _ = x_sample.shape
    depth, n_phys, page, kvh, hd = cache_k.shape
    assert depth == 1 and dec_seq == 1 and kvh == N_KV_HEADS and hd == HEAD_DIM
    qw, kvw = N_HEADS * HEAD_DIM, N_KV_HEADS * HEAD_DIM
    cw = conv_w.shape[-1]
    k_col, v_col, b_col = qw, qw + kvw, qw + 2 * kvw
    ga_col = b_col + 3 * cw
    gc_col = ga_col + d
    rope_cols = qw + kvw
    past = page_table.shape[1] * page
    trunk_f32 = (w_attn_br[0], w_conv_br[0], w_o[0], ln2[0], w_ff_gate[0], w_ff_up[0], w_ff_down[0], ln_f)
    cache_k3 = cache_k.reshape(n_phys, page * N_KV_HEADS, HEAD_DIM)
    cache_k4 = cache_k.reshape(n_phys, page, N_KV_HEADS, HEAD_DIM)
    cache_v4 = cache_v.reshape(n_phys, page, N_KV_HEADS, HEAD_DIM)

    xs = x_sample.reshape(db, d)
    cos_s, sin_s = _rope_tables(jnp.full((db,), past, jnp.int32))
    proj_s, w_in_b = _inproj(xs, ln1[0], w_in[0], cos_s, sin_s, rope_cols)
    q_s = proj_s[:, :qw]
    k_s = proj_s[:, k_col:k_col + kvw]
    v_s = proj_s[:, v_col:v_col + kvw]
    mp = batch * seq
    xp = x_prompt.reshape(mp, d)
    cos_p, sin_p = _rope_tables(jnp.arange(seq, dtype=jnp.int32))
    proj_p, _ = _inproj(xp, ln1[0], w_in_b, cos_p, sin_p, rope_cols)

    a_p, kmean_s, k_p, v_p = _attn_prompt(proj_p, batch, seq, k_col, v_col, cache_k3, page_table)
    zero_state = jnp.zeros((batch, SUBLANES, cw), F32)
    cb_p, utail_p = _conv_seq(proj_p, zero_state, conv_w[0], seq, b_col, cw)
    tiles_per_seq = utail_p.shape[0] // batch
    conv_p = utail_p.reshape(batch, tiles_per_seq, SUBLANES, cw)[:, -1, SUBLANES - (CONV_K - 1):, :]
    sel = _sample_gate(q_s.reshape(db, N_HEADS, HEAD_DIM), kmean_s)
    a_s = _sample_attn(q_s.reshape(db, N_HEADS, HEAD_DIM), k_s.reshape(db, N_KV_HEADS, HEAD_DIM),
                       v_s.reshape(db, N_KV_HEADS, HEAD_DIM), cache_k4, cache_v4, page_table,
                       sel.reshape(db, N_HEADS * MOBA_TOPK)).reshape(db, qw).astype(BF16)
    state = state_conv[0].astype(F32)
    cb_s, u_s = _conv_step(proj_s, state.reshape(db, (CONV_K - 1) * cw), conv_w[0], b_col, cw)
    conv_s = jnp.concatenate([state[:, 1:, :], u_s[:, None, :]], axis=1)

    y_s, trunk_bf16 = _trunk_tail(xs, a_s, cb_s, proj_s, *trunk_f32, ga_col, gc_col)
    y_p, _ = _trunk_tail(xp, a_p, cb_p, proj_p, *trunk_bf16, ga_col, gc_col)

    return (
        y_p.reshape(batch, seq, d),
        y_s.reshape(db, 1, d),
        k_p[None],
        v_p[None],
        conv_p[None],
        k_s.reshape(1, db, 1, N_KV_HEADS, HEAD_DIM),
        v_s.reshape(1, db, 1, N_KV_HEADS, HEAD_DIM),
        conv_s[None],
    )
```

```python
import functools

import jax
import jax.numpy as jnp
from jax import lax
from jax.experimental import pallas as pl
from jax.experimental.pallas import tpu as pltpu

F32 = jnp.float32
BF16 = jnp.bfloat16

N_HEADS = 16
HEAD_DIM = 128
N_KV_HEADS = 4
KV_GROUP = N_HEADS // N_KV_HEADS
MOBA_BLOCK = 256
MOBA_TOPK = 3
ROPE_THETA = 10000.0
CONV_K = 3
RMS_EPS = 1e-6
ATTN_SCALE = HEAD_DIM ** -0.5
LOG2E = 1.4426950408889634
MASK_BIAS = -1e30

LANES = 128
SUBLANES = 8
MIB = 1 << 20
ROW_TILE = 1024
WIDE_ROW_TILE = 2048
COL_TILE = 512
PAGES_PER_STEP = 32
ROW_GROUP = 512


VMEM_LIMIT_MIB = {
    "inproj": 56, "attn_prompt": 48, "conv_seq": 32, "conv_step": 32, "merge": 48, "outproj": 48,
    "ffn_up": 48, "ffn_down": 48, "sample_gate": 32, "sample_attn": 40,
}


def _cparams(name, semantics):
    return pltpu.CompilerParams(dimension_semantics=semantics, vmem_limit_bytes=VMEM_LIMIT_MIB[name] * MIB)


def _row_tile(m, cap=ROW_TILE):
    t = min(m, cap)
    assert m % t == 0 and t % SUBLANES == 0, (m, t)
    return t


def _rmsnorm_f32(x, g):
    return x * lax.rsqrt(jnp.mean(x * x, axis=-1, keepdims=True) + RMS_EPS) * g


def _mxu_weight(w_ref, wb_refs):
    if not wb_refs:
        return w_ref[...]
    w = w_ref[...].astype(BF16)
    wb_refs[0][...] = w
    return w


def _weight_out(w, block_shape, index_map):
    if w.dtype == BF16:
        return [], []
    assert w.dtype == F32
    return [pl.BlockSpec(block_shape, index_map)], [jax.ShapeDtypeStruct(w.shape, BF16)]


def _inproj_kernel(x_ref, g_ref, w_ref, cos_ref, sin_ref, o_ref, *rest, rope_tiles):
    *wb_refs, xn_ref = rest
    j = pl.program_id(1)

    @pl.when(j == 0)
    def _():
        xn_ref[...] = _rmsnorm_f32(x_ref[...], g_ref[...]).astype(BF16)

    acc = jnp.dot(xn_ref[...], _mxu_weight(w_ref, wb_refs), preferred_element_type=F32)

    @pl.when(j < rope_tiles)
    def _():
        cos = cos_ref[...]
        sin = sin_ref[...]
        for c in range(acc.shape[1] // HEAD_DIM):
            a = acc[:, c * HEAD_DIM:(c + 1) * HEAD_DIM]
            o_ref[:, c * HEAD_DIM:(c + 1) * HEAD_DIM] = a * cos + pltpu.roll(a, HEAD_DIM // 2, axis=1) * sin

    @pl.when(j >= rope_tiles)
    def _():
        o_ref[...] = acc


def _block_means(get_page, n_pages, page_rows):
    pages_per_block = MOBA_BLOCK * N_KV_HEADS // page_rows
    blocks = n_pages // pages_per_block
    row = lax.broadcasted_iota(jnp.int32, (blocks, HEAD_DIM), 0)
    means = [jnp.zeros((blocks, HEAD_DIM), F32) for _ in range(N_KV_HEADS)]
    for r in range(blocks):
        tot = jnp.zeros((SUBLANES, HEAD_DIM), F32)
        for t in range(pages_per_block):
            x = get_page(r * pages_per_block + t)
            tot = tot + jnp.sum(x.reshape(page_rows // SUBLANES, SUBLANES, HEAD_DIM), axis=0)
        head_sum = tot[0:N_KV_HEADS, :]
        for c in range(1, SUBLANES // N_KV_HEADS):
            head_sum = head_sum + tot[c * N_KV_HEADS:(c + 1) * N_KV_HEADS, :]
        head_mean = head_sum * (1.0 / MOBA_BLOCK)
        for g in range(N_KV_HEADS):
            means[g] = jnp.where(row == r, head_mean[g:g + 1, :], means[g])
    return means


class _KeyStream:
    def __init__(self, pt_ref, kc_ref, km_ref, pbuf, psem, step, chunks_per_seq, n_chunks, every_step):
        self.pt_ref, self.kc_ref, self.km_ref, self.pbuf, self.psem = pt_ref, kc_ref, km_ref, pbuf, psem
        self.step, self.chunks_per_seq, self.n_chunks = step, chunks_per_seq, n_chunks
        self.every_step = every_step
        self.slot = step % 2
        self.pages = pbuf.shape[1]

    def _copies(self, s, sl):
        b, c = s // self.chunks_per_seq, s % self.chunks_per_seq
        return [pltpu.make_async_copy(self.kc_ref.at[self.pt_ref[b, c * self.pages + t]],
                                      self.pbuf.at[sl, t], self.psem.at[sl])
                for t in range(self.pages)]

    def prefetch(self):
        @pl.when(self.step == 0)
        def _():
            for c in self._copies(self.step, self.slot):
                c.start()

        @pl.when(self.step + 1 < self.n_chunks)
        def _():
            for c in self._copies(self.step + 1, 1 - self.slot):
                c.start()

    def reduce(self):
        def body():
            for c in self._copies(self.step, self.slot):
                c.wait()
            means = _block_means(lambda t: self.pbuf[self.slot, t], self.pages, self.pbuf.shape[2])
            for g in range(N_KV_HEADS):
                self.km_ref[0, g] = means[g]

        if self.every_step:
            body()
        else:
            pl.when(self.step < self.n_chunks)(body)


def _key_stream_specs(cache_k3, page_table, n_steps, step_of):
    db, n_pages = page_table.shape
    _, page_rows, hd = cache_k3.shape
    page = page_rows // N_KV_HEADS
    pages = PAGES_PER_STEP
    assert MOBA_BLOCK % page == 0 and n_pages % pages == 0 and SUBLANES % N_KV_HEADS == 0
    blocks = pages * page // MOBA_BLOCK
    assert blocks % SUBLANES == 0 and hd == HEAD_DIM
    chunks_per_seq = n_pages // pages
    n_chunks = db * chunks_per_seq
    nb = n_pages * page // MOBA_BLOCK
    assert n_chunks <= n_steps, "not enough host grid steps to stream the key cache"

    def km_map(*ids_and_pt):
        s = jnp.minimum(step_of(*ids_and_pt[:-1]), n_chunks - 1)
        return (s // chunks_per_seq, 0, s % chunks_per_seq, 0)

    return (dict(chunks_per_seq=chunks_per_seq, n_chunks=n_chunks, every_step=n_chunks == n_steps),
            pl.BlockSpec((1, N_KV_HEADS, blocks, HEAD_DIM), km_map),
            jax.ShapeDtypeStruct((db, N_KV_HEADS, nb, HEAD_DIM), F32),
            [pltpu.VMEM((2, pages, page_rows, HEAD_DIM), F32), pltpu.SemaphoreType.DMA((2,))])


def _inproj(x, ln1, w_in, cos, sin, rope_cols):
    m, d = x.shape
    n = w_in.shape[1]
    tm, tn = _row_tile(m, WIDE_ROW_TILE), COL_TILE
    assert n % tn == 0 and rope_cols % tn == 0
    assert cos.shape[0] % tm == 0 and m % cos.shape[0] == 0
    table_tiles = cos.shape[0] // tm
    wb_specs, wb_shapes = _weight_out(w_in, (d, tn), lambda i, j: (0, j))
    assert not wb_specs or m == tm, "the bf16 weight copy is written once per weight tile"
    out = pl.pallas_call(
        functools.partial(_inproj_kernel, rope_tiles=rope_cols // tn),
        grid=(m // tm, n // tn),
        in_specs=[
            pl.BlockSpec((tm, d), lambda i, j: (i, 0), pipeline_mode=pl.Buffered(1)),
            pl.BlockSpec((1, d), lambda i, j: (0, 0)),
            pl.BlockSpec((d, tn), lambda i, j: (0, j)),
            pl.BlockSpec((tm, HEAD_DIM), lambda i, j: (i % table_tiles, 0)),
            pl.BlockSpec((tm, HEAD_DIM), lambda i, j: (i % table_tiles, 0)),
        ],
        out_specs=[pl.BlockSpec((tm, tn), lambda i, j: (i, j))] + wb_specs,
        out_shape=[jax.ShapeDtypeStruct((m, n), F32)] + wb_shapes,
        scratch_shapes=[pltpu.VMEM((tm, d), BF16)],
        compiler_params=_cparams("inproj", ("parallel", "arbitrary")),
        name="inproj",
    )(x, ln1.reshape(1, d), w_in, cos, sin)
    return out[0], (out[1] if wb_specs else w_in)


def _attn_prompt_kernel(pt_ref, q_ref, k_ref, v_ref, kc_ref, o_ref, km_ref, ko_ref, vo_ref, ka_ref, va_ref,
                        kmean_ref, qa_ref, m_ref, acc_ref, pbuf, psem, osem, *, chunks_per_seq, n_chunks,
                        every_step):
    qi = pl.program_id(2)
    seq = k_ref.shape[0]
    nb = seq // MOBA_BLOCK
    rows = KV_GROUP * MOBA_BLOCK
    pair = 2 * MOBA_BLOCK
    step = (pl.program_id(0) * pl.num_programs(1) + pl.program_id(1)) * pl.num_programs(2) + qi
    stream = _KeyStream(pt_ref, kc_ref, km_ref, pbuf, psem, step, chunks_per_seq, n_chunks, every_step)
    stream.prefetch()

    def kv_out_copies():
        b, g = pl.program_id(0), pl.program_id(1)
        return [pltpu.make_async_copy(k_ref, ko_ref.at[b, :, g, :], osem.at[0]),
                pltpu.make_async_copy(v_ref, vo_ref.at[b, :, g, :], osem.at[1])]

    @pl.when(qi == 0)
    def _():
        for c in kv_out_copies():
            c.start()

    @pl.when(qi == pl.num_programs(2) - 1)
    def _():
        for c in kv_out_copies():
            c.wait()

    @pl.when(qi == 0)
    def _():
        k = k_ref[...]
        row_blk = lax.broadcasted_iota(jnp.int32, (seq, LANES), 0) // MOBA_BLOCK
        lane = lax.broadcasted_iota(jnp.int32, (seq, LANES), 1)
        ka_ref[:, 0:HEAD_DIM] = k.astype(BF16)
        ka_ref[:, HEAD_DIM:] = jnp.where(lane == row_blk, 1.0, 0.0).astype(BF16)
        va_ref[:, 0:HEAD_DIM] = v_ref[...].astype(BF16)
        va_ref[:, HEAD_DIM:] = jnp.ones((seq, LANES), BF16)
        kmean = jnp.mean(k.reshape(nb, MOBA_BLOCK, HEAD_DIM), axis=1)
        km_hi = kmean.astype(BF16)
        km_lo = (kmean - km_hi.astype(F32)).astype(BF16)
        kmean_ref[...] = jnp.zeros_like(kmean_ref)
        kmean_ref[0:nb, :] = jnp.concatenate([km_hi, km_hi, km_lo], axis=1)

    stream.reduce()

    q = q_ref[...]
    q4 = jnp.concatenate([q[:, h * HEAD_DIM:(h + 1) * HEAD_DIM] for h in range(KV_GROUP)], axis=0)
    qs = (q4 * (ATTN_SCALE * LOG2E)).astype(BF16)

    q_hi = q4.astype(BF16)
    q_lo = (q4 - q_hi.astype(F32)).astype(BF16)
    gate = lax.dot_general(jnp.concatenate([q_hi, q_lo, q_hi], axis=1), kmean_ref[...],
                           (((1,), (1,)), ((), ())), preferred_element_type=F32)
    lane_i = lax.broadcasted_iota(jnp.int32, (rows, LANES), 1)
    lane = lane_i.astype(F32)
    past = lane_i < qi
    gate = jnp.where(past, gate, -jnp.inf)
    bias = jnp.full((rows, LANES), MASK_BIAS, F32)
    for _ in range(MOBA_TOPK):
        top = jnp.max(gate, axis=1, keepdims=True)
        pick = lane == jnp.min(jnp.where(gate == top, lane, float(LANES)), axis=1, keepdims=True)
        bias = jnp.where(pick & past, 0.0, bias)
        gate = jnp.where(pick, -jnp.inf, gate)
    qa = jnp.concatenate([qs, bias.astype(BF16)], axis=1)

    own = pl.multiple_of(qi * MOBA_BLOCK, MOBA_BLOCK)
    s = lax.dot_general(qs, ka_ref[pl.ds(own, MOBA_BLOCK), 0:HEAD_DIM], (((1,), (1,)), ((), ())),
                        preferred_element_type=F32)
    qrow = lax.broadcasted_iota(jnp.int32, (rows, MOBA_BLOCK), 0) & (MOBA_BLOCK - 1)
    kcol = lax.broadcasted_iota(jnp.int32, (rows, MOBA_BLOCK), 1)
    s = jnp.where(kcol <= qrow, s, MASK_BIAS)
    m0 = jnp.broadcast_to(jnp.max(s, axis=1, keepdims=True), (rows, LANES))
    p = jnp.concatenate([jnp.exp2(s[:, c * LANES:(c + 1) * LANES] - m0)
                         for c in range(MOBA_BLOCK // LANES)], axis=1)
    m_ref[...] = m0
    acc_ref[...] = jnp.dot(p.astype(BF16), va_ref[pl.ds(own, MOBA_BLOCK), :], preferred_element_type=F32)

    qa_ref[...] = qa

    def past_pair(t):
        start = pl.multiple_of(t * pair, pair)
        groups = [slice(r0, r0 + ROW_GROUP) for r0 in range(0, rows, ROW_GROUP)]

        def scores(rs):
            return lax.dot_general(qa_ref[rs, :], ka_ref[pl.ds(start, pair), :], (((1,), (1,)), ((), ())),
                                   preferred_element_type=F32)

        m_olds = [m_ref[rs, :] for rs in groups]
        s_next = scores(groups[0])
        updates = []
        for gi, rs in enumerate(groups):
            s = s_next
            if gi + 1 < len(groups):
                s_next = scores(groups[gi + 1])
            m_old = m_olds[gi]
            m_new = jnp.maximum(m_old, jnp.max(s, axis=1, keepdims=True))
            alpha = jnp.exp2(m_old - m_new)
            p = jnp.concatenate([jnp.exp2(s[:, c * LANES:(c + 1) * LANES] - m_new)
                                 for c in range(pair // LANES)], axis=1)
            pv = jnp.dot(p.astype(BF16), va_ref[pl.ds(start, pair), :], preferred_element_type=F32)
            updates.append((rs, m_new, alpha, pv))
        for rs, m_new, alpha, pv in updates:
            for c in range(2):
                cs = slice(c * LANES, (c + 1) * LANES)
                acc_ref[rs, cs] = alpha * acc_ref[rs, cs] + pv[:, cs]
            m_ref[rs, :] = m_new

    def two_pairs(t, carry):
        past_pair(2 * t)
        past_pair(2 * t + 1)
        return carry

    n_pairs = (qi + 1) // 2
    lax.fori_loop(0, n_pairs // 2, two_pairs, 0)

    @pl.when(n_pairs % 2 == 1)
    def _():
        past_pair(n_pairs - 1)

    o = acc_ref[:, 0:HEAD_DIM] / acc_ref[:, HEAD_DIM:]
    for h in range(KV_GROUP):
        o_ref[:, h * HEAD_DIM:(h + 1) * HEAD_DIM] = o[h * MOBA_BLOCK:(h + 1) * MOBA_BLOCK].astype(o_ref.dtype)


def _attn_prompt(proj, batch, seq, k_col, v_col, cache_k3, page_table):
    nq = seq // MOBA_BLOCK
    assert seq % (2 * MOBA_BLOCK) == 0 and MOBA_TOPK <= nq <= LANES
    gw = KV_GROUP * HEAD_DIM
    rows = KV_GROUP * MOBA_BLOCK
    kblk, vblk = k_col // HEAD_DIM, v_col // HEAD_DIM
    grid = (batch, N_KV_HEADS, nq)
    stream_kwargs, km_spec, km_shape, stream_scratch = _key_stream_specs(
        cache_k3, page_table, batch * N_KV_HEADS * nq, lambda b, g, i: (b * N_KV_HEADS + g) * nq + i)
    grid_spec = pltpu.PrefetchScalarGridSpec(
        num_scalar_prefetch=1,
        grid=grid,
        in_specs=[
            pl.BlockSpec((MOBA_BLOCK, gw), lambda b, g, i, pt: (b * nq + i, g)),
            pl.BlockSpec((seq, HEAD_DIM), lambda b, g, i, pt: (b, kblk + g)),
            pl.BlockSpec((seq, HEAD_DIM), lambda b, g, i, pt: (b, vblk + g)),
            pl.BlockSpec(memory_space=pl.ANY),
        ],
        out_specs=[pl.BlockSpec((MOBA_BLOCK, gw), lambda b, g, i, pt: (b * nq + i, g)), km_spec,
                   pl.BlockSpec(memory_space=pl.ANY), pl.BlockSpec(memory_space=pl.ANY)],
        scratch_shapes=[
            pltpu.VMEM((seq, HEAD_DIM + LANES), BF16),
            pltpu.VMEM((seq, HEAD_DIM + LANES), BF16),
            pltpu.VMEM((LANES, 3 * HEAD_DIM), BF16),
            pltpu.VMEM((rows, HEAD_DIM + LANES), BF16),
            pltpu.VMEM((rows, LANES), F32),
            pltpu.VMEM((rows, HEAD_DIM + LANES), F32),
        ] + stream_scratch + [pltpu.SemaphoreType.DMA((2,))],
    )
    kv_shape = jax.ShapeDtypeStruct((batch, seq, N_KV_HEADS, HEAD_DIM), proj.dtype)
    return pl.pallas_call(
        functools.partial(_attn_prompt_kernel, **stream_kwargs),
        grid_spec=grid_spec,
        out_shape=[jax.ShapeDtypeStruct((batch * seq, N_HEADS * HEAD_DIM), BF16), km_shape, kv_shape, kv_shape],
        compiler_params=_cparams("attn_prompt", ("arbitrary", "arbitrary", "arbitrary")),
        name="attn_prompt",
    )(page_table, proj, proj, proj, cache_k3)


def _conv_seq_kernel(b_ref, c_ref, h_ref, cp_ref, hp_ref, st_ref, w_ref, cb_ref, ut_ref, *, seq):
    i = pl.program_id(0)
    tm = c_ref.shape[0]
    u = c_ref[...] * h_ref[...]
    prev = jnp.where((i * tm) % seq == 0, st_ref[0], cp_ref[...] * hp_ref[...])
    p1 = prev[SUBLANES - 1:SUBLANES, :]
    p2 = prev[SUBLANES - 2:SUBLANES - 1, :]
    row = lax.broadcasted_iota(jnp.int32, u.shape, 0)
    u1 = jnp.where(row == 0, p1, pltpu.roll(u, 1, axis=0))
    u2 = jnp.where(row == 0, p2, jnp.where(row == 1, p1, pltpu.roll(u, 2, axis=0)))
    w = w_ref[...]
    conv = w[0:1, :] * u2 + w[1:2, :] * u1 + w[2:3, :] * u
    cb_ref[...] = (b_ref[...] * conv).astype(BF16)
    ut_ref[0] = u[tm - SUBLANES:tm, :]


def _conv_seq(proj, state8, conv_w, seq, b_col, cw):
    m = proj.shape[0]
    tm = _row_tile(seq, 512)
    nt = m // tm
    cb = b_col // cw
    pstep = tm // SUBLANES
    return pl.pallas_call(
        functools.partial(_conv_seq_kernel, seq=seq),
        grid=(nt,),
        in_specs=[
            pl.BlockSpec((tm, cw), lambda i: (i, cb)),
            pl.BlockSpec((tm, cw), lambda i: (i, cb + 1)),
            pl.BlockSpec((tm, cw), lambda i: (i, cb + 2)),
            pl.BlockSpec((SUBLANES, cw), lambda i: (jnp.maximum(i * pstep - 1, 0), cb + 1)),
            pl.BlockSpec((SUBLANES, cw), lambda i: (jnp.maximum(i * pstep - 1, 0), cb + 2)),
            pl.BlockSpec((1, SUBLANES, cw), lambda i: ((i * tm) // seq, 0, 0)),
            pl.BlockSpec((CONV_K, cw), lambda i: (0, 0)),
        ],
        out_specs=[
            pl.BlockSpec((tm, cw), lambda i: (i, 0)),
            pl.BlockSpec((1, SUBLANES, cw), lambda i: (i, 0, 0)),
        ],
        out_shape=[
            jax.ShapeDtypeStruct((m, cw), BF16),
            jax.ShapeDtypeStruct((nt, SUBLANES, cw), F32),
        ],
        compiler_params=_cparams("conv_seq", ("parallel",)),
        name="conv_seq",
    )(proj, proj, proj, proj, proj, state8, conv_w)


def _conv_step_kernel(b_ref, c_ref, h_ref, st_ref, w_ref, cb_ref, u_ref):
    cw = c_ref.shape[1]
    u = c_ref[...] * h_ref[...]
    w = w_ref[...]
    conv = w[0:1, :] * st_ref[:, 0:cw] + w[1:2, :] * st_ref[:, cw:2 * cw] + w[2:3, :] * u
    cb_ref[...] = (b_ref[...] * conv).astype(BF16)
    u_ref[...] = u


def _conv_step(proj, state, conv_w, b_col, cw):
    m = proj.shape[0]
    cb = b_col // cw
    return pl.pallas_call(
        _conv_step_kernel,
        grid=(1,),
        in_specs=[
            pl.BlockSpec((m, cw), lambda i: (0, cb)),
            pl.BlockSpec((m, cw), lambda i: (0, cb + 1)),
            pl.BlockSpec((m, cw), lambda i: (0, cb + 2)),
            pl.BlockSpec((m, (CONV_K - 1) * cw), lambda i: (0, 0)),
            pl.BlockSpec((CONV_K, cw), lambda i: (0, 0)),
        ],
        out_specs=[pl.BlockSpec((m, cw), lambda i: (0, 0)), pl.BlockSpec((m, cw), lambda i: (0, 0))],
        out_shape=[jax.ShapeDtypeStruct((m, cw), BF16), jax.ShapeDtypeStruct((m, cw), F32)],
        compiler_params=_cparams("conv_step", ("arbitrary",)),
        name="conv_step",
    )(proj, proj, proj, state, conv_w)


def _merge_kernel(a_ref, cb_ref, ga_ref, gc_ref, wa_ref, wc_ref, o_ref, *wb_refs):
    ya = jnp.dot(a_ref[...], _mxu_weight(wa_ref, wb_refs[0:1]), preferred_element_type=F32)
    yc = jnp.dot(cb_ref[...], _mxu_weight(wc_ref, wb_refs[1:2]), preferred_element_type=F32)
    o_ref[...] = (jax.nn.sigmoid(ga_ref[...]) * ya + jax.nn.sigmoid(gc_ref[...]) * yc).astype(o_ref.dtype)


def _merge(a, cb, proj, w_attn_br, w_conv_br, ga_col, gc_col):
    m, qw = a.shape
    cw = cb.shape[1]
    d = w_attn_br.shape[1]
    tm, tn = _row_tile(m), COL_TILE
    ga, gc = ga_col // tn, gc_col // tn
    assert w_attn_br.dtype == w_conv_br.dtype
    wa_specs, wa_shapes = _weight_out(w_attn_br, (qw, tn), lambda i, j: (0, j))
    wc_specs, wc_shapes = _weight_out(w_conv_br, (cw, tn), lambda i, j: (0, j))
    assert not wa_specs or m == tm, "the bf16 weight copy is written once per weight tile"
    out = pl.pallas_call(
        _merge_kernel,
        grid=(m // tm, d // tn),
        in_specs=[
            pl.BlockSpec((tm, qw), lambda i, j: (i, 0)),
            pl.BlockSpec((tm, cw), lambda i, j: (i, 0)),
            pl.BlockSpec((tm, tn), lambda i, j: (i, ga + j)),
            pl.BlockSpec((tm, tn), lambda i, j: (i, gc + j)),
            pl.BlockSpec((qw, tn), lambda i, j: (0, j)),
            pl.BlockSpec((cw, tn), lambda i, j: (0, j)),
        ],
        out_specs=[pl.BlockSpec((tm, tn), lambda i, j: (i, j))] + wa_specs + wc_specs,
        out_shape=[jax.ShapeDtypeStruct((m, d), BF16)] + wa_shapes + wc_shapes,
        compiler_params=_cparams("merge", ("parallel", "parallel")),
        name="merge",
    )(a, cb, proj, proj, w_attn_br, w_conv_br)
    return out[0], ((out[1], out[2]) if wa_specs else (w_attn_br, w_conv_br))


def _outproj_kernel(m_ref, w_ref, x_ref, o_ref, *wb_refs):
    o_ref[...] = x_ref[...] + jnp.dot(m_ref[...], _mxu_weight(w_ref, wb_refs), preferred_element_type=F32)


def _outproj(merged, w_o, x):
    m, d = merged.shape
    n = w_o.shape[1]
    tm, tn = _row_tile(m, WIDE_ROW_TILE), COL_TILE
    wb_specs, wb_shapes = _weight_out(w_o, (d, tn), lambda i, j: (0, j))
    assert not wb_specs or m == tm, "the bf16 weight copy is written once per weight tile"
    out = pl.pallas_call(
        _outproj_kernel,
        grid=(m // tm, n // tn),
        in_specs=[
            pl.BlockSpec((tm, d), lambda i, j: (i, 0), pipeline_mode=pl.Buffered(1)),
            pl.BlockSpec((d, tn), lambda i, j: (0, j)),
            pl.BlockSpec((tm, tn), lambda i, j: (i, j)),
        ],
        out_specs=[pl.BlockSpec((tm, tn), lambda i, j: (i, j))] + wb_specs,
        out_shape=[jax.ShapeDtypeStruct((m, n), F32)] + wb_shapes,
        compiler_params=_cparams("outproj", ("parallel", "parallel")),
        name="outproj",
    )(merged, w_o, x)
    return out[0], (out[1] if wb_specs else w_o)


def _ffn_up_kernel(x_ref, g_ref, wg_ref, wu_ref, o_ref, *rest):
    *wb_refs, hn_ref = rest

    @pl.when(pl.program_id(1) == 0)
    def _():
        hn_ref[...] = _rmsnorm_f32(x_ref[...], g_ref[...]).astype(BF16)

    hn = hn_ref[...]
    gate = jnp.dot(hn, _mxu_weight(wg_ref, wb_refs[0:1]), preferred_element_type=F32)
    up = jnp.dot(hn, _mxu_weight(wu_ref, wb_refs[1:2]), preferred_element_type=F32)
    o_ref[...] = (gate * jax.nn.sigmoid(gate) * up).astype(o_ref.dtype)


def _ffn_up(x, ln2, w_gate, w_up):
    m, d = x.shape
    f = w_gate.shape[1]
    tm, tn = _row_tile(m), COL_TILE
    assert f % tn == 0 and w_gate.dtype == w_up.dtype
    wg_specs, wg_shapes = _weight_out(w_gate, (d, tn), lambda i, j: (0, j))
    wu_specs, wu_shapes = _weight_out(w_up, (d, tn), lambda i, j: (0, j))
    assert not wg_specs or m == tm, "the bf16 weight copy is written once per weight tile"
    out = pl.pallas_call(
        _ffn_up_kernel,
        grid=(m // tm, f // tn),
        in_specs=[
            pl.BlockSpec((tm, d), lambda i, j: (i, 0)),
            pl.BlockSpec((1, d), lambda i, j: (0, 0)),
            pl.BlockSpec((d, tn), lambda i, j: (0, j)),
            pl.BlockSpec((d, tn), lambda i, j: (0, j)),
        ],
        out_specs=[pl.BlockSpec((tm, tn), lambda i, j: (i, j))] + wg_specs + wu_specs,
        out_shape=[jax.ShapeDtypeStruct((m, f), BF16)] + wg_shapes + wu_shapes,
        scratch_shapes=[pltpu.VMEM((tm, d), BF16)],
        compiler_params=_cparams("ffn_up", ("parallel", "arbitrary")),
        name="ffn_up",
    )(x, ln2.reshape(1, d), w_gate, w_up)
    return out[0], ((out[1], out[2]) if wg_specs else (w_gate, w_up))


def _ffn_down_kernel(h_ref, w_ref, x_ref, g_ref, o_ref, *wb_refs):
    k = pl.program_id(1)

    @pl.when(k == 0)
    def _():
        o_ref[...] = x_ref[...]

    o_ref[...] += jnp.dot(h_ref[...], _mxu_weight(w_ref, wb_refs), preferred_element_type=F32)

    @pl.when(k == pl.num_programs(1) - 1)
    def _():
        o_ref[...] = _rmsnorm_f32(o_ref[...], g_ref[...])


def _ffn_down(hmid, w_down, x, ln_f):
    m, f = hmid.shape
    d = w_down.shape[1]
    tm, tk = _row_tile(m), COL_TILE
    wb_specs, wb_shapes = _weight_out(w_down, (tk, d), lambda i, k: (k, 0))
    assert not wb_specs or m == tm, "the bf16 weight copy is written once per weight tile"
    out = pl.pallas_call(
        _ffn_down_kernel,
        grid=(m // tm, f // tk),
        in_specs=[
            pl.BlockSpec((tm, tk), lambda i, k: (i, k)),
            pl.BlockSpec((tk, d), lambda i, k: (k, 0)),
            pl.BlockSpec((tm, d), lambda i, k: (i, 0)),
            pl.BlockSpec((1, d), lambda i, k: (0, 0)),
        ],
        out_specs=[pl.BlockSpec((tm, d), lambda i, k: (i, 0))] + wb_specs,
        out_shape=[jax.ShapeDtypeStruct((m, d), F32)] + wb_shapes,
        compiler_params=_cparams("ffn_down", ("parallel", "arbitrary")),
        name="ffn_down",
    )(hmid, w_down, x, ln_f.reshape(1, d))
    return out[0], (out[1] if wb_specs else w_down)


def _sample_gate_kernel(q_ref, km_ref, sel_ref):
    db, _, nb, _ = km_ref.shape
    head_group = lax.broadcasted_iota(jnp.int32, (N_HEADS, nb), 0) // KV_GROUP
    blk = lax.broadcasted_iota(jnp.int32, (N_HEADS, nb), 1).astype(F32)
    lane = lax.broadcasted_iota(jnp.int32, (N_HEADS, LANES), 1)

    def one_sequence(b, carry):
        q = q_ref[b]
        gate = jnp.zeros((N_HEADS, nb), F32)
        for g in range(N_KV_HEADS):
            gg = lax.dot_general(q, km_ref[b, g], (((1,), (1,)), ((), ())),
                                 precision=lax.Precision.HIGHEST, preferred_element_type=F32)
            gate = jnp.where(head_group == g, gg, gate)
        out = jnp.zeros((N_HEADS, LANES), F32)
        for r in range(MOBA_TOPK):
            top = jnp.max(gate, axis=1, keepdims=True)
            first = jnp.min(jnp.where(gate == top, blk, float(nb)), axis=1, keepdims=True)
            out = jnp.where(lane == r, first, out)
            gate = jnp.where(blk == first, -jnp.inf, gate)
        sel_ref[b] = out.astype(jnp.int32)
        return carry

    lax.fori_loop(0, db, one_sequence, 0, unroll=2)


def _sample_gate(q_s, kmean):
    db, _, nb, _ = kmean.shape
    assert MOBA_TOPK <= nb
    sel = pl.pallas_call(
        _sample_gate_kernel,
        grid=(1,),
        in_specs=[pl.BlockSpec((db, N_HEADS, HEAD_DIM), lambda i: (0, 0, 0)),
                  pl.BlockSpec((db, N_KV_HEADS, nb, HEAD_DIM), lambda i: (0, 0, 0, 0))],
        out_specs=pl.BlockSpec((db, N_HEADS, LANES), lambda i: (0, 0, 0)),
        out_shape=jax.ShapeDtypeStruct((db, N_HEADS, LANES), jnp.int32),
        compiler_params=_cparams("sample_gate", ("arbitrary",)),
        name="sample_gate",
    )(q_s, kmean)
    return sel[:, :, :MOBA_TOPK]


def _sample_attn_kernel(pt_ref, sel_ref, q_ref, ks_ref, vs_ref, kc_ref, vc_ref, o_ref, kbuf, vbuf, sem, *,
                        pages_per_block):
    b = pl.program_id(0)
    slot = b % 2
    _, n_heads, n_sel, page, hd = kbuf.shape

    def head_copies(bb, h, sl):
        g = h // KV_GROUP
        out = []
        for t in range(n_sel):
            blk = sel_ref[bb, h * MOBA_TOPK + t // pages_per_block]
            pg = pt_ref[bb, blk * pages_per_block + t % pages_per_block]
            out.append(pltpu.make_async_copy(kc_ref.at[pg, :, g, :], kbuf.at[sl, h, t], sem.at[0, sl, h]))
            out.append(pltpu.make_async_copy(vc_ref.at[pg, :, g, :], vbuf.at[sl, h, t], sem.at[1, sl, h]))
        return out

    def start_all(bb, sl):
        def start_head(h, carry):
            for n, c in enumerate(head_copies(bb, h, sl)):
                c.start(priority=n % 2)
            return carry
        lax.fori_loop(0, n_heads, start_head, 0)

    @pl.when(b == 0)
    def _():
        start_all(b, slot)

    @pl.when(b + 1 < pl.num_programs(0))
    def _():
        start_all(b + 1, 1 - slot)

    ones = jnp.ones((hd, LANES), BF16)

    def one_head(h):
        g = h // KV_GROUP
        q = q_ref[0, pl.ds(h, 1), :] * ATTN_SCALE
        prod = kbuf[slot, h].reshape(n_sel * page, hd) * q
        s = jnp.dot(prod.astype(BF16), ones, preferred_element_type=F32)
        s_new = jnp.sum(ks_ref[0, pl.ds(g, 1), :] * q, axis=1, keepdims=True)
        m = jnp.maximum(jnp.max(s, axis=0, keepdims=True), s_new)
        p = jnp.exp(s - m)
        p_new = jnp.exp(s_new - m)
        l = jnp.sum(p, axis=0, keepdims=True) + p_new
        o = jnp.sum(p * vbuf[slot, h].reshape(n_sel * page, hd), axis=0, keepdims=True)
        o = o + p_new * vs_ref[0, pl.ds(g, 1), :]
        o_ref[0, pl.ds(h, 1), :] = (o / l).astype(o_ref.dtype)

    heads_per_iter = 2 * KV_GROUP

    def head_group(g, carry):
        for j in range(heads_per_iter):
            for c in head_copies(b, g * heads_per_iter + j, slot):
                c.wait()
        for j in range(heads_per_iter):
            one_head(g * heads_per_iter + j)
        return carry

    lax.fori_loop(0, n_heads // heads_per_iter, head_group, 0)


def _sample_attn(q_s, k_s, v_s, cache_k4, cache_v4, page_table, sel):
    db, n_pages = page_table.shape
    page = cache_k4.shape[1]
    ppb = MOBA_BLOCK // page
    n_sel = MOBA_TOPK * ppb
    assert HEAD_DIM == LANES

    grid_spec = pltpu.PrefetchScalarGridSpec(
        num_scalar_prefetch=2,
        grid=(db,),
        in_specs=[
            pl.BlockSpec((1, N_HEADS, HEAD_DIM), lambda b, pt, sl: (b, 0, 0)),
            pl.BlockSpec((1, N_KV_HEADS, HEAD_DIM), lambda b, pt, sl: (b, 0, 0)),
            pl.BlockSpec((1, N_KV_HEADS, HEAD_DIM), lambda b, pt, sl: (b, 0, 0)),
            pl.BlockSpec(memory_space=pl.ANY),
            pl.BlockSpec(memory_space=pl.ANY),
        ],
        out_specs=pl.BlockSpec((1, N_HEADS, HEAD_DIM), lambda b, pt, sl: (b, 0, 0)),
        scratch_shapes=[
            pltpu.VMEM((2, N_HEADS, n_sel, page, HEAD_DIM), F32),
            pltpu.VMEM((2, N_HEADS, n_sel, page, HEAD_DIM), F32),
            pltpu.SemaphoreType.DMA((2, 2, N_HEADS)),
        ],
    )
    return pl.pallas_call(
        functools.partial(_sample_attn_kernel, pages_per_block=ppb),
        grid_spec=grid_spec,
        out_shape=jax.ShapeDtypeStruct((db, N_HEADS, HEAD_DIM), F32),
        compiler_params=_cparams("sample_attn", ("arbitrary",)),
        name="sample_attn",
    )(page_table, sel, q_s, k_s, v_s, cache_k4, cache_v4)


def _rope_tables(pos):
    half = HEAD_DIM // 2
    inv = 1.0 / (ROPE_THETA ** (jnp.arange(half, dtype=F32) / half))
    ang = pos.astype(F32)[:, None] * inv[None, :]
    cos, sin = jnp.cos(ang), jnp.sin(ang)
    return jnp.concatenate([cos, cos], axis=-1), jnp.concatenate([-sin, sin], axis=-1)


def _trunk_tail(x, a, cb, proj, w_attn_br, w_conv_br, w_o, ln2, w_ff_gate, w_ff_up, w_ff_down, ln_f,
                ga_col, gc_col):
    merged, (w_attn_br, w_conv_br) = _merge(a, cb, proj, w_attn_br, w_conv_br, ga_col, gc_col)
    x1, w_o = _outproj(merged, w_o, x)
    hmid, (w_ff_gate, w_ff_up) = _ffn_up(x1, ln2, w_ff_gate, w_ff_up)
    y, w_ff_down = _ffn_down(hmid, w_ff_down, x1, ln_f)
    return y, (w_attn_br, w_conv_br, w_o, ln2, w_ff_gate, w_ff_up, w_ff_down, ln_f)


def kernel(x_prompt, x_sample, cache_k, cache_v, state_conv, page_table, ln1, w_in, conv_w, w_attn_br,
           w_conv_br, w_o, ln2, w_ff_gate, w_ff_up, w_ff_down, ln_f):
    batch, seq, d = x_prompt.shape
    db, dec_seq, _ = x_sample.shape
    depth, n_phys, page, kvh, hd = cache_k.shape
    assert depth == 1 and dec_seq == 1 and kvh == N_KV_HEADS and hd == HEAD_DIM
    qw, kvw = N_HEADS * HEAD_DIM, N_KV_HEADS * HEAD_DIM
    cw = conv_w.shape[-1]
    k_col, v_col, b_col = qw, qw + kvw, qw + 2 * kvw
    ga_col = b_col + 3 * cw
    gc_col = ga_col + d
    rope_cols = qw + kvw
    past = page_table.shape[1] * page
    trunk_f32 = (w_attn_br[0], w_conv_br[0], w_o[0], ln2[0], w_ff_gate[0], w_ff_up[0], w_ff_down[0], ln_f)
    cache_k3 = cache_k.reshape(n_phys, page * N_KV_HEADS, HEAD_DIM)
    cache_k4 = cache_k.reshape(n_phys, page, N_KV_HEADS, HEAD_DIM)
    cache_v4 = cache_v.reshape(n_phys, page, N_KV_HEADS, HEAD_DIM)

    xs = x_sample.reshape(db, d)
    cos_s, sin_s = _rope_tables(jnp.full((db,), past, jnp.int32))
    proj_s, w_in_b = _inproj(xs, ln1[0], w_in[0], cos_s, sin_s, rope_cols)
    q_s = proj_s[:, :qw]
    k_s = proj_s[:, k_col:k_col + kvw]
    v_s = proj_s[:, v_col:v_col + kvw]
    mp = batch * seq
    xp = x_prompt.reshape(mp, d)
    cos_p, sin_p = _rope_tables(jnp.arange(seq, dtype=jnp.int32))
    proj_p, _ = _inproj(xp, ln1[0], w_in_b, cos_p, sin_p, rope_cols)

    a_p, kmean_s, k_p, v_p = _attn_prompt(proj_p, batch, seq, k_col, v_col, cache_k3, page_table)
    zero_state = jnp.zeros((batch, SUBLANES, cw), F32)
    cb_p, utail_p = _conv_seq(proj_p, zero_state, conv_w[0], seq, b_col, cw)
    tiles_per_seq = utail_p.shape[0] // batch
    conv_p = utail_p.reshape(batch, tiles_per_seq, SUBLANES, cw)[:, -1, SUBLANES - (CONV_K - 1):, :]
    sel = _sample_gate(q_s.reshape(db, N_HEADS, HEAD_DIM), kmean_s)
    a_s = _sample_attn(q_s.reshape(db, N_HEADS, HEAD_DIM), k_s.reshape(db, N_KV_HEADS, HEAD_DIM),
                       v_s.reshape(db, N_KV_HEADS, HEAD_DIM), cache_k4, cache_v4, page_table,
                       sel.reshape(db, N_HEADS * MOBA_TOPK)).reshape(db, qw).astype(BF16)
    state = state_conv[0].astype(F32)
    cb_s, u_s = _conv_step(proj_s, state.reshape(db, (CONV_K - 1) * cw), conv_w[0], b_col, cw)
    conv_s = jnp.concatenate([state[:, 1:, :], u_s[:, None, :]], axis=1)

    y_s, trunk_bf16 = _trunk_tail(xs, a_s, cb_s, proj_s, *trunk_f32, ga_col, gc_col)
    y_p, _ = _trunk_tail(xp, a_p, cb_p, proj_p, *trunk_bf16, ga_col, gc_col)

    return (
        y_p.reshape(batch, seq, d),
        y_s.reshape(db, 1, d),
        k_p[None],
        v_p[None],
        conv_p[None],
        k_s.reshape(1, db, 1, N_KV_HEADS, HEAD_DIM),
        v_s.reshape(1, db, 1, N_KV_HEADS, HEAD_DIM),
        conv_s[None],
    )
```
